```python
import jax, jax.numpy as jnp
from jax import lax
import numpy as np

D_MODEL = 1024
BATCH = 2
SEQ = 8192
DEPTH = 2

GRID_W = 64
CTX_LEN = 256
EPS = 1e-6
D_FF = 4 * D_MODEL
SGU_HEADS = 4
SGU_HEAD_DIM = 64
SGU_WIDTH = SGU_HEADS * SGU_HEAD_DIM
SGU_CHUNK = 128
GLA_HEADS = 4
GLA_DK = 32
GLA_DV = 64
GLA_K_WIDTH = GLA_HEADS * GLA_DK
GLA_V_WIDTH = GLA_HEADS * GLA_DV
GLA_GATE_RANK = 16
GLA_TAU = 16.0
GLA_CHUNK = 64
MLA_HEADS = 4
MLA_Q_RANK = 256
MLA_KV_RANK = 256
MLA_NOPE = 128
MLA_ROPE = 64
MLA_V = 128
MLA_WIDTH = MLA_HEADS * MLA_V
ROPE_BASE = 10000.0
ATTN_BLOCK = 128
MIX_WIDTH = SGU_WIDTH + GLA_V_WIDTH + MLA_WIDTH
KV_SPLITS = (GLA_K_WIDTH, GLA_V_WIDTH, GLA_GATE_RANK, GLA_GATE_RANK, MLA_KV_RANK, MLA_ROPE)
Q_SPLITS = (SGU_WIDTH, SGU_WIDTH, GLA_K_WIDTH, GLA_V_WIDTH, MLA_Q_RANK)
KV_COLS = sum(KV_SPLITS)
IN_COLS = KV_COLS + sum(Q_SPLITS)

kernel_name = "hybrid_sgu_gla_mla_prefix_dit_block"


def split_cols(t, widths):
    idx = np.cumsum(widths)[:-1].tolist()
    return jnp.split(t, idx, axis=-1)


def rmsnorm(x, w):
    xf = x.astype(jnp.float32)
    y = xf * lax.rsqrt(jnp.mean(xf * xf, axis=-1, keepdims=True) + EPS)
    return (y * w.astype(jnp.float32)).astype(x.dtype)


def layernorm(x, w, b):
    xf = x.astype(jnp.float32)
    mu = jnp.mean(xf, axis=-1, keepdims=True)
    var = jnp.mean(jnp.square(xf - mu), axis=-1, keepdims=True)
    y = (xf - mu) * lax.rsqrt(var + EPS)
    return (y * w.astype(jnp.float32) + b.astype(jnp.float32)).astype(x.dtype)


def modulate(xn, shift, scale):
    return xn * (1.0 + scale) + shift


def axial_angles(n):
    rows = n // GRID_W
    row = jnp.repeat(jnp.arange(rows), GRID_W).astype(jnp.float32)
    col = jnp.tile(jnp.arange(GRID_W), rows).astype(jnp.float32)
    half = MLA_ROPE // 2
    freq = ROPE_BASE ** (-jnp.arange(half // 2, dtype=jnp.float32) * 2.0 / half)
    return row[:, None] * freq[None, :], col[:, None] * freq[None, :]


def rotate(x, ang):
    m = x.shape[-1] // 2
    xf = x.astype(jnp.float32)
    x1, x2 = xf[..., :m], xf[..., m:]
    cos, sin = jnp.cos(ang), jnp.sin(ang)
    return jnp.concatenate([x1 * cos - x2 * sin, x1 * sin + x2 * cos], axis=-1).astype(x.dtype)


def axial_rope(x, ang_row, ang_col):
    half = x.shape[-1] // 2
    return jnp.concatenate([rotate(x[..., :half], ang_row), rotate(x[..., half:], ang_col)], axis=-1)


def to_heads(t, d):
    bsz, n, _ = t.shape
    return t.reshape(bsz, n, -1, d).transpose(0, 2, 1, 3)


def sgu(p_u, p_v, norm_w, norm_b, w_s, b_s):
    bsz, n, _ = p_u.shape
    u = jax.nn.gelu(p_u)
    v = layernorm(jax.nn.gelu(p_v), norm_w, norm_b)
    v = v.reshape(bsz, n // SGU_CHUNK, SGU_CHUNK, SGU_HEADS, SGU_HEAD_DIM)
    s = jnp.einsum('hij,bcjhe->bcihe', w_s, v) + b_s.T[:, :, None]
    return u * s.reshape(bsz, n, SGU_WIDTH)


def gla_log_gate(p_g, w, b):
    z = (p_g @ w + b).astype(jnp.float32)
    return jax.nn.log_sigmoid(z) / GLA_TAU


def gla_scan(q, k, v, g, s0):
    bsz, h, n, _ = q.shape
    nc = n // GLA_CHUNK

    def to_chunks(t):
        return jnp.moveaxis(t.reshape(bsz, h, nc, GLA_CHUNK, t.shape[-1]), 2, 0)

    mask = jnp.tril(jnp.ones((GLA_CHUNK, GLA_CHUNK), bool))[:, :, None]

    def step(s, inp):
        qc, kc, vc, gc = inp
        b = jnp.cumsum(gc, axis=2)
        diff = b[:, :, :, None, :] - b[:, :, None, :, :]
        decay = jnp.where(mask, jnp.exp(jnp.where(mask, diff, 0.0)), 0.0)
        att = jnp.einsum('bhid,bhjd,bhijd->bhij', qc, kc, decay)
        o = jnp.einsum('bhij,bhje->bhie', att, vc) + jnp.einsum('bhid,bhde->bhie', qc * jnp.exp(b), s)
        b_last = b[:, :, -1:, :]
        s_new = jnp.exp(b_last)[:, :, 0, :, None] * s + jnp.einsum('bhjd,bhje->bhde', kc * jnp.exp(b_last - b), vc)
        return s_new, o

    s_fin, o = lax.scan(step, s0, (to_chunks(q), to_chunks(k), to_chunks(v), to_chunks(g)))
    o = jnp.moveaxis(o, 0, 2).reshape(bsz, h, n, v.shape[-1])
    return o, s_fin


def gla_final_state(k, v, g):
    b = jnp.cumsum(g, axis=2)
    return jnp.einsum('bhjd,bhje->bhde', k * jnp.exp(b[:, :, -1:, :] - b), v)


def gla_output(o, r, norm_w, dtype):
    bsz, h, n, dv = o.shape
    o = rmsnorm(o.transpose(0, 2, 1, 3), norm_w)
    o = o * jax.nn.silu(r.reshape(bsz, n, h, dv).astype(jnp.float32))
    return o.reshape(bsz, n, h * dv).astype(dtype)


def mla_kv(p_ckv, p_krope, kv_norm_w, w_ukv, ang):
    bsz, n, _ = p_ckv.shape
    kv = (rmsnorm(p_ckv, kv_norm_w) @ w_ukv).reshape(bsz, n, MLA_HEADS, MLA_NOPE + MLA_V)
    k_nope, v = kv[..., :MLA_NOPE], kv[..., MLA_NOPE:]
    k_rope = p_krope if ang is None else axial_rope(p_krope, ang[0], ang[1])
    return k_nope, k_rope, v


def mla_q(p_dq, q_norm_w, w_uq, ang):
    bsz, n, _ = p_dq.shape
    q = (rmsnorm(p_dq, q_norm_w) @ w_uq).reshape(bsz, n, MLA_HEADS, MLA_NOPE + MLA_ROPE)
    q_nope, q_rope = q[..., :MLA_NOPE], q[..., MLA_NOPE:]
    if ang is not None:
        q_rope = axial_rope(q_rope, ang[0][:, None, :], ang[1][:, None, :])
    return q_nope, q_rope


def mla_attend(q_nope, q_rope, k_nope, k_rope, v):
    scale = (MLA_NOPE + MLA_ROPE) ** -0.5
    s = jnp.einsum('bqhd,bkhd->bhqk', q_nope, k_nope) + jnp.einsum('bqhr,bkr->bhqk', q_rope, k_rope)
    p = jax.nn.softmax(s.astype(jnp.float32) * scale, axis=-1).astype(v.dtype)
    return jnp.einsum('bhqk,bkhd->bqhd', p, v)


def mla_attend_blocked(q_nope, q_rope, k_nope, k_rope, v):
    bsz, n = q_nope.shape[:2]
    nb = n // ATTN_BLOCK
    qn = jnp.moveaxis(q_nope.reshape(bsz, nb, ATTN_BLOCK, MLA_HEADS, MLA_NOPE), 1, 0)
    qr = jnp.moveaxis(q_rope.reshape(bsz, nb, ATTN_BLOCK, MLA_HEADS, MLA_ROPE), 1, 0)
    out = lax.map(lambda qs: mla_attend(qs[0], qs[1], k_nope, k_rope, v), (qn, qr))
    return jnp.moveaxis(out, 0, 1).reshape(bsz, n, MLA_WIDTH)


def mixing(h, hc, with_ctx_out, w_in, sgu_norm_w, sgu_norm_b, sgu_w, sgu_b, gla_wg_fwd, gla_bg_fwd,
           gla_wg_bwd, gla_bg_bwd, gla_norm_w, mla_q_norm_w, mla_w_uq, mla_kv_norm_w, mla_w_ukv, ang):
    bsz, n, _ = h.shape
    p = h @ w_in
    gk, gv, ggf, ggb, ckv, kr = split_cols(p[..., :KV_COLS], KV_SPLITS)
    su, sv, gq, gr, dq = split_cols(p[..., KV_COLS:], Q_SPLITS)
    pc = hc @ (w_in if with_ctx_out else w_in[:, :KV_COLS])
    gk_c, gv_c, ggf_c, ggb_c, ckv_c, kr_c = split_cols(pc[..., :KV_COLS], KV_SPLITS)

    y_sgu = sgu(su, sv, sgu_norm_w, sgu_norm_b, sgu_w, sgu_b)

    def kvg(k_, v_, gf_, gb_):
        return (to_heads(k_, GLA_DK).astype(jnp.float32), to_heads(v_, GLA_DV).astype(jnp.float32),
                to_heads(gla_log_gate(gf_, gla_wg_fwd, gla_bg_fwd), GLA_DK),
                to_heads(gla_log_gate(gb_, gla_wg_bwd, gla_bg_bwd), GLA_DK))

    def gq_heads(q_):
        return to_heads(q_, GLA_DK).astype(jnp.float32) * (GLA_DK ** -0.5)

    flip = lambda t: jnp.flip(t, axis=2)
    k_l, v_l, gf_l, gb_l = kvg(gk, gv, ggf, ggb)
    k_c, v_c, gf_c, gb_c = kvg(gk_c, gv_c, ggf_c, ggb_c)
    q_l = gq_heads(gq)
    if with_ctx_out:
        su_c, sv_c, gq_c, gr_c, dq_c = split_cols(pc[..., KV_COLS:], Q_SPLITS)
        q_c = gq_heads(gq_c)
        zero = jnp.zeros((bsz, GLA_HEADS, GLA_DK, GLA_DV), jnp.float32)
        o_cf, s_f = gla_scan(q_c, k_c, v_c, gf_c, zero)
        o_cb, s_b = gla_scan(flip(q_c), flip(k_c), flip(v_c), flip(gb_c), zero)
        y_gla_c = gla_output(o_cf + flip(o_cb), gr_c, gla_norm_w, hc.dtype)
    else:
        s_f = gla_final_state(k_c, v_c, gf_c)
        s_b = gla_final_state(flip(k_c), flip(v_c), flip(gb_c))
    o_lf, _ = gla_scan(q_l, k_l, v_l, gf_l, s_f)
    o_lb, _ = gla_scan(flip(q_l), flip(k_l), flip(v_l), flip(gb_l), s_b)
    y_gla = gla_output(o_lf + flip(o_lb), gr, gla_norm_w, h.dtype)

    kn_c, krp_c, vv_c = mla_kv(ckv_c, kr_c, mla_kv_norm_w, mla_w_ukv, None)
    kn_l, krp_l, vv_l = mla_kv(ckv, kr, mla_kv_norm_w, mla_w_ukv, ang)
    qn_l, qr_l = mla_q(dq, mla_q_norm_w, mla_w_uq, ang)
    y_mla = mla_attend_blocked(qn_l, qr_l, jnp.concatenate([kn_c, kn_l], axis=1),
                               jnp.concatenate([krp_c, krp_l], axis=1), jnp.concatenate([vv_c, vv_l], axis=1))
    y = jnp.concatenate([y_sgu, y_gla, y_mla], axis=-1)

    yc = None
    if with_ctx_out:
        y_sgu_c = sgu(su_c, sv_c, sgu_norm_w, sgu_norm_b, sgu_w, sgu_b)
        qn_c, qr_c = mla_q(dq_c, mla_q_norm_w, mla_w_uq, None)
        y_mla_c = mla_attend(qn_c, qr_c, kn_c, krp_c, vv_c).reshape(bsz, -1, MLA_WIDTH)
        yc = jnp.concatenate([y_sgu_c, y_gla_c, y_mla_c], axis=-1)
    return y, yc


def ffn(h, w1, w2):
    a = jax.nn.relu(h @ w1)
    return (a * a) @ w2


def setup_inputs(seed: int = 0) -> dict:
    key = jax.random.key(seed)
    ks = jax.random.split(key, 32)
    L, D = DEPTH, D_MODEL

    def nrm(k, shape, scale):
        return jax.random.normal(k, shape, jnp.float32) * scale

    return {
        "x": nrm(ks[0], (BATCH, SEQ, D), 1.0),
        "c": nrm(ks[1], (BATCH, D), 1.0),
        "ctx": nrm(ks[2], (BATCH, CTX_LEN, D), 1.0),
        "c_ctx": nrm(ks[3], (D,), 1.0),
        "w_mod": nrm(ks[4], (L, D, 6 * D), 0.5 * D ** -0.5),
        "b_mod": nrm(ks[5], (L, 6 * D), 0.02),
        "norm1_w": 1.0 + nrm(ks[6], (L, D), 0.02),
        "w_in": nrm(ks[7], (L, D, IN_COLS), D ** -0.5),
        "w_out": nrm(ks[8], (L, MIX_WIDTH, D), MIX_WIDTH ** -0.5),
        "sgu_norm_w": 1.0 + nrm(ks[9], (L, SGU_WIDTH), 0.02),
        "sgu_norm_b": nrm(ks[10], (L, SGU_WIDTH), 0.02),
        "sgu_w": nrm(ks[11], (L, SGU_HEADS, SGU_CHUNK, SGU_CHUNK), SGU_CHUNK ** -0.5),
        "sgu_b": 1.0 + nrm(ks[12], (L, SGU_HEADS, SGU_CHUNK), 0.02),
        "gla_wg_fwd": nrm(ks[13], (L, GLA_GATE_RANK, GLA_K_WIDTH), GLA_GATE_RANK ** -0.5),
        "gla_bg_fwd": nrm(ks[14], (L, GLA_K_WIDTH), 0.1),
        "gla_wg_bwd": nrm(ks[15], (L, GLA_GATE_RANK, GLA_K_WIDTH), GLA_GATE_RANK ** -0.5),
        "gla_bg_bwd": nrm(ks[16], (L, GLA_K_WIDTH), 0.1),
        "gla_norm_w": 1.0 + nrm(ks[17], (L, GLA_DV), 0.02),
        "mla_q_norm_w": 1.0 + nrm(ks[18], (L, MLA_Q_RANK), 0.02),
        "mla_w_uq": nrm(ks[19], (L, MLA_Q_RANK, MLA_HEADS * (MLA_NOPE + MLA_ROPE)), MLA_Q_RANK ** -0.5),
        "mla_kv_norm_w": 1.0 + nrm(ks[20], (L, MLA_KV_RANK), 0.02),
        "mla_w_ukv": nrm(ks[21], (L, MLA_KV_RANK, MLA_HEADS * (MLA_NOPE + MLA_V)), MLA_KV_RANK ** -0.5),
        "norm2_w": 1.0 + nrm(ks[22], (L, D), 0.02),
        "w_ff1": nrm(ks[23], (L, D, D_FF), D ** -0.5),
        "w_ff2": nrm(ks[24], (L, D_FF, D), D_FF ** -0.5),
        "final_norm_w": 1.0 + nrm(ks[25], (D,), 0.02),
    }


def reference(x, c, ctx, c_ctx, w_mod, b_mod, norm1_w, w_in, w_out, sgu_norm_w, sgu_norm_b, sgu_w, sgu_b,
              gla_wg_fwd, gla_bg_fwd, gla_wg_bwd, gla_bg_bwd, gla_norm_w, mla_q_norm_w, mla_w_uq,
              mla_kv_norm_w, mla_w_ukv, norm2_w, w_ff1, w_ff2, final_norm_w):
    n = x.shape[1]
    ang = axial_angles(n)
    silu_c = jax.nn.silu(c)
    silu_cc = jax.nn.silu(c_ctx)
    xc = ctx
    for l in range(DEPTH):
        last = l == DEPTH - 1
        mod = (silu_c @ w_mod[l] + b_mod[l])[:, None, :]
        sh1, sc1, g1, sh2, sc2, g2 = jnp.split(mod, 6, axis=-1)
        if last:
            mod_c = silu_cc @ w_mod[l][:, :2 * D_MODEL] + b_mod[l][:2 * D_MODEL]
            sh1c, sc1c = jnp.split(mod_c, 2, axis=-1)
        else:
            mod_c = silu_cc @ w_mod[l] + b_mod[l]
            sh1c, sc1c, g1c, sh2c, sc2c, g2c = jnp.split(mod_c, 6, axis=-1)
        h = modulate(rmsnorm(x, norm1_w[l]), sh1, sc1)
        hc = modulate(rmsnorm(xc, norm1_w[l]), sh1c, sc1c)
        y, yc = mixing(h, hc, not last, w_in[l], sgu_norm_w[l], sgu_norm_b[l], sgu_w[l], sgu_b[l],
                       gla_wg_fwd[l], gla_bg_fwd[l], gla_wg_bwd[l], gla_bg_bwd[l], gla_norm_w[l],
                       mla_q_norm_w[l], mla_w_uq[l], mla_kv_norm_w[l], mla_w_ukv[l], ang)
        x = x + g1 * (y @ w_out[l])
        x = x + g2 * ffn(modulate(rmsnorm(x, norm2_w[l]), sh2, sc2), w_ff1[l], w_ff2[l])
        if not last:
            xc = xc + g1c * (yc @ w_out[l])
            xc = xc + g2c * ffn(modulate(rmsnorm(xc, norm2_w[l]), sh2c, sc2c), w_ff1[l], w_ff2[l])
    return rmsnorm(x, final_norm_w)
```

```python
import functools
import math

import numpy as np
import jax
import jax.numpy as jnp
from jax import lax
from jax.experimental import pallas as pl
from jax.experimental.pallas import tpu as pltpu

F32 = jnp.float32
BF16 = jnp.bfloat16

EPS = 1e-6
GRID_W = 64
ROPE_BASE = 10000.0
LANES = 128
SGU_HEADS = 4
SGU_HEAD_DIM = 64
SGU_WIDTH = 256
SGU_CHUNK = 128
GLA_HEADS = 4
GLA_DK = 32
GLA_DV = 64
GLA_KW = 128
GLA_VW = 256
GLA_RANK = 16
GLA_TAU = 16.0
GLA_BLOCK = 16
MLA_HEADS = 4
MLA_RANK = 256
MLA_NOPE = 128
MLA_ROPE = 64
MLA_V = 128
MLA_QK = 256
VMEM_LIMIT = 56 * 1024 * 1024

C_GQ, C_GK, C_GV, C_GR, C_SU, C_SV, C_CKV, C_DQ, C_ROPE_A, C_ROPE_B, C_END = (
    0, 128, 256, 512, 768, 1024, 1280, 1536, 1792, 1920, 2048)


def _cparams(sem):
    return pltpu.CompilerParams(dimension_semantics=sem, vmem_limit_bytes=VMEM_LIMIT)


def _dot(a, b):
    return jnp.dot(a, b, preferred_element_type=F32)


def _dot_nt(a, b):
    return lax.dot_general(a, b, (((1,), (1,)), ((), ())), preferred_element_type=F32)


def _dot_tn(a, b):
    return lax.dot_general(a, b, (((0,), (0,)), ((), ())), preferred_element_type=F32)


def _rms(x, w):
    return x * lax.rsqrt(jnp.mean(x * x, axis=-1, keepdims=True) + EPS) * w


def _gelu_tanh(x):
    return 0.5 * x * (1.0 + jnp.tanh(math.sqrt(2.0 / math.pi) * (x + 0.044715 * (x * x * x))))


def _silu(x):
    return x * jax.nn.sigmoid(x)


def _full(shape):
    return pl.BlockSpec(shape, lambda *_: (0,) * len(shape))


def _mod_kernel(c_ref, w_ref, b_ref, o_ref):
    s = _silu(c_ref[...]).astype(BF16)
    o_ref[...] = _dot(s, w_ref[...].astype(BF16)) + b_ref[...]


def _mod_call(cvecs, w_mod, b_mod):
    nl, d, d6 = w_mod.shape
    tn = 1536
    return pl.pallas_call(
        _mod_kernel,
        grid=(nl, d6 // tn),
        in_specs=[pl.BlockSpec((8, d), lambda l, j: (0, 0)),
                  pl.BlockSpec((None, d, tn), lambda l, j: (l, 0, j)),
                  pl.BlockSpec((None, 1, tn), lambda l, j: (l, 0, j))],
        out_specs=pl.BlockSpec((None, 8, tn), lambda l, j: (l, 0, j)),
        out_shape=jax.ShapeDtypeStruct((nl, 8, d6), F32),
        compiler_params=_cparams(("parallel", "parallel")),
        name="mod",
    )(cvecs, w_mod, b_mod.reshape(nl, 1, d6))


def _proj_kernel(x_ref, sh_ref, sc_ref, n1w_ref, win_ref, sgw_ref, sgb_ref, ws_ref, bs_ref, wg_ref, bg_ref,
                 qnw_ref, wuq_ref, kvnw_ref, wukv_ref, cos_ref, sin_ref,
                 ysgu_ref, qk_ref, gv_ref, gate_ref, gr_ref, q_ref, k_ref, v_ref, *, q_scale):
    tm = x_ref.shape[0]
    h = _rms(x_ref[...], n1w_ref[...]) * (1.0 + sc_ref[...]) + sh_ref[...]
    hb = h.astype(BF16)

    def proj(lo, hi):
        return _dot(hb, win_ref[:, lo:hi])

    qk_ref[:, 0:GLA_KW] = proj(C_GQ, C_GK) * (GLA_DK ** -0.5)
    qk_ref[:, GLA_KW:] = proj(C_GK, C_GV)
    gv_ref[...] = proj(C_GV, C_GR)
    gr_ref[...] = proj(C_GR, C_SU)

    u = _gelu_tanh(proj(C_SU, C_SV))
    g = _gelu_tanh(proj(C_SV, C_CKV))
    mu = jnp.mean(g, axis=-1, keepdims=True)
    gc = g - mu
    var = jnp.mean(gc * gc, axis=-1, keepdims=True)
    vln = (gc * lax.rsqrt(var + EPS) * sgw_ref[...] + sgb_ref[...]).astype(BF16)
    head_of_lane = lax.broadcasted_iota(jnp.int32, (SGU_CHUNK, SGU_WIDTH), 1) // SGU_HEAD_DIM
    ws = ws_ref[...]
    for c in range(tm // SGU_CHUNK):
        rows = slice(c * SGU_CHUNK, (c + 1) * SGU_CHUNK)
        r = _dot(ws, vln[rows, :])
        s = r[0:SGU_CHUNK]
        for hh in range(1, SGU_HEADS):
            s = jnp.where(head_of_lane == hh, r[hh * SGU_CHUNK:(hh + 1) * SGU_CHUNK], s)
        ysgu_ref[rows, :] = (u[rows, :] * (s + bs_ref[...])).astype(BF16)

    ga = proj(C_ROPE_A, C_ROPE_B)
    gb = proj(C_ROPE_B, C_END)
    z = _dot(ga.astype(BF16), wg_ref[...]) + bg_ref[...]
    log_sig = jnp.minimum(z, 0.0) - jnp.log1p(jnp.exp(-jnp.abs(z)))
    gate_ref[...] = log_sig * (1.0 / GLA_TAU)
    cos = cos_ref[...]
    sin = sin_ref[...]
    k_rope = (ga * cos + gb * sin).astype(BF16)

    ckv = _rms(proj(C_CKV, C_DQ), kvnw_ref[...]).astype(BF16)
    kv = _dot(ckv, wukv_ref[...])
    for hh in range(MLA_HEADS):
        k_ref[hh, :, 0:MLA_NOPE] = kv[:, hh * MLA_NOPE:(hh + 1) * MLA_NOPE].astype(BF16)
        k_ref[hh, :, MLA_NOPE:] = k_rope
        off = MLA_HEADS * MLA_NOPE + hh * MLA_V
        v_ref[hh] = kv[:, off:off + MLA_V].astype(BF16)

    cq = _rms(proj(C_DQ, C_ROPE_A), qnw_ref[...]).astype(BF16)
    qq = _dot(cq, wuq_ref[...])
    nq = MLA_HEADS * LANES
    for hh in range(MLA_HEADS):
        sl = slice(hh * LANES, (hh + 1) * LANES)
        q_ref[hh, :, 0:MLA_NOPE] = (qq[:, sl] * q_scale).astype(BF16)
        rot = qq[:, nq + hh * LANES:nq + (hh + 1) * LANES] * cos + qq[:, 2 * nq + hh * LANES:2 * nq + (hh + 1) * LANES] * sin
        q_ref[hh, :, MLA_NOPE:] = (rot * q_scale).astype(BF16)


def _proj_call(x, sh, sc, lw, cos, sin, tm):
    bsz, n, d = x.shape
    nh = MLA_HEADS
    q_scale = (MLA_NOPE + MLA_ROPE) ** -0.5 * math.log2(math.e)
    tok = lambda w: pl.BlockSpec((None, tm, w), lambda b, i: (b, i, 0))
    vec = pl.BlockSpec((None, 1, d), lambda b, i: (b, 0, 0))
    heads = lambda w: pl.BlockSpec((None, nh, tm, w), lambda b, i: (b, 0, i, 0))
    tab = pl.BlockSpec((tm, LANES), lambda b, i: (i, 0))
    in_specs = [tok(d), vec, vec, _full((1, d)), _full((d, C_END)),
                _full((1, SGU_WIDTH)), _full((1, SGU_WIDTH)), _full((SGU_HEADS * SGU_CHUNK, SGU_CHUNK)),
                _full((SGU_CHUNK, SGU_WIDTH)), _full((LANES, 2 * GLA_KW)), _full((1, 2 * GLA_KW)),
                _full((1, MLA_RANK)), _full((MLA_RANK, 3 * nh * LANES)),
                _full((1, MLA_RANK)), _full((MLA_RANK, nh * (MLA_NOPE + MLA_V))), tab, tab]
    out_specs = [tok(SGU_WIDTH), tok(2 * GLA_KW), tok(GLA_VW), tok(2 * GLA_KW), tok(GLA_VW),
                 heads(MLA_QK), heads(MLA_QK), heads(MLA_V)]
    out_shape = [jax.ShapeDtypeStruct((bsz, n, SGU_WIDTH), BF16),
                 jax.ShapeDtypeStruct((bsz, n, 2 * GLA_KW), F32),
                 jax.ShapeDtypeStruct((bsz, n, GLA_VW), F32),
                 jax.ShapeDtypeStruct((bsz, n, 2 * GLA_KW), F32),
                 jax.ShapeDtypeStruct((bsz, n, GLA_VW), F32),
                 jax.ShapeDtypeStruct((bsz, nh, n, MLA_QK), BF16),
                 jax.ShapeDtypeStruct((bsz, nh, n, MLA_QK), BF16),
                 jax.ShapeDtypeStruct((bsz, nh, n, MLA_V), BF16)]
    return pl.pallas_call(
        functools.partial(_proj_kernel, q_scale=q_scale),
        grid=(bsz, n // tm), in_specs=in_specs, out_specs=out_specs, out_shape=out_shape,
        compiler_params=_cparams(("parallel", "parallel")), name="proj",
    )(x, sh, sc, lw["n1w"], lw["w_in"], lw["sgw"], lw["sgb"], lw["ws"], lw["bs"], lw["wg"], lw["bg"],
      lw["qnw"], lw["wuq"], lw["kvnw"], lw["wukv"], cos, sin)


def _split3(x):
    hi = x.astype(BF16)
    r1 = x - hi.astype(F32)
    mid = r1.astype(BF16)
    lo = (r1 - mid.astype(F32)).astype(BF16)
    return hi, mid, lo


def _gla_block(q_ref, k_ref, v0_ref, v1_ref, b_ref, st_ref, o_ref, r0, fwd, ind, mask_t):
    nb = GLA_BLOCK
    rows = pl.ds(r0, nb)
    bcast = lambda ref, r: ref[pl.ds(r, nb, stride=0), :]
    q = q_ref[rows, :]
    k = k_ref[rows, :]
    v = jnp.concatenate([v0_ref[rows, :], v1_ref[rows, :]], axis=1)
    b = b_ref[rows, :]
    b_end = bcast(b_ref, r0 + (nb - 1) if fwd else r0)
    st = st_ref[...]
    o = _dot_nt((q * jnp.exp(b)).astype(BF16), st.astype(BF16))
    kh = (k * jnp.exp(b_end - b)).astype(BF16)
    st_ref[...] = st * jnp.exp(b_end[0:1, :]) + mask_t * _dot_tn(v.astype(BF16), kh)

    row = lax.broadcasted_iota(jnp.int32, (nb, GLA_KW), 0)
    parts = []
    for j in range(nb):
        valid = (row >= j) if fwd else (row <= j)
        e = jnp.exp(jnp.where(valid, b - bcast(b_ref, r0 + j), -jnp.inf))
        parts.append((e * q * bcast(k_ref, r0 + j)).astype(BF16))
    r = _dot(jnp.concatenate(parts, axis=0), ind)
    for j in range(nb):
        vj = jnp.concatenate([bcast(v0_ref, r0 + j), bcast(v1_ref, r0 + j)], axis=1)
        o = o + r[j * nb:(j + 1) * nb] * vj
    o_ref[rows, :] = o


def _gla_kernel(qf_ref, kf_ref, v0f_ref, v1f_ref, gf_ref, qb_ref, kb_ref, v0b_ref, v1b_ref, gb_ref,
                s0_ref, trif_ref, trib_ref, ind_ref, maskt_ref, of_ref, ob_ref, sfin_ref, stf, stb, bf, bb):
    t = pl.program_id(1)
    tt = qf_ref.shape[0]
    nblk = tt // GLA_BLOCK

    @pl.when(t == 0)
    def _():
        stf[...] = s0_ref[0]
        stb[...] = s0_ref[1]

    for g_ref, tri_ref, b_sc in ((gf_ref, trif_ref, bf), (gb_ref, trib_ref, bb)):
        hi, mid, lo = _split3(g_ref[...])
        tri = tri_ref[...]
        b_sc[...] = _dot(tri, hi) + _dot(tri, mid) + _dot(tri, lo)

    ind = ind_ref[...]
    mask_t = maskt_ref[...]

    def step(i, carry):
        rf = pl.multiple_of(i * GLA_BLOCK, GLA_BLOCK)
        rb = pl.multiple_of((nblk - 1 - i) * GLA_BLOCK, GLA_BLOCK)
        _gla_block(qf_ref, kf_ref, v0f_ref, v1f_ref, bf, stf, of_ref, rf, True, ind, mask_t)
        _gla_block(qb_ref, kb_ref, v0b_ref, v1b_ref, bb, stb, ob_ref, rb, False, ind, mask_t)
        return carry

    lax.fori_loop(0, nblk, step, 0)

    @pl.when(t == pl.num_programs(1) - 1)
    def _():
        sfin_ref[0] = stf[...]
        sfin_ref[1] = stb[...]


def _gla_consts(tt):
    i = np.arange(tt)
    same = (i[:, None] // GLA_BLOCK) == (i[None, :] // GLA_BLOCK)
    tri_f = (same & (i[None, :] <= i[:, None])).astype(np.float32)
    tri_b = (same & (i[None, :] >= i[:, None])).astype(np.float32)
    kd = np.arange(GLA_KW) // GLA_DK
    ve = np.arange(GLA_VW) // GLA_DV
    ind = (kd[:, None] == ve[None, :]).astype(np.float32)
    return (jnp.asarray(tri_f, BF16), jnp.asarray(tri_b, BF16), jnp.asarray(ind, BF16), jnp.asarray(ind.T, F32))


def _gla_call(qk, gv, gates, s0, tt):
    bsz, n, _ = qk.shape
    nt = n // tt
    tri_f, tri_b, ind, mask_t = _gla_consts(tt)
    half = lambda rev, c: pl.BlockSpec((None, tt, LANES), lambda b, t: (b, nt - 1 - t if rev else t, c))
    full = lambda rev: pl.BlockSpec((None, tt, GLA_VW), lambda b, t: (b, nt - 1 - t if rev else t, 0))
    st = pl.BlockSpec((None, 2, GLA_VW, GLA_KW), lambda b, t: (b, 0, 0, 0))
    fwd_in = [half(False, 0), half(False, 1), half(False, 0), half(False, 1), half(False, 0)]
    bwd_in = [half(True, 0), half(True, 1), half(True, 0), half(True, 1), half(True, 1)]
    return pl.pallas_call(
        _gla_kernel,
        grid=(bsz, nt),
        in_specs=fwd_in + bwd_in + [st, _full((tt, tt)), _full((tt, tt)), _full((GLA_KW, GLA_VW)),
                                    _full((GLA_VW, GLA_KW))],
        out_specs=[full(False), full(True), st],
        out_shape=[jax.ShapeDtypeStruct((bsz, n, GLA_VW), F32), jax.ShapeDtypeStruct((bsz, n, GLA_VW), F32),
                   jax.ShapeDtypeStruct((bsz, 2, GLA_VW, GLA_KW), F32)],
        scratch_shapes=[pltpu.VMEM((GLA_VW, GLA_KW), F32), pltpu.VMEM((GLA_VW, GLA_KW), F32),
                        pltpu.VMEM((tt, GLA_KW), F32), pltpu.VMEM((tt, GLA_KW), F32)],
        compiler_params=_cparams(("parallel", "arbitrary")), name="gla",
    )(qk, qk, gv, gv, gates, qk, qk, gv, gv, gates, s0, tri_f, tri_b, ind, mask_t)


def _attn_kernel(q_ref, k_ref, v_ref, o_ref, m_sc, l_sc, acc_sc, *, tk):
    nk = k_ref.shape[0] // tk
    q = q_ref[...]
    m_sc[...] = jnp.full(m_sc.shape, -jnp.inf, F32)
    l_sc[...] = jnp.zeros(l_sc.shape, F32)
    acc_sc[...] = jnp.zeros(acc_sc.shape, F32)

    def body(j, carry):
        r0 = pl.multiple_of(j * tk, tk)
        s = _dot_nt(q, k_ref[pl.ds(r0, tk), :])
        m_prev = m_sc[...]
        m_new = jnp.maximum(m_prev, jnp.max(s, axis=1, keepdims=True))
        alpha = jnp.exp2(m_prev - m_new)
        p = jnp.exp2(s - m_new)
        l_sc[...] = alpha * l_sc[...] + jnp.sum(p, axis=1, keepdims=True)
        acc_sc[...] = alpha * acc_sc[...] + _dot(p.astype(BF16), v_ref[pl.ds(r0, tk), :])
        m_sc[...] = m_new
        return carry

    lax.fori_loop(0, nk, body, 0)
    o_ref[...] = (acc_sc[...] * (1.0 / l_sc[...])).astype(o_ref.dtype)


def _attn_call(q, k, v, tq, tk):
    bsz, nh, nq, dq = q.shape
    nk = k.shape[2]
    return pl.pallas_call(
        functools.partial(_attn_kernel, tk=tk),
        grid=(bsz, nh, nq // tq),
        in_specs=[pl.BlockSpec((None, None, tq, dq), lambda b, h, i: (b, h, i, 0)),
                  pl.BlockSpec((None, None, nk, dq), lambda b, h, i: (b, h, 0, 0)),
                  pl.BlockSpec((None, None, nk, MLA_V), lambda b, h, i: (b, h, 0, 0))],
        out_specs=pl.BlockSpec((None, None, tq, MLA_V), lambda b, h, i: (b, h, i, 0)),
        out_shape=jax.ShapeDtypeStruct((bsz, nh, nq, MLA_V), BF16),
        scratch_shapes=[pltpu.VMEM((tq, 1), F32), pltpu.VMEM((tq, 1), F32), pltpu.VMEM((tq, MLA_V), F32)],
        compiler_params=_cparams(("parallel", "parallel", "parallel")), name="attn",
    )(q, k, v)


def _out_kernel(x_ref, ysgu_ref, of_ref, ob_ref, gr_ref, ymla_ref, g1_ref, sh2_ref, sc2_ref, g2_ref, glaw_ref,
                hsum_ref, wout_ref, n2w_ref, w1_ref, w2_ref, fnw_ref, o_ref, y_sc, *, final_norm):
    o = of_ref[...] + ob_ref[...]
    o2 = o * o
    hi = o2.astype(BF16)
    lo = (o2 - hi.astype(F32)).astype(BF16)
    ms = (_dot(hi, hsum_ref[...]) + _dot(lo, hsum_ref[...])) * (1.0 / GLA_DV)
    y_gla = o * lax.rsqrt(ms + EPS) * glaw_ref[...] * _silu(gr_ref[...])
    y_sc[:, 0:SGU_WIDTH] = ysgu_ref[...]
    y_sc[:, SGU_WIDTH:SGU_WIDTH + GLA_VW] = y_gla.astype(BF16)
    for hh in range(MLA_HEADS):
        off = SGU_WIDTH + GLA_VW + hh * MLA_V
        y_sc[:, off:off + MLA_V] = ymla_ref[hh]
    x1 = x_ref[...] + g1_ref[...] * _dot(y_sc[...], wout_ref[...])
    hn = (_rms(x1, n2w_ref[...]) * (1.0 + sc2_ref[...]) + sh2_ref[...]).astype(BF16)
    dff = w1_ref.shape[1]
    fc = 1024
    acc = jnp.zeros(x1.shape, F32)
    for c in range(dff // fc):
        a = jnp.maximum(_dot(hn, w1_ref[:, c * fc:(c + 1) * fc]), 0.0)
        acc = acc + _dot((a * a).astype(BF16), w2_ref[c * fc:(c + 1) * fc, :])
    x2 = x1 + g2_ref[...] * acc
    if final_norm:
        x2 = _rms(x2, fnw_ref[...])
    o_ref[...] = x2


def _out_call(x, ysgu, o_f, o_b, gr, ymla, g1, sh2, sc2, g2, lw, fnw, tm, final_norm):
    bsz, n, d = x.shape
    nh = MLA_HEADS
    mix = SGU_WIDTH + GLA_VW + nh * MLA_V
    dff = lw["w1"].shape[1]
    tok = lambda w: pl.BlockSpec((None, tm, w), lambda b, i: (b, i, 0))
    vec = pl.BlockSpec((None, 1, d), lambda b, i: (b, 0, 0))
    once = lambda shape: pl.BlockSpec(shape, lambda *_: (0,) * len(shape), pipeline_mode=pl.Buffered(1))
    in_specs = [tok(d), tok(SGU_WIDTH), tok(GLA_VW), tok(GLA_VW), tok(GLA_VW),
                pl.BlockSpec((None, nh, tm, MLA_V), lambda b, i: (b, 0, i, 0)),
                vec, vec, vec, vec, _full((1, GLA_VW)), _full((GLA_VW, GLA_VW)),
                once((mix, d)), _full((1, d)), once((d, dff)), once((dff, d)), _full((1, d))]
    return pl.pallas_call(
        functools.partial(_out_kernel, final_norm=final_norm),
        grid=(bsz, n // tm), in_specs=in_specs, out_specs=tok(d),
        out_shape=jax.ShapeDtypeStruct((bsz, n, d), F32),
        scratch_shapes=[pltpu.VMEM((tm, mix), BF16)],
        compiler_params=_cparams(("parallel", "parallel")), name="out_ffn",
    )(x, ysgu, o_f, o_b, gr, ymla, g1, sh2, sc2, g2, lw["glaw"], lw["hsum"], lw["w_out"], lw["n2w"],
      lw["w1"], lw["w2"], fnw)


def _rope_tables(n):
    half = MLA_ROPE // 2
    pos = np.arange(n)
    freq = ROPE_BASE ** (-np.arange(half // 2, dtype=np.float64) * 2.0 / half)
    ang_r = (pos // GRID_W)[:, None] * freq[None, :]
    ang_c = (pos % GRID_W)[:, None] * freq[None, :]
    cos = np.zeros((n, LANES), np.float32)
    sin = np.zeros((n, LANES), np.float32)
    cos[:, 0:64] = np.concatenate([np.cos(ang_r), np.cos(ang_r), np.cos(ang_c), np.cos(ang_c)], axis=1)
    sin[:, 0:64] = np.concatenate([-np.sin(ang_r), np.sin(ang_r), -np.sin(ang_c), np.sin(ang_c)], axis=1)
    return jnp.asarray(cos), jnp.asarray(sin)


def _identity_tables(n):
    cos = np.zeros((n, LANES), np.float32)
    cos[:, 0:64] = 1.0
    return jnp.asarray(cos), jnp.zeros((n, LANES), F32)


_ROT_PARTNER = np.concatenate([np.arange(16, 32), np.arange(0, 16), np.arange(48, 64), np.arange(32, 48)])


def _layer_weights(l, p):
    w_in = p["w_in"][l]
    d = w_in.shape[0]
    o_gk, o_gv, o_gf, o_gb, o_ckv, o_kr, o_su, o_sv, o_gq, o_gr, o_dq = (
        0, 128, 384, 400, 416, 672, 736, 992, 1248, 1376, 1632)
    col = lambda o, w: w_in[:, o:o + w]
    kr = col(o_kr, MLA_ROPE)
    z = lambda w: jnp.zeros((d, w), w_in.dtype)
    w_in_p = jnp.concatenate([
        col(o_gq, 128), col(o_gk, 128), col(o_gv, 256), col(o_gr, 256), col(o_su, 256), col(o_sv, 256),
        col(o_ckv, 256), col(o_dq, 256),
        kr, col(o_gf, GLA_RANK), col(o_gb, GLA_RANK), z(32),
        kr[:, _ROT_PARTNER], z(64)], axis=1).astype(BF16)

    wg = jnp.zeros((LANES, 2 * GLA_KW), F32)
    wg = wg.at[64:80, 0:GLA_KW].set(p["gla_wg_fwd"][l]).at[80:96, GLA_KW:].set(p["gla_wg_bwd"][l])
    bg = jnp.concatenate([p["gla_bg_fwd"][l], p["gla_bg_bwd"][l]])[None, :]

    w_uq = p["mla_w_uq"][l].reshape(MLA_RANK, MLA_HEADS, MLA_NOPE + MLA_ROPE)
    nope = w_uq[:, :, :MLA_NOPE].reshape(MLA_RANK, -1)
    rope = w_uq[:, :, MLA_NOPE:]
    pad = lambda t: jnp.pad(t, ((0, 0), (0, 0), (0, LANES - MLA_ROPE))).reshape(MLA_RANK, -1)
    wuq = jnp.concatenate([nope, pad(rope), pad(rope[:, :, _ROT_PARTNER])], axis=1).astype(BF16)

    w_ukv = p["mla_w_ukv"][l].reshape(MLA_RANK, MLA_HEADS, MLA_NOPE + MLA_V)
    wukv = jnp.concatenate([w_ukv[:, :, :MLA_NOPE].reshape(MLA_RANK, -1),
                            w_ukv[:, :, MLA_NOPE:].reshape(MLA_RANK, -1)], axis=1).astype(BF16)

    hv = np.arange(GLA_VW) // GLA_DV
    return dict(
        n1w=p["norm1_w"][l][None, :], w_in=w_in_p,
        sgw=p["sgu_norm_w"][l][None, :], sgb=p["sgu_norm_b"][l][None, :],
        ws=p["sgu_w"][l].reshape(SGU_HEADS * SGU_CHUNK, SGU_CHUNK).astype(BF16),
        bs=jnp.repeat(p["sgu_b"][l].T, SGU_HEAD_DIM, axis=1),
        wg=wg.astype(BF16), bg=bg,
        qnw=p["mla_q_norm_w"][l][None, :], wuq=wuq, kvnw=p["mla_kv_norm_w"][l][None, :], wukv=wukv,
        glaw=jnp.tile(p["gla_norm_w"][l], GLA_HEADS)[None, :],
        hsum=jnp.asarray((hv[:, None] == hv[None, :]).astype(np.float32), BF16),
        w_out=p["w_out"][l].astype(BF16), n2w=p["norm2_w"][l][None, :],
        w1=p["w_ff1"][l].astype(BF16), w2=p["w_ff2"][l].astype(BF16))


def _pick(n, pref):
    for t in pref:
        if n % t == 0:
            return t
    return n


def kernel(x, c, ctx, c_ctx, w_mod, b_mod, norm1_w, w_in, w_out, sgu_norm_w, sgu_norm_b, sgu_w, sgu_b, gla_wg_fwd, gla_bg_fwd, gla_wg_bwd, gla_bg_bwd, gla_norm_w, mla_q_norm_w, mla_w_uq, mla_kv_norm_w, mla_w_ukv, norm2_w, w_ff1, w_ff2, final_norm_w):
    p = dict(norm1_w=norm1_w, w_in=w_in, w_out=w_out, sgu_norm_w=sgu_norm_w, sgu_norm_b=sgu_norm_b, sgu_w=sgu_w,
             sgu_b=sgu_b, gla_wg_fwd=gla_wg_fwd, gla_bg_fwd=gla_bg_fwd, gla_wg_bwd=gla_wg_bwd, gla_bg_bwd=gla_bg_bwd,
             gla_norm_w=gla_norm_w, mla_q_norm_w=mla_q_norm_w, mla_w_uq=mla_w_uq, mla_kv_norm_w=mla_kv_norm_w,
             mla_w_ukv=mla_w_ukv, norm2_w=norm2_w, w_ff1=w_ff1, w_ff2=w_ff2)
    bsz, n, d = x.shape
    nc = ctx.shape[1]
    depth = w_mod.shape[0]
    fnw = final_norm_w[None, :]

    cvecs = jnp.zeros((8, d), F32).at[0:bsz].set(c).at[bsz].set(c_ctx)
    mod = _mod_call(cvecs, w_mod, b_mod)

    tm = _pick(n, (512, 256, 128))
    tmc = _pick(nc, (256, 128))
    tq = _pick(n, (512, 256, 128))
    tk = _pick(n + nc, (768, 512, 256, 128))
    cos, sin = _rope_tables(n)
    cos_c, sin_c = _identity_tables(nc)

    xc = ctx
    for l in range(depth):
        last = l == depth - 1
        lw = _layer_weights(l, p)
        m = mod[l, 0:bsz].reshape(bsz, 1, 6, d)
        sh1, sc1, g1, sh2, sc2, g2 = [m[:, :, i] for i in range(6)]
        mc = jnp.broadcast_to(mod[l, bsz].reshape(1, 1, 6, d), (bsz, 1, 6, d))
        sh1c, sc1c, g1c, sh2c, sc2c, g2c = [mc[:, :, i] for i in range(6)]

        ysgu_c, qk_c, gv_c, gate_c, gr_c, q_c, k_c, v_c = _proj_call(xc, sh1c, sc1c, lw, cos_c, sin_c, tmc)
        zero = jnp.zeros((bsz, 2, GLA_VW, GLA_KW), F32)
        of_c, ob_c, s_ctx = _gla_call(qk_c, gv_c, gate_c, zero, tmc)

        ysgu, qk, gv, gate, gr, q, k, v = _proj_call(x, sh1, sc1, lw, cos, sin, tm)
        o_f, o_b, _ = _gla_call(qk, gv, gate, s_ctx, tm)
        ymla = _attn_call(q, jnp.concatenate([k, k_c], axis=2), jnp.concatenate([v, v_c], axis=2), tq, tk)
        x = _out_call(x, ysgu, o_f, o_b, gr, ymla, g1, sh2, sc2, g2, lw, fnw, tm, last)

        if not last:
            ymla_c = _attn_call(q_c, k_c, v_c, tmc, tmc)
            xc = _out_call(xc, ysgu_c, of_c, ob_c, gr_c, ymla_c, g1c, sh2c, sc2c, g2c, lw, fnw, tmc, False)
    return x
```

```python
import functools
import math

import numpy as np
import jax
import jax.numpy as jnp
from jax import lax
from jax.experimental import pallas as pl
from jax.experimental.pallas import tpu as pltpu

F32 = jnp.float32
BF16 = jnp.bfloat16

EPS = 1e-6
GRID_W = 64
ROPE_BASE = 10000.0
LANES = 128
SGU_HEADS = 4
SGU_HEAD_DIM = 64
SGU_WIDTH = 256
SGU_CHUNK = 128
GLA_HEADS = 4
GLA_DK = 32
GLA_DV = 64
GLA_KW = 128
GLA_VW = 256
GLA_RANK = 16
GLA_TAU = 16.0
GLA_BLOCK = 16
MLA_HEADS = 4
MLA_RANK = 256
MLA_NOPE = 128
MLA_ROPE = 64
MLA_V = 128
MLA_QK = 256
MLA_VROWS = 144
VMEM_LIMIT = 56 * 1024 * 1024

C_GQ, C_GK, C_GV, C_GR, C_SU, C_SV, C_CKV, C_DQ, C_ROPE_A, C_ROPE_B, C_END = (
    0, 128, 256, 512, 768, 1024, 1280, 1536, 1792, 1920, 2048)


def _cparams(sem):
    return pltpu.CompilerParams(dimension_semantics=sem, vmem_limit_bytes=VMEM_LIMIT)


def _dot(a, b):
    return jnp.dot(a, b, preferred_element_type=F32)


def _dot_nt(a, b):
    return lax.dot_general(a, b, (((1,), (1,)), ((), ())), preferred_element_type=F32)


def _dot_tn(a, b):
    return lax.dot_general(a, b, (((0,), (0,)), ((), ())), preferred_element_type=F32)


def _rms(x, w):
    return x * lax.rsqrt(jnp.mean(x * x, axis=-1, keepdims=True) + EPS) * w


def _gelu_tanh(x):
    return 0.5 * x * (1.0 + jnp.tanh(math.sqrt(2.0 / math.pi) * (x + 0.044715 * (x * x * x))))


def _silu(x):
    return x * jax.nn.sigmoid(x)


def _full(shape):
    return pl.BlockSpec(shape, lambda *_: (0,) * len(shape))


def _mod_kernel(c_ref, w_ref, b_ref, o_ref):
    s = _silu(c_ref[...]).astype(BF16)
    o_ref[...] = _dot(s, w_ref[...].astype(BF16)) + b_ref[...]


def _mod_call(cvecs, w_mod, b_mod):
    nl, d, d6 = w_mod.shape
    tn = 1536
    return pl.pallas_call(
        _mod_kernel,
        grid=(nl, d6 // tn),
        in_specs=[pl.BlockSpec((8, d), lambda l, j: (0, 0)),
                  pl.BlockSpec((None, d, tn), lambda l, j: (l, 0, j)),
                  pl.BlockSpec((None, 1, tn), lambda l, j: (l, 0, j))],
        out_specs=pl.BlockSpec((None, 8, tn), lambda l, j: (l, 0, j)),
        out_shape=jax.ShapeDtypeStruct((nl, 8, d6), F32),
        compiler_params=_cparams(("parallel", "parallel")),
        name="mod",
    )(cvecs, w_mod, b_mod.reshape(nl, 1, d6))


def _proj_kernel(x_ref, sh_ref, sc_ref, n1w_ref, win_ref, sgw_ref, sgb_ref, ws_ref, bs_ref, wg_ref, bg_ref,
                 qnw_ref, wuq_ref, kvnw_ref, wukv_ref, cos_ref, sin_ref,
                 ysgu_ref, qk_ref, gv_ref, gate_ref, gr_ref, q_ref, k_ref, v_ref, *, q_scale):
    tm = x_ref.shape[0]
    h = _rms(x_ref[...], n1w_ref[...]) * (1.0 + sc_ref[...]) + sh_ref[...]
    hb = h.astype(BF16)

    def proj(lo, hi):
        return _dot(hb, win_ref[:, lo:hi])

    qk_ref[:, 0:GLA_KW] = proj(C_GQ, C_GK) * (GLA_DK ** -0.5)
    qk_ref[:, GLA_KW:] = proj(C_GK, C_GV)
    gv_ref[...] = proj(C_GV, C_GR)
    gr_ref[...] = proj(C_GR, C_SU)

    u = _gelu_tanh(proj(C_SU, C_SV))
    g = _gelu_tanh(proj(C_SV, C_CKV))
    mu = jnp.mean(g, axis=-1, keepdims=True)
    gc = g - mu
    var = jnp.mean(gc * gc, axis=-1, keepdims=True)
    vln = (gc * lax.rsqrt(var + EPS) * sgw_ref[...] + sgb_ref[...]).astype(BF16)
    head_of_lane = lax.broadcasted_iota(jnp.int32, (SGU_CHUNK, SGU_WIDTH), 1) // SGU_HEAD_DIM
    ws = ws_ref[...]
    for c in range(tm // SGU_CHUNK):
        rows = slice(c * SGU_CHUNK, (c + 1) * SGU_CHUNK)
        r = _dot(ws, vln[rows, :])
        s = r[0:SGU_CHUNK]
        for hh in range(1, SGU_HEADS):
            s = jnp.where(head_of_lane == hh, r[hh * SGU_CHUNK:(hh + 1) * SGU_CHUNK], s)
        ysgu_ref[rows, :] = (u[rows, :] * (s + bs_ref[...])).astype(BF16)

    ga = proj(C_ROPE_A, C_ROPE_B)
    gb = proj(C_ROPE_B, C_END)
    z = _dot(ga.astype(BF16), wg_ref[...]) + bg_ref[...]
    log_sig = jnp.minimum(z, 0.0) - jnp.log1p(jnp.exp(-jnp.abs(z)))
    gate_ref[...] = log_sig * (1.0 / GLA_TAU)
    cos = cos_ref[...]
    sin = sin_ref[...]
    k_rope = (ga * cos + gb * sin).astype(BF16)

    ckv = _rms(proj(C_CKV, C_DQ), kvnw_ref[...]).astype(BF16)
    kv = _dot(ckv, wukv_ref[...])
    for hh in range(MLA_HEADS):
        k_ref[hh, :, 0:MLA_NOPE] = kv[:, hh * MLA_NOPE:(hh + 1) * MLA_NOPE].astype(BF16)
        k_ref[hh, :, MLA_NOPE:] = k_rope
        off = MLA_HEADS * MLA_NOPE + hh * MLA_V
        v_ref[hh] = kv[:, off:off + MLA_V].astype(BF16)

    cq = _rms(proj(C_DQ, C_ROPE_A), qnw_ref[...]).astype(BF16)
    qq = _dot(cq, wuq_ref[...])
    nq = MLA_HEADS * LANES
    for hh in range(MLA_HEADS):
        sl = slice(hh * LANES, (hh + 1) * LANES)
        q_ref[hh, :, 0:MLA_NOPE] = (qq[:, sl] * q_scale).astype(BF16)
        rot = qq[:, nq + hh * LANES:nq + (hh + 1) * LANES] * cos + qq[:, 2 * nq + hh * LANES:2 * nq + (hh + 1) * LANES] * sin
        q_ref[hh, :, MLA_NOPE:] = (rot * q_scale).astype(BF16)


def _proj_call(x, sh, sc, lw, cos, sin, tm):
    bsz, n, d = x.shape
    nh = MLA_HEADS
    q_scale = (MLA_NOPE + MLA_ROPE) ** -0.5 * math.log2(math.e)
    tok = lambda w: pl.BlockSpec((None, tm, w), lambda b, i: (b, i, 0))
    vec = pl.BlockSpec((None, 1, d), lambda b, i: (b, 0, 0))
    heads = lambda w: pl.BlockSpec((None, nh, tm, w), lambda b, i: (b, 0, i, 0))
    tab = pl.BlockSpec((tm, LANES), lambda b, i: (i, 0))
    in_specs = [tok(d), vec, vec, _full((1, d)), _full((d, C_END)),
                _full((1, SGU_WIDTH)), _full((1, SGU_WIDTH)), _full((SGU_HEADS * SGU_CHUNK, SGU_CHUNK)),
                _full((SGU_CHUNK, SGU_WIDTH)), _full((LANES, 2 * GLA_KW)), _full((1, 2 * GLA_KW)),
                _full((1, MLA_RANK)), _full((MLA_RANK, 3 * nh * LANES)),
                _full((1, MLA_RANK)), _full((MLA_RANK, nh * (MLA_NOPE + MLA_V))), tab, tab]
    out_specs = [tok(SGU_WIDTH), tok(2 * GLA_KW), tok(GLA_VW), tok(2 * GLA_KW), tok(GLA_VW),
                 heads(MLA_QK), heads(MLA_QK), heads(MLA_V)]
    out_shape = [jax.ShapeDtypeStruct((bsz, n, SGU_WIDTH), BF16),
                 jax.ShapeDtypeStruct((bsz, n, 2 * GLA_KW), F32),
                 jax.ShapeDtypeStruct((bsz, n, GLA_VW), F32),
                 jax.ShapeDtypeStruct((bsz, n, 2 * GLA_KW), F32),
                 jax.ShapeDtypeStruct((bsz, n, GLA_VW), F32),
                 jax.ShapeDtypeStruct((bsz, nh, n, MLA_QK), BF16),
                 jax.ShapeDtypeStruct((bsz, nh, n, MLA_QK), BF16),
                 jax.ShapeDtypeStruct((bsz, nh, n, MLA_V), BF16)]
    return pl.pallas_call(
        functools.partial(_proj_kernel, q_scale=q_scale),
        grid=(bsz, n // tm), in_specs=in_specs, out_specs=out_specs, out_shape=out_shape,
        compiler_params=_cparams(("parallel", "parallel")), name="proj",
    )(x, sh, sc, lw["n1w"], lw["w_in"], lw["sgw"], lw["sgb"], lw["ws"], lw["bs"], lw["wg"], lw["bg"],
      lw["qnw"], lw["wuq"], lw["kvnw"], lw["wukv"], cos, sin)


def _split3(x):
    hi = x.astype(BF16)
    r1 = x - hi.astype(F32)
    mid = r1.astype(BF16)
    lo = (r1 - mid.astype(F32)).astype(BF16)
    return hi, mid, lo


def _row_bcast(ref, r, n):
    return ref[pl.ds(r, n, stride=0), :]


def _gla_block(q_ref, k_ref, v0_ref, v1_ref, b_ref, st_ref, o_ref, r0, fwd, ind, mask_t):
    nb = GLA_BLOCK
    rows = pl.ds(r0, nb)
    bcast = lambda ref, r: _row_bcast(ref, r, nb)
    q = q_ref[rows, :]
    k = k_ref[rows, :]
    v = jnp.concatenate([v0_ref[rows, :], v1_ref[rows, :]], axis=1)
    b = b_ref[rows, :]
    b_end = bcast(b_ref, r0 + (nb - 1) if fwd else r0)
    st = st_ref[...]
    o = _dot_nt((q * jnp.exp(b)).astype(BF16), st.astype(BF16))
    kh = (k * jnp.exp(b_end - b)).astype(BF16)
    st_ref[...] = st * jnp.exp(b_end[0:1, :]) + mask_t * _dot_tn(v.astype(BF16), kh)

    row = lax.broadcasted_iota(jnp.int32, (nb, GLA_KW), 0)
    parts = []
    for j in range(nb):
        valid = (row >= j) if fwd else (row <= j)
        e = jnp.exp(jnp.where(valid, b - bcast(b_ref, r0 + j), -jnp.inf))
        parts.append((e * q * bcast(k_ref, r0 + j)).astype(BF16))
    r = _dot(jnp.concatenate(parts, axis=0), ind)
    for j in range(nb):
        vj = jnp.concatenate([bcast(v0_ref, r0 + j), bcast(v1_ref, r0 + j)], axis=1)
        o = o + r[j * nb:(j + 1) * nb] * vj
    o_ref[rows, :] = o


def _gla_kernel(qf_ref, kf_ref, v0f_ref, v1f_ref, gf_ref, qb_ref, kb_ref, v0b_ref, v1b_ref, gb_ref,
                s0_ref, trif_ref, trib_ref, ind_ref, maskt_ref, of_ref, ob_ref, sfin_ref, stf, stb, bf, bb):
    t = pl.program_id(1)
    tt = qf_ref.shape[0]
    nblk = tt // GLA_BLOCK

    @pl.when(t == 0)
    def _():
        stf[...] = s0_ref[0]
        stb[...] = s0_ref[1]

    for g_ref, tri_ref, b_sc in ((gf_ref, trif_ref, bf), (gb_ref, trib_ref, bb)):
        hi, mid, lo = _split3(g_ref[...])
        tri = tri_ref[...]
        b_sc[...] = _dot(tri, hi) + _dot(tri, mid) + _dot(tri, lo)

    ind = ind_ref[...]
    mask_t = maskt_ref[...]

    def step(i, carry):
        rf = pl.multiple_of(i * GLA_BLOCK, GLA_BLOCK)
        rb = pl.multiple_of((nblk - 1 - i) * GLA_BLOCK, GLA_BLOCK)
        _gla_block(qf_ref, kf_ref, v0f_ref, v1f_ref, bf, stf, of_ref, rf, True, ind, mask_t)
        _gla_block(qb_ref, kb_ref, v0b_ref, v1b_ref, bb, stb, ob_ref, rb, False, ind, mask_t)
        return carry

    lax.fori_loop(0, nblk, step, 0)

    @pl.when(t == pl.num_programs(1) - 1)
    def _():
        sfin_ref[0] = stf[...]
        sfin_ref[1] = stb[...]


def _gla_consts(tt):
    i = np.arange(tt)
    same = (i[:, None] // GLA_BLOCK) == (i[None, :] // GLA_BLOCK)
    tri_f = (same & (i[None, :] <= i[:, None])).astype(np.float32)
    tri_b = (same & (i[None, :] >= i[:, None])).astype(np.float32)
    kd = np.arange(GLA_KW) // GLA_DK
    ve = np.arange(GLA_VW) // GLA_DV
    ind = (kd[:, None] == ve[None, :]).astype(np.float32)
    return (jnp.asarray(tri_f, BF16), jnp.asarray(tri_b, BF16), jnp.asarray(ind, BF16), jnp.asarray(ind.T, F32))


def _gla_call(qk, gv, gates, s0, tt):
    bsz, n, _ = qk.shape
    nt = n // tt
    tri_f, tri_b, ind, mask_t = _gla_consts(tt)
    half = lambda rev, c: pl.BlockSpec((None, tt, LANES), lambda b, t: (b, nt - 1 - t if rev else t, c))
    full = lambda rev: pl.BlockSpec((None, tt, GLA_VW), lambda b, t: (b, nt - 1 - t if rev else t, 0))
    st = pl.BlockSpec((None, 2, GLA_VW, GLA_KW), lambda b, t: (b, 0, 0, 0))
    fwd_in = [half(False, 0), half(False, 1), half(False, 0), half(False, 1), half(False, 0)]
    bwd_in = [half(True, 0), half(True, 1), half(True, 0), half(True, 1), half(True, 1)]
    return pl.pallas_call(
        _gla_kernel,
        grid=(bsz, nt),
        in_specs=fwd_in + bwd_in + [st, _full((tt, tt)), _full((tt, tt)), _full((GLA_KW, GLA_VW)),
                                    _full((GLA_VW, GLA_KW))],
        out_specs=[full(False), full(True), st],
        out_shape=[jax.ShapeDtypeStruct((bsz, n, GLA_VW), F32), jax.ShapeDtypeStruct((bsz, n, GLA_VW), F32),
                   jax.ShapeDtypeStruct((bsz, 2, GLA_VW, GLA_KW), F32)],
        scratch_shapes=[pltpu.VMEM((GLA_VW, GLA_KW), F32), pltpu.VMEM((GLA_VW, GLA_KW), F32),
                        pltpu.VMEM((tt, GLA_KW), F32), pltpu.VMEM((tt, GLA_KW), F32)],
        compiler_params=_cparams(("parallel", "arbitrary")), name="gla",
    )(qk, qk, gv, gv, gates, qk, qk, gv, gv, gates, s0, tri_f, tri_b, ind, mask_t)


def _attn_kernel(q_ref, k_ref, vt_ref, o_ref, s_sc, m_sc, acc_sc, *, tk):
    nk = k_ref.shape[0] // tk
    q = q_ref[...]
    m_sc[...] = jnp.full(m_sc.shape, -jnp.inf, F32)
    acc_sc[...] = jnp.zeros(acc_sc.shape, F32)

    def scores(j, slot):
        r0 = pl.multiple_of(j * tk, tk)
        s_sc[slot] = _dot_nt(k_ref[pl.ds(r0, tk), :], q)

    def update(j, slot):
        s = s_sc[slot]
        m_prev = m_sc[...]
        m_new = jnp.maximum(m_prev, jnp.max(s, axis=0, keepdims=True))
        alpha = jnp.exp2(m_prev - m_new)
        p = jnp.exp2(s - m_new).astype(BF16)
        acc_sc[...] = alpha * acc_sc[...] + _dot(vt_ref[j], p)
        m_sc[...] = m_new

    scores(0, 0)

    def pair(i, carry):
        j = 2 * i
        scores(j + 1, 1)
        update(j, 0)
        scores(j + 2, 0)
        update(j + 1, 1)
        return carry

    lax.fori_loop(0, (nk - 1) // 2, pair, 0)
    if nk % 2 == 0:
        scores(nk - 1, 1)
        update(nk - 2, 0)
        update(nk - 1, 1)
    else:
        update(nk - 1, 0)
    acc = acc_sc[...]
    o_t = acc[0:MLA_V] * (1.0 / acc[MLA_V:MLA_V + 1])
    o_ref[...] = o_t.T.astype(o_ref.dtype)


def _attn_call(q, k, v, tq, tk):
    bsz, nh, nq, dq = q.shape
    nk = k.shape[2]
    nkt = nk // tk
    pad = jnp.zeros(v.shape[:-1] + (MLA_VROWS - MLA_V - 1,), v.dtype)
    va = jnp.concatenate([v, jnp.ones(v.shape[:-1] + (1,), v.dtype), pad], axis=-1)
    vt = va.reshape(bsz, nh, nkt, tk, MLA_VROWS).transpose(0, 1, 2, 4, 3)
    return pl.pallas_call(
        functools.partial(_attn_kernel, tk=tk),
        grid=(bsz, nh, nq // tq),
        in_specs=[pl.BlockSpec((None, None, tq, dq), lambda b, h, i: (b, h, i, 0)),
                  pl.BlockSpec((None, None, nk, dq), lambda b, h, i: (b, h, 0, 0)),
                  pl.BlockSpec((None, None, nkt, MLA_VROWS, tk), lambda b, h, i: (b, h, 0, 0, 0))],
        out_specs=pl.BlockSpec((None, None, tq, MLA_V), lambda b, h, i: (b, h, i, 0)),
        out_shape=jax.ShapeDtypeStruct((bsz, nh, nq, MLA_V), BF16),
        scratch_shapes=[pltpu.VMEM((2, tk, tq), F32), pltpu.VMEM((1, tq), F32), pltpu.VMEM((MLA_VROWS, tq), F32)],
        compiler_params=_cparams(("parallel", "parallel", "parallel")), name="attn",
    )(q, k, vt)


def _out_kernel(x_ref, ysgu_ref, of_ref, ob_ref, gr_ref, ymla_ref, g1_ref, sh2_ref, sc2_ref, g2_ref, glaw_ref,
                hsum_ref, wout_ref, n2w_ref, w1_ref, w2_ref, fnw_ref, o_ref, y_sc, *, final_norm):
    o = of_ref[...] + ob_ref[...]
    o2 = o * o
    hi = o2.astype(BF16)
    lo = (o2 - hi.astype(F32)).astype(BF16)
    ms = (_dot(hi, hsum_ref[...]) + _dot(lo, hsum_ref[...])) * (1.0 / GLA_DV)
    y_gla = o * lax.rsqrt(ms + EPS) * glaw_ref[...] * _silu(gr_ref[...])
    y_sc[:, 0:SGU_WIDTH] = ysgu_ref[...]
    y_sc[:, SGU_WIDTH:SGU_WIDTH + GLA_VW] = y_gla.astype(BF16)
    for hh in range(MLA_HEADS):
        off = SGU_WIDTH + GLA_VW + hh * MLA_V
        y_sc[:, off:off + MLA_V] = ymla_ref[hh]
    x1 = x_ref[...] + g1_ref[...] * _dot(y_sc[...], wout_ref[...])
    hn = (_rms(x1, n2w_ref[...]) * (1.0 + sc2_ref[...]) + sh2_ref[...]).astype(BF16)
    dff = w1_ref.shape[1]
    fc = 1024
    acc = jnp.zeros(x1.shape, F32)
    for c in range(dff // fc):
        a = jnp.maximum(_dot(hn, w1_ref[:, c * fc:(c + 1) * fc]), 0.0)
        acc = acc + _dot((a * a).astype(BF16), w2_ref[c * fc:(c + 1) * fc, :])
    x2 = x1 + g2_ref[...] * acc
    if final_norm:
        x2 = _rms(x2, fnw_ref[...])
    o_ref[...] = x2


def _out_call(x, ysgu, o_f, o_b, gr, ymla, g1, sh2, sc2, g2, lw, fnw, tm, final_norm):
    bsz, n, d = x.shape
    nh = MLA_HEADS
    mix = SGU_WIDTH + GLA_VW + nh * MLA_V
    dff = lw["w1"].shape[1]
    tok = lambda w: pl.BlockSpec((None, tm, w), lambda b, i: (b, i, 0))
    vec = pl.BlockSpec((None, 1, d), lambda b, i: (b, 0, 0))
    once = lambda shape: pl.BlockSpec(shape, lambda *_: (0,) * len(shape), pipeline_mode=pl.Buffered(1))
    in_specs = [tok(d), tok(SGU_WIDTH), tok(GLA_VW), tok(GLA_VW), tok(GLA_VW),
                pl.BlockSpec((None, nh, tm, MLA_V), lambda b, i: (b, 0, i, 0)),
                vec, vec, vec, vec, _full((1, GLA_VW)), _full((GLA_VW, GLA_VW)),
                once((mix, d)), _full((1, d)), once((d, dff)), once((dff, d)), _full((1, d))]
    return pl.pallas_call(
        functools.partial(_out_kernel, final_norm=final_norm),
        grid=(bsz, n // tm), in_specs=in_specs, out_specs=tok(d),
        out_shape=jax.ShapeDtypeStruct((bsz, n, d), F32),
        scratch_shapes=[pltpu.VMEM((tm, mix), BF16)],
        compiler_params=_cparams(("parallel", "parallel")), name="out_ffn",
    )(x, ysgu, o_f, o_b, gr, ymla, g1, sh2, sc2, g2, lw["glaw"], lw["hsum"], lw["w_out"], lw["n2w"],
      lw["w1"], lw["w2"], fnw)


def _rope_tables(n):
    half = MLA_ROPE // 2
    pos = np.arange(n)
    freq = ROPE_BASE ** (-np.arange(half // 2, dtype=np.float64) * 2.0 / half)
    ang_r = (pos // GRID_W)[:, None] * freq[None, :]
    ang_c = (pos % GRID_W)[:, None] * freq[None, :]
    cos = np.zeros((n, LANES), np.float32)
    sin = np.zeros((n, LANES), np.float32)
    cos[:, 0:64] = np.concatenate([np.cos(ang_r), np.cos(ang_r), np.cos(ang_c), np.cos(ang_c)], axis=1)
    sin[:, 0:64] = np.concatenate([-np.sin(ang_r), np.sin(ang_r), -np.sin(ang_c), np.sin(ang_c)], axis=1)
    return jnp.asarray(cos), jnp.asarray(sin)


def _identity_tables(n):
    cos = np.zeros((n, LANES), np.float32)
    cos[:, 0:64] = 1.0
    return jnp.asarray(cos), jnp.zeros((n, LANES), F32)


_ROT_PARTNER = np.concatenate([np.arange(16, 32), np.arange(0, 16), np.arange(48, 64), np.arange(32, 48)])


def _layer_weights(l, p):
    w_in = p["w_in"][l]
    d = w_in.shape[0]
    o_gk, o_gv, o_gf, o_gb, o_ckv, o_kr, o_su, o_sv, o_gq, o_gr, o_dq = (
        0, 128, 384, 400, 416, 672, 736, 992, 1248, 1376, 1632)
    col = lambda o, w: w_in[:, o:o + w]
    kr = col(o_kr, MLA_ROPE)
    z = lambda w: jnp.zeros((d, w), w_in.dtype)
    w_in_p = jnp.concatenate([
        col(o_gq, 128), col(o_gk, 128), col(o_gv, 256), col(o_gr, 256), col(o_su, 256), col(o_sv, 256),
        col(o_ckv, 256), col(o_dq, 256),
        kr, col(o_gf, GLA_RANK), col(o_gb, GLA_RANK), z(32),
        kr[:, _ROT_PARTNER], z(64)], axis=1).astype(BF16)

    wg = jnp.zeros((LANES, 2 * GLA_KW), F32)
    wg = wg.at[64:80, 0:GLA_KW].set(p["gla_wg_fwd"][l]).at[80:96, GLA_KW:].set(p["gla_wg_bwd"][l])
    bg = jnp.concatenate([p["gla_bg_fwd"][l], p["gla_bg_bwd"][l]])[None, :]

    w_uq = p["mla_w_uq"][l].reshape(MLA_RANK, MLA_HEADS, MLA_NOPE + MLA_ROPE)
    nope = w_uq[:, :, :MLA_NOPE].reshape(MLA_RANK, -1)
    rope = w_uq[:, :, MLA_NOPE:]
    pad = lambda t: jnp.pad(t, ((0, 0), (0, 0), (0, LANES - MLA_ROPE))).reshape(MLA_RANK, -1)
    wuq = jnp.concatenate([nope, pad(rope), pad(rope[:, :, _ROT_PARTNER])], axis=1).astype(BF16)

    w_ukv = p["mla_w_ukv"][l].reshape(MLA_RANK, MLA_HEADS, MLA_NOPE + MLA_V)
    wukv = jnp.concatenate([w_ukv[:, :, :MLA_NOPE].reshape(MLA_RANK, -1),
                            w_ukv[:, :, MLA_NOPE:].reshape(MLA_RANK, -1)], axis=1).astype(BF16)

    hv = np.arange(GLA_VW) // GLA_DV
    return dict(
        n1w=p["norm1_w"][l][None, :], w_in=w_in_p,
        sgw=p["sgu_norm_w"][l][None, :], sgb=p["sgu_norm_b"][l][None, :],
        ws=p["sgu_w"][l].reshape(SGU_HEADS * SGU_CHUNK, SGU_CHUNK).astype(BF16),
        bs=jnp.repeat(p["sgu_b"][l].T, SGU_HEAD_DIM, axis=1),
        wg=wg.astype(BF16), bg=bg,
        qnw=p["mla_q_norm_w"][l][None, :], wuq=wuq, kvnw=p["mla_kv_norm_w"][l][None, :], wukv=wukv,
        glaw=jnp.tile(p["gla_norm_w"][l], GLA_HEADS)[None, :],
        hsum=jnp.asarray((hv[:, None] == hv[None, :]).astype(np.float32), BF16),
        w_out=p["w_out"][l].astype(BF16), n2w=p["norm2_w"][l][None, :],
        w1=p["w_ff1"][l].astype(BF16), w2=p["w_ff2"][l].astype(BF16))


def _pick(n, pref):
    for t in pref:
        if n % t == 0:
            return t
    return n


def kernel(x, c, ctx, c_ctx, w_mod, b_mod, norm1_w, w_in, w_out, sgu_norm_w, sgu_norm_b, sgu_w, sgu_b, gla_wg_fwd, gla_bg_fwd, gla_wg_bwd, gla_bg_bwd, gla_norm_w, mla_q_norm_w, mla_w_uq, mla_kv_norm_w, mla_w_ukv, norm2_w, w_ff1, w_ff2, final_norm_w):
    p = dict(norm1_w=norm1_w, w_in=w_in, w_out=w_out, sgu_norm_w=sgu_norm_w, sgu_norm_b=sgu_norm_b, sgu_w=sgu_w,
             sgu_b=sgu_b, gla_wg_fwd=gla_wg_fwd, gla_bg_fwd=gla_bg_fwd, gla_wg_bwd=gla_wg_bwd, gla_bg_bwd=gla_bg_bwd,
             gla_norm_w=gla_norm_w, mla_q_norm_w=mla_q_norm_w, mla_w_uq=mla_w_uq, mla_kv_norm_w=mla_kv_norm_w,
             mla_w_ukv=mla_w_ukv, norm2_w=norm2_w, w_ff1=w_ff1, w_ff2=w_ff2)
    bsz, n, d = x.shape
    nc = ctx.shape[1]
    depth = w_mod.shape[0]
    fnw = final_norm_w[None, :]

    cvecs = jnp.zeros((8, d), F32).at[0:bsz].set(c).at[bsz].set(c_ctx)
    mod = _mod_call(cvecs, w_mod, b_mod)

    tm = _pick(n, (512, 256, 128))
    tmc = _pick(nc, (256, 128))
    tq = _pick(n, (512, 256, 128))
    tk = _pick(n + nc, (768, 512, 256, 128))
    cos, sin = _rope_tables(n)
    cos_c, sin_c = _identity_tables(nc)

    xc = ctx
    for l in range(depth):
        last = l == depth - 1
        lw = _layer_weights(l, p)
        m = mod[l, 0:bsz].reshape(bsz, 1, 6, d)
        sh1, sc1, g1, sh2, sc2, g2 = [m[:, :, i] for i in range(6)]
        mc = jnp.broadcast_to(mod[l, bsz].reshape(1, 1, 6, d), (bsz, 1, 6, d))
        sh1c, sc1c, g1c, sh2c, sc2c, g2c = [mc[:, :, i] for i in range(6)]

        ysgu_c, qk_c, gv_c, gate_c, gr_c, q_c, k_c, v_c = _proj_call(xc, sh1c, sc1c, lw, cos_c, sin_c, tmc)
        zero = jnp.zeros((bsz, 2, GLA_VW, GLA_KW), F32)
        of_c, ob_c, s_ctx = _gla_call(qk_c, gv_c, gate_c, zero, tmc)

        ysgu, qk, gv, gate, gr, q, k, v = _proj_call(x, sh1, sc1, lw, cos, sin, tm)
        o_f, o_b, _ = _gla_call(qk, gv, gate, s_ctx, tm)
        ymla = _attn_call(q, jnp.concatenate([k, k_c], axis=2), jnp.concatenate([v, v_c], axis=2), tq, tk)
        x = _out_call(x, ysgu, o_f, o_b, gr, ymla, g1, sh2, sc2, g2, lw, fnw, tm, last)

        if not last:
            ymla_c = _attn_call(q_c, k_c, v_c, tmc, tmc)
            xc = _out_call(xc, ysgu_c, of_c, ob_c, gr_c, ymla_c, g1c, sh2c, sc2c, g2c, lw, fnw, tmc, False)
    return x
```

```python
import functools
import math

import numpy as np
import jax
import jax.numpy as jnp
from jax import lax
from jax.experimental import pallas as pl
from jax.experimental.pallas import tpu as pltpu

F32 = jnp.float32
BF16 = jnp.bfloat16

EPS = 1e-6
GRID_W = 64
ROPE_BASE = 10000.0
LANES = 128
SGU_HEADS = 4
SGU_HEAD_DIM = 64
SGU_WIDTH = 256
SGU_CHUNK = 128
GLA_HEADS = 4
GLA_DK = 32
GLA_DV = 64
GLA_KW = 128
GLA_VW = 256
GLA_RANK = 16
GLA_TAU = 16.0
GLA_BLOCK = 16
GLA_CHUNK = 64
GLA_SAFE_DECAY = 60.0
MLA_HEADS = 4
MLA_RANK = 256
MLA_NOPE = 128
MLA_ROPE = 64
MLA_V = 128
MLA_QK = 256
MLA_VROWS = 144
VMEM_LIMIT = 56 * 1024 * 1024

C_GQ, C_GK, C_GV, C_GR, C_SU, C_SV, C_CKV, C_DQ, C_ROPE_A, C_ROPE_B, C_END = (
    0, 128, 256, 512, 768, 1024, 1280, 1536, 1792, 1920, 2048)


def _cparams(sem):
    return pltpu.CompilerParams(dimension_semantics=sem, vmem_limit_bytes=VMEM_LIMIT)


def _dot(a, b):
    return jnp.dot(a, b, preferred_element_type=F32)


def _dot_nt(a, b):
    return lax.dot_general(a, b, (((1,), (1,)), ((), ())), preferred_element_type=F32)


def _dot_tn(a, b):
    return lax.dot_general(a, b, (((0,), (0,)), ((), ())), preferred_element_type=F32)


def _rms(x, w):
    return x * lax.rsqrt(jnp.mean(x * x, axis=-1, keepdims=True) + EPS) * w


def _gelu_tanh(x):
    return 0.5 * x * (1.0 + jnp.tanh(math.sqrt(2.0 / math.pi) * (x + 0.044715 * (x * x * x))))


def _silu(x):
    return x * jax.nn.sigmoid(x)


def _full(shape):
    return pl.BlockSpec(shape, lambda *_: (0,) * len(shape))


def _mod_kernel(c_ref, w_ref, b_ref, o_ref):
    s = _silu(c_ref[...]).astype(BF16)
    o_ref[...] = _dot(s, w_ref[...].astype(BF16)) + b_ref[...]


def _mod_call(cvecs, w_mod, b_mod):
    nl, d, d6 = w_mod.shape
    tn = 1536
    return pl.pallas_call(
        _mod_kernel,
        grid=(nl, d6 // tn),
        in_specs=[pl.BlockSpec((8, d), lambda l, j: (0, 0)),
                  pl.BlockSpec((None, d, tn), lambda l, j: (l, 0, j)),
                  pl.BlockSpec((None, 1, tn), lambda l, j: (l, 0, j))],
        out_specs=pl.BlockSpec((None, 8, tn), lambda l, j: (l, 0, j)),
        out_shape=jax.ShapeDtypeStruct((nl, 8, d6), F32),
        compiler_params=_cparams(("parallel", "parallel")),
        name="mod",
    )(cvecs, w_mod, b_mod.reshape(nl, 1, d6))


def _proj_kernel(x_ref, sh_ref, sc_ref, n1w_ref, win_ref, sgw_ref, sgb_ref, ws_ref, bs_ref, wg_ref, bg_ref,
                 qnw_ref, wuq_ref, kvnw_ref, wukv_ref, cos_ref, sin_ref,
                 ysgu_ref, qk_ref, gv_ref, gate_ref, gr_ref, q_ref, k_ref, v_ref, *, q_scale):
    tm = x_ref.shape[0]
    h = _rms(x_ref[...], n1w_ref[...]) * (1.0 + sc_ref[...]) + sh_ref[...]
    hb = h.astype(BF16)

    def proj(lo, hi):
        return _dot(hb, win_ref[:, lo:hi])

    qk_ref[:, 0:GLA_KW] = proj(C_GQ, C_GK) * (GLA_DK ** -0.5)
    qk_ref[:, GLA_KW:] = proj(C_GK, C_GV)
    gv_ref[...] = proj(C_GV, C_GR)
    gr_ref[...] = proj(C_GR, C_SU)

    u = _gelu_tanh(proj(C_SU, C_SV))
    g = _gelu_tanh(proj(C_SV, C_CKV))
    mu = jnp.mean(g, axis=-1, keepdims=True)
    gc = g - mu
    var = jnp.mean(gc * gc, axis=-1, keepdims=True)
    vln = (gc * lax.rsqrt(var + EPS) * sgw_ref[...] + sgb_ref[...]).astype(BF16)
    head_of_lane = lax.broadcasted_iota(jnp.int32, (SGU_CHUNK, SGU_WIDTH), 1) // SGU_HEAD_DIM
    ws = ws_ref[...]
    for c in range(tm // SGU_CHUNK):
        rows = slice(c * SGU_CHUNK, (c + 1) * SGU_CHUNK)
        r = _dot(ws, vln[rows, :])
        s = r[0:SGU_CHUNK]
        for hh in range(1, SGU_HEADS):
            s = jnp.where(head_of_lane == hh, r[hh * SGU_CHUNK:(hh + 1) * SGU_CHUNK], s)
        ysgu_ref[rows, :] = (u[rows, :] * (s + bs_ref[...])).astype(BF16)

    ga = proj(C_ROPE_A, C_ROPE_B)
    gb = proj(C_ROPE_B, C_END)
    z = _dot(ga.astype(BF16), wg_ref[...]) + bg_ref[...]
    log_sig = jnp.minimum(z, 0.0) - jnp.log1p(jnp.exp(-jnp.abs(z)))
    gate_ref[...] = log_sig * (1.0 / GLA_TAU)
    cos = cos_ref[...]
    sin = sin_ref[...]
    k_rope = (ga * cos + gb * sin).astype(BF16)

    ckv = _rms(proj(C_CKV, C_DQ), kvnw_ref[...]).astype(BF16)
    kv = _dot(ckv, wukv_ref[...])
    for hh in range(MLA_HEADS):
        k_ref[hh, :, 0:MLA_NOPE] = kv[:, hh * MLA_NOPE:(hh + 1) * MLA_NOPE].astype(BF16)
        k_ref[hh, :, MLA_NOPE:] = k_rope
        off = MLA_HEADS * MLA_NOPE + hh * MLA_V
        v_ref[hh] = kv[:, off:off + MLA_V].astype(BF16)

    cq = _rms(proj(C_DQ, C_ROPE_A), qnw_ref[...]).astype(BF16)
    qq = _dot(cq, wuq_ref[...])
    nq = MLA_HEADS * LANES
    for hh in range(MLA_HEADS):
        sl = slice(hh * LANES, (hh + 1) * LANES)
        q_ref[hh, :, 0:MLA_NOPE] = (qq[:, sl] * q_scale).astype(BF16)
        rot = qq[:, nq + hh * LANES:nq + (hh + 1) * LANES] * cos + qq[:, 2 * nq + hh * LANES:2 * nq + (hh + 1) * LANES] * sin
        q_ref[hh, :, MLA_NOPE:] = (rot * q_scale).astype(BF16)


def _proj_call(x, sh, sc, lw, cos, sin, tm):
    bsz, n, d = x.shape
    nh = MLA_HEADS
    q_scale = (MLA_NOPE + MLA_ROPE) ** -0.5 * math.log2(math.e)
    tok = lambda w: pl.BlockSpec((None, tm, w), lambda b, i: (b, i, 0))
    vec = pl.BlockSpec((None, 1, d), lambda b, i: (b, 0, 0))
    heads = lambda w: pl.BlockSpec((None, nh, tm, w), lambda b, i: (b, 0, i, 0))
    tab = pl.BlockSpec((tm, LANES), lambda b, i: (i, 0))
    in_specs = [tok(d), vec, vec, _full((1, d)), _full((d, C_END)),
                _full((1, SGU_WIDTH)), _full((1, SGU_WIDTH)), _full((SGU_HEADS * SGU_CHUNK, SGU_CHUNK)),
                _full((SGU_CHUNK, SGU_WIDTH)), _full((LANES, 2 * GLA_KW)), _full((1, 2 * GLA_KW)),
                _full((1, MLA_RANK)), _full((MLA_RANK, 3 * nh * LANES)),
                _full((1, MLA_RANK)), _full((MLA_RANK, nh * (MLA_NOPE + MLA_V))), tab, tab]
    out_specs = [tok(SGU_WIDTH), tok(2 * GLA_KW), tok(GLA_VW), tok(2 * GLA_KW), tok(GLA_VW),
                 heads(MLA_QK), heads(MLA_QK), heads(MLA_V)]
    out_shape = [jax.ShapeDtypeStruct((bsz, n, SGU_WIDTH), BF16),
                 jax.ShapeDtypeStruct((bsz, n, 2 * GLA_KW), F32),
                 jax.ShapeDtypeStruct((bsz, n, GLA_VW), F32),
                 jax.ShapeDtypeStruct((bsz, n, 2 * GLA_KW), F32),
                 jax.ShapeDtypeStruct((bsz, n, GLA_VW), F32),
                 jax.ShapeDtypeStruct((bsz, nh, n, MLA_QK), BF16),
                 jax.ShapeDtypeStruct((bsz, nh, n, MLA_QK), BF16),
                 jax.ShapeDtypeStruct((bsz, nh, n, MLA_V), BF16)]
    return pl.pallas_call(
        functools.partial(_proj_kernel, q_scale=q_scale),
        grid=(bsz, n // tm), in_specs=in_specs, out_specs=out_specs, out_shape=out_shape,
        compiler_params=_cparams(("parallel", "parallel")), name="proj",
    )(x, sh, sc, lw["n1w"], lw["w_in"], lw["sgw"], lw["sgb"], lw["ws"], lw["bs"], lw["wg"], lw["bg"],
      lw["qnw"], lw["wuq"], lw["kvnw"], lw["wukv"], cos, sin)


def _split3(x):
    hi = x.astype(BF16)
    r1 = x - hi.astype(F32)
    mid = r1.astype(BF16)
    lo = (r1 - mid.astype(F32)).astype(BF16)
    return hi, mid, lo


def _row_bcast(ref, r, n):
    return ref[pl.ds(r, n, stride=0), :]


def _gla_block(q_ref, k_ref, v0_ref, v1_ref, b_ref, st_ref, o_ref, r0, fwd, ind, mask_t):
    nb = GLA_BLOCK
    rows = pl.ds(r0, nb)
    bcast = lambda ref, r: _row_bcast(ref, r, nb)
    q = q_ref[rows, :]
    k = k_ref[rows, :]
    v = jnp.concatenate([v0_ref[rows, :], v1_ref[rows, :]], axis=1)
    b = b_ref[rows, :]
    b_end = bcast(b_ref, r0 + (nb - 1) if fwd else r0)
    st = st_ref[...]
    o = _dot_nt((q * jnp.exp(b)).astype(BF16), st.astype(BF16))
    kh = (k * jnp.exp(b_end - b)).astype(BF16)
    st_ref[...] = st * jnp.exp(b_end[0:1, :]) + mask_t * _dot_tn(v.astype(BF16), kh)

    row = lax.broadcasted_iota(jnp.int32, (nb, GLA_KW), 0)
    parts = []
    for j in range(nb):
        valid = (row >= j) if fwd else (row <= j)
        e = jnp.exp(jnp.where(valid, b - bcast(b_ref, r0 + j), -jnp.inf))
        parts.append((e * q * bcast(k_ref, r0 + j)).astype(BF16))
    r = _dot(jnp.concatenate(parts, axis=0), ind)
    for j in range(nb):
        vj = jnp.concatenate([bcast(v0_ref, r0 + j), bcast(v1_ref, r0 + j)], axis=1)
        o = o + r[j * nb:(j + 1) * nb] * vj
    o_ref[rows, :] = o


def _gla_chunk(q_ref, k_ref, v_ref, b_ref, st_ref, o_ref, r0, fwd, qmask, tri, mask_t):
    nc = GLA_CHUNK
    rows = pl.ds(r0, nc)
    q = q_ref[rows, :]
    k = k_ref[rows, :]
    vb = v_ref[rows, :].astype(BF16)
    b = b_ref[rows, :]
    e_end = jnp.exp(b_ref[pl.ds(r0 + (nc - 1) if fwd else r0, 1), :])
    qt = q * jnp.exp(b)
    kt = k * jnp.exp(-b)
    st = st_ref[...]
    o = _dot_nt(qt.astype(BF16), st.astype(BF16))
    st_ref[...] = st * e_end + mask_t * _dot_tn(vb, (kt * e_end).astype(BF16))
    q4 = (jnp.concatenate([qt] * GLA_HEADS, axis=0) * qmask).astype(BF16)
    a = jnp.where(tri, _dot_nt(q4, kt.astype(BF16)), 0.0)
    r = _dot(a.astype(BF16), vb)
    head_of_lane = lax.broadcasted_iota(jnp.int32, (nc, GLA_VW), 1) // GLA_DV
    intra = r[0:nc]
    for h in range(1, GLA_HEADS):
        intra = jnp.where(head_of_lane == h, r[h * nc:(h + 1) * nc], intra)
    o_ref[rows, :] = o + intra


def _cumsum_blocks(g_ref, tri_ref, b_sc):
    tri = tri_ref[...]
    span = tri.shape[0]
    for c in range(g_ref.shape[0] // span):
        rows = slice(c * span, (c + 1) * span)
        hi, mid, lo = _split3(g_ref[rows, :])
        b_sc[rows, :] = _dot(tri, hi) + _dot(tri, mid) + _dot(tri, lo)


def _gla_kernel(qf_ref, kf_ref, v0f_ref, v1f_ref, vf_ref, gf_ref, qb_ref, kb_ref, v0b_ref, v1b_ref, vb_ref, gb_ref,
                s0_ref, t16f_ref, t16b_ref, t64f_ref, t64b_ref, ind_ref, maskt_ref, qmask_ref,
                of_ref, ob_ref, sfin_ref, stf, stb, bf, bb):
    t = pl.program_id(1)
    tt = qf_ref.shape[0]

    @pl.when(t == 0)
    def _():
        stf[...] = s0_ref[0]
        stb[...] = s0_ref[1]

    _cumsum_blocks(gf_ref, t64f_ref, bf)
    _cumsum_blocks(gb_ref, t64b_ref, bb)
    lowest = jnp.minimum(jnp.min(bf[...]), jnp.min(bb[...]))
    mask_t = maskt_ref[...]

    @pl.when(lowest > -GLA_SAFE_DECAY)
    def _():
        nch = tt // GLA_CHUNK
        qmask = qmask_ref[...]
        row = lax.broadcasted_iota(jnp.int32, (GLA_HEADS * GLA_CHUNK, GLA_CHUNK), 0) % GLA_CHUNK
        col = lax.broadcasted_iota(jnp.int32, (GLA_HEADS * GLA_CHUNK, GLA_CHUNK), 1)
        lower, upper = col <= row, col >= row

        def step(i, carry):
            rf = pl.multiple_of(i * GLA_CHUNK, GLA_CHUNK)
            rb = pl.multiple_of((nch - 1 - i) * GLA_CHUNK, GLA_CHUNK)
            _gla_chunk(qf_ref, kf_ref, vf_ref, bf, stf, of_ref, rf, True, qmask, lower, mask_t)
            _gla_chunk(qb_ref, kb_ref, vb_ref, bb, stb, ob_ref, rb, False, qmask, upper, mask_t)
            return carry

        lax.fori_loop(0, nch, step, 0, unroll=2)

    @pl.when(jnp.logical_not(lowest > -GLA_SAFE_DECAY))
    def _():
        nblk = tt // GLA_BLOCK
        _cumsum_blocks(gf_ref, t16f_ref, bf)
        _cumsum_blocks(gb_ref, t16b_ref, bb)
        ind = ind_ref[...]

        def step(i, carry):
            rf = pl.multiple_of(i * GLA_BLOCK, GLA_BLOCK)
            rb = pl.multiple_of((nblk - 1 - i) * GLA_BLOCK, GLA_BLOCK)
            _gla_block(qf_ref, kf_ref, v0f_ref, v1f_ref, bf, stf, of_ref, rf, True, ind, mask_t)
            _gla_block(qb_ref, kb_ref, v0b_ref, v1b_ref, bb, stb, ob_ref, rb, False, ind, mask_t)
            return carry

        lax.fori_loop(0, nblk, step, 0)

    @pl.when(t == pl.num_programs(1) - 1)
    def _():
        sfin_ref[0] = stf[...]
        sfin_ref[1] = stb[...]


def _tri(span, blk):
    i = np.arange(span)
    same = (i[:, None] // blk) == (i[None, :] // blk)
    lower = (same & (i[None, :] <= i[:, None])).astype(np.float32)
    upper = (same & (i[None, :] >= i[:, None])).astype(np.float32)
    return jnp.asarray(lower, BF16), jnp.asarray(upper, BF16)


def _gla_call(qk, gv, gates, s0, tt):
    bsz, n, _ = qk.shape
    nt = n // tt
    span = min(tt, 256)
    t16f, t16b = _tri(span, GLA_BLOCK)
    t64f, t64b = _tri(span, GLA_CHUNK)
    kd = np.arange(GLA_KW) // GLA_DK
    ve = np.arange(GLA_VW) // GLA_DV
    ind = (kd[:, None] == ve[None, :]).astype(np.float32)
    qmask = (np.repeat(np.arange(GLA_HEADS), GLA_CHUNK)[:, None] == kd[None, :]).astype(np.float32)
    half = lambda rev, c: pl.BlockSpec((None, tt, LANES), lambda b, t: (b, nt - 1 - t if rev else t, c))
    full = lambda rev: pl.BlockSpec((None, tt, GLA_VW), lambda b, t: (b, nt - 1 - t if rev else t, 0))
    st = pl.BlockSpec((None, 2, GLA_VW, GLA_KW), lambda b, t: (b, 0, 0, 0))
    fwd_in = [half(False, 0), half(False, 1), half(False, 0), half(False, 1), full(False), half(False, 0)]
    bwd_in = [half(True, 0), half(True, 1), half(True, 0), half(True, 1), full(True), half(True, 1)]
    return pl.pallas_call(
        _gla_kernel,
        grid=(bsz, nt),
        in_specs=fwd_in + bwd_in + [st] + [_full((span, span))] * 4 + [
            _full((GLA_KW, GLA_VW)), _full((GLA_VW, GLA_KW)), _full((GLA_HEADS * GLA_CHUNK, GLA_KW))],
        out_specs=[full(False), full(True), st],
        out_shape=[jax.ShapeDtypeStruct((bsz, n, GLA_VW), F32), jax.ShapeDtypeStruct((bsz, n, GLA_VW), F32),
                   jax.ShapeDtypeStruct((bsz, 2, GLA_VW, GLA_KW), F32)],
        scratch_shapes=[pltpu.VMEM((GLA_VW, GLA_KW), F32), pltpu.VMEM((GLA_VW, GLA_KW), F32),
                        pltpu.VMEM((tt, GLA_KW), F32), pltpu.VMEM((tt, GLA_KW), F32)],
        compiler_params=_cparams(("parallel", "arbitrary")), name="gla",
    )(qk, qk, gv, gv, gv, gates, qk, qk, gv, gv, gv, gates, s0, t16f, t16b, t64f, t64b,
      jnp.asarray(ind, BF16), jnp.asarray(ind.T, F32), jnp.asarray(qmask, F32))


def _attn_kernel(q_ref, k_ref, vt_ref, o_ref, s_sc, m_sc, acc_sc, *, tk):
    nk = k_ref.shape[0] // tk
    q = q_ref[...]
    m_sc[...] = jnp.full(m_sc.shape, -jnp.inf, F32)
    acc_sc[...] = jnp.zeros(acc_sc.shape, F32)

    def scores(j, slot):
        r0 = pl.multiple_of(j * tk, tk)
        s_sc[slot] = _dot_nt(k_ref[pl.ds(r0, tk), :], q)

    def update(j, slot):
        s = s_sc[slot]
        m_prev = m_sc[...]
        m_new = jnp.maximum(m_prev, jnp.max(s, axis=0, keepdims=True))
        alpha = jnp.exp2(m_prev - m_new)
        p = jnp.exp2(s - m_new).astype(BF16)
        acc_sc[...] = alpha * acc_sc[...] + _dot(vt_ref[j], p)
        m_sc[...] = m_new

    scores(0, 0)

    def pair(i, carry):
        j = 2 * i
        scores(j + 1, 1)
        update(j, 0)
        scores(j + 2, 0)
        update(j + 1, 1)
        return carry

    lax.fori_loop(0, (nk - 1) // 2, pair, 0)
    if nk % 2 == 0:
        scores(nk - 1, 1)
        update(nk - 2, 0)
        update(nk - 1, 1)
    else:
        update(nk - 1, 0)
    acc = acc_sc[...]
    o_t = acc[0:MLA_V] * (1.0 / acc[MLA_V:MLA_V + 1])
    o_ref[...] = o_t.T.astype(o_ref.dtype)


def _attn_call(q, k, v, tq, tk):
    bsz, nh, nq, dq = q.shape
    nk = k.shape[2]
    nkt = nk // tk
    pad = jnp.zeros(v.shape[:-1] + (MLA_VROWS - MLA_V - 1,), v.dtype)
    va = jnp.concatenate([v, jnp.ones(v.shape[:-1] + (1,), v.dtype), pad], axis=-1)
    vt = va.reshape(bsz, nh, nkt, tk, MLA_VROWS).transpose(0, 1, 2, 4, 3)
    return pl.pallas_call(
        functools.partial(_attn_kernel, tk=tk),
        grid=(bsz, nh, nq // tq),
        in_specs=[pl.BlockSpec((None, None, tq, dq), lambda b, h, i: (b, h, i, 0)),
                  pl.BlockSpec((None, None, nk, dq), lambda b, h, i: (b, h, 0, 0)),
                  pl.BlockSpec((None, None, nkt, MLA_VROWS, tk), lambda b, h, i: (b, h, 0, 0, 0))],
        out_specs=pl.BlockSpec((None, None, tq, MLA_V), lambda b, h, i: (b, h, i, 0)),
        out_shape=jax.ShapeDtypeStruct((bsz, nh, nq, MLA_V), BF16),
        scratch_shapes=[pltpu.VMEM((2, tk, tq), F32), pltpu.VMEM((1, tq), F32), pltpu.VMEM((MLA_VROWS, tq), F32)],
        compiler_params=_cparams(("parallel", "parallel", "parallel")), name="attn",
    )(q, k, vt)


def _out_kernel(x_ref, ysgu_ref, of_ref, ob_ref, gr_ref, ymla_ref, g1_ref, sh2_ref, sc2_ref, g2_ref, glaw_ref,
                hsum_ref, wout_ref, n2w_ref, w1_ref, w2_ref, fnw_ref, o_ref, y_sc, *, final_norm):
    o = of_ref[...] + ob_ref[...]
    o2 = o * o
    hi = o2.astype(BF16)
    lo = (o2 - hi.astype(F32)).astype(BF16)
    ms = (_dot(hi, hsum_ref[...]) + _dot(lo, hsum_ref[...])) * (1.0 / GLA_DV)
    y_gla = o * lax.rsqrt(ms + EPS) * glaw_ref[...] * _silu(gr_ref[...])
    y_sc[:, 0:SGU_WIDTH] = ysgu_ref[...]
    y_sc[:, SGU_WIDTH:SGU_WIDTH + GLA_VW] = y_gla.astype(BF16)
    for hh in range(MLA_HEADS):
        off = SGU_WIDTH + GLA_VW + hh * MLA_V
        y_sc[:, off:off + MLA_V] = ymla_ref[hh]
    x1 = x_ref[...] + g1_ref[...] * _dot(y_sc[...], wout_ref[...])
    hn = (_rms(x1, n2w_ref[...]) * (1.0 + sc2_ref[...]) + sh2_ref[...]).astype(BF16)
    dff = w1_ref.shape[1]
    fc = 1024
    acc = jnp.zeros(x1.shape, F32)
    for c in range(dff // fc):
        a = jnp.maximum(_dot(hn, w1_ref[:, c * fc:(c + 1) * fc]), 0.0)
        acc = acc + _dot((a * a).astype(BF16), w2_ref[c * fc:(c + 1) * fc, :])
    x2 = x1 + g2_ref[...] * acc
    if final_norm:
        x2 = _rms(x2, fnw_ref[...])
    o_ref[...] = x2


def _out_call(x, ysgu, o_f, o_b, gr, ymla, g1, sh2, sc2, g2, lw, fnw, tm, final_norm):
    bsz, n, d = x.shape
    nh = MLA_HEADS
    mix = SGU_WIDTH + GLA_VW + nh * MLA_V
    dff = lw["w1"].shape[1]
    tok = lambda w: pl.BlockSpec((None, tm, w), lambda b, i: (b, i, 0))
    vec = pl.BlockSpec((None, 1, d), lambda b, i: (b, 0, 0))
    once = lambda shape: pl.BlockSpec(shape, lambda *_: (0,) * len(shape), pipeline_mode=pl.Buffered(1))
    in_specs = [tok(d), tok(SGU_WIDTH), tok(GLA_VW), tok(GLA_VW), tok(GLA_VW),
                pl.BlockSpec((None, nh, tm, MLA_V), lambda b, i: (b, 0, i, 0)),
                vec, vec, vec, vec, _full((1, GLA_VW)), _full((GLA_VW, GLA_VW)),
                once((mix, d)), _full((1, d)), once((d, dff)), once((dff, d)), _full((1, d))]
    return pl.pallas_call(
        functools.partial(_out_kernel, final_norm=final_norm),
        grid=(bsz, n // tm), in_specs=in_specs, out_specs=tok(d),
        out_shape=jax.ShapeDtypeStruct((bsz, n, d), F32),
        scratch_shapes=[pltpu.VMEM((tm, mix), BF16)],
        compiler_params=_cparams(("parallel", "parallel")), name="out_ffn",
    )(x, ysgu, o_f, o_b, gr, ymla, g1, sh2, sc2, g2, lw["glaw"], lw["hsum"], lw["w_out"], lw["n2w"],
      lw["w1"], lw["w2"], fnw)


def _rope_tables(n):
    half = MLA_ROPE // 2
    pos = np.arange(n)
    freq = ROPE_BASE ** (-np.arange(half // 2, dtype=np.float64) * 2.0 / half)
    ang_r = (pos // GRID_W)[:, None] * freq[None, :]
    ang_c = (pos % GRID_W)[:, None] * freq[None, :]
    cos = np.zeros((n, LANES), np.float32)
    sin = np.zeros((n, LANES), np.float32)
    cos[:, 0:64] = np.concatenate([np.cos(ang_r), np.cos(ang_r), np.cos(ang_c), np.cos(ang_c)], axis=1)
    sin[:, 0:64] = np.concatenate([-np.sin(ang_r), np.sin(ang_r), -np.sin(ang_c), np.sin(ang_c)], axis=1)
    return jnp.asarray(cos), jnp.asarray(sin)


def _identity_tables(n):
    cos = np.zeros((n, LANES), np.float32)
    cos[:, 0:64] = 1.0
    return jnp.asarray(cos), jnp.zeros((n, LANES), F32)


_ROT_PARTNER = np.concatenate([np.arange(16, 32), np.arange(0, 16), np.arange(48, 64), np.arange(32, 48)])


def _layer_weights(l, p):
    w_in = p["w_in"][l]
    d = w_in.shape[0]
    o_gk, o_gv, o_gf, o_gb, o_ckv, o_kr, o_su, o_sv, o_gq, o_gr, o_dq = (
        0, 128, 384, 400, 416, 672, 736, 992, 1248, 1376, 1632)
    col = lambda o, w: w_in[:, o:o + w]
    kr = col(o_kr, MLA_ROPE)
    z = lambda w: jnp.zeros((d, w), w_in.dtype)
    w_in_p = jnp.concatenate([
        col(o_gq, 128), col(o_gk, 128), col(o_gv, 256), col(o_gr, 256), col(o_su, 256), col(o_sv, 256),
        col(o_ckv, 256), col(o_dq, 256),
        kr, col(o_gf, GLA_RANK), col(o_gb, GLA_RANK), z(32),
        kr[:, _ROT_PARTNER], z(64)], axis=1).astype(BF16)

    wg = jnp.zeros((LANES, 2 * GLA_KW), F32)
    wg = wg.at[64:80, 0:GLA_KW].set(p["gla_wg_fwd"][l]).at[80:96, GLA_KW:].set(p["gla_wg_bwd"][l])
    bg = jnp.concatenate([p["gla_bg_fwd"][l], p["gla_bg_bwd"][l]])[None, :]

    w_uq = p["mla_w_uq"][l].reshape(MLA_RANK, MLA_HEADS, MLA_NOPE + MLA_ROPE)
    nope = w_uq[:, :, :MLA_NOPE].reshape(MLA_RANK, -1)
    rope = w_uq[:, :, MLA_NOPE:]
    pad = lambda t: jnp.pad(t, ((0, 0), (0, 0), (0, LANES - MLA_ROPE))).reshape(MLA_RANK, -1)
    wuq = jnp.concatenate([nope, pad(rope), pad(rope[:, :, _ROT_PARTNER])], axis=1).astype(BF16)

    w_ukv = p["mla_w_ukv"][l].reshape(MLA_RANK, MLA_HEADS, MLA_NOPE + MLA_V)
    wukv = jnp.concatenate([w_ukv[:, :, :MLA_NOPE].reshape(MLA_RANK, -1),
                            w_ukv[:, :, MLA_NOPE:].reshape(MLA_RANK, -1)], axis=1).astype(BF16)

    hv = np.arange(GLA_VW) // GLA_DV
    return dict(
        n1w=p["norm1_w"][l][None, :], w_in=w_in_p,
        sgw=p["sgu_norm_w"][l][None, :], sgb=p["sgu_norm_b"][l][None, :],
        ws=p["sgu_w"][l].reshape(SGU_HEADS * SGU_CHUNK, SGU_CHUNK).astype(BF16),
        bs=jnp.repeat(p["sgu_b"][l].T, SGU_HEAD_DIM, axis=1),
        wg=wg.astype(BF16), bg=bg,
        qnw=p["mla_q_norm_w"][l][None, :], wuq=wuq, kvnw=p["mla_kv_norm_w"][l][None, :], wukv=wukv,
        glaw=jnp.tile(p["gla_norm_w"][l], GLA_HEADS)[None, :],
        hsum=jnp.asarray((hv[:, None] == hv[None, :]).astype(np.float32), BF16),
        w_out=p["w_out"][l].astype(BF16), n2w=p["norm2_w"][l][None, :],
        w1=p["w_ff1"][l].astype(BF16), w2=p["w_ff2"][l].astype(BF16))


def _pick(n, pref):
    for t in pref:
        if n % t == 0:
            return t
    return n


def kernel(x, c, ctx, c_ctx, w_mod, b_mod, norm1_w, w_in, w_out, sgu_norm_w, sgu_norm_b, sgu_w, sgu_b, gla_wg_fwd, gla_bg_fwd, gla_wg_bwd, gla_bg_bwd, gla_norm_w, mla_q_norm_w, mla_w_uq, mla_kv_norm_w, mla_w_ukv, norm2_w, w_ff1, w_ff2, final_norm_w):
    p = dict(norm1_w=norm1_w, w_in=w_in, w_out=w_out, sgu_norm_w=sgu_norm_w, sgu_norm_b=sgu_norm_b, sgu_w=sgu_w,
             sgu_b=sgu_b, gla_wg_fwd=gla_wg_fwd, gla_bg_fwd=gla_bg_fwd, gla_wg_bwd=gla_wg_bwd, gla_bg_bwd=gla_bg_bwd,
             gla_norm_w=gla_norm_w, mla_q_norm_w=mla_q_norm_w, mla_w_uq=mla_w_uq, mla_kv_norm_w=mla_kv_norm_w,
             mla_w_ukv=mla_w_ukv, norm2_w=norm2_w, w_ff1=w_ff1, w_ff2=w_ff2)
    bsz, n, d = x.shape
    nc = ctx.shape[1]
    depth = w_mod.shape[0]
    fnw = final_norm_w[None, :]

    cvecs = jnp.zeros((8, d), F32).at[0:bsz].set(c).at[bsz].set(c_ctx)
    mod = _mod_call(cvecs, w_mod, b_mod)

    tm = _pick(n, (512, 256, 128))
    tmc = _pick(nc, (256, 128))
    tq = _pick(n, (512, 256, 128))
    tk = _pick(n + nc, (768, 512, 256, 128))
    cos, sin = _rope_tables(n)
    cos_c, sin_c = _identity_tables(nc)

    xc = ctx
    for l in range(depth):
        last = l == depth - 1
        lw = _layer_weights(l, p)
        m = mod[l, 0:bsz].reshape(bsz, 1, 6, d)
        sh1, sc1, g1, sh2, sc2, g2 = [m[:, :, i] for i in range(6)]
        mc = jnp.broadcast_to(mod[l, bsz].reshape(1, 1, 6, d), (bsz, 1, 6, d))
        sh1c, sc1c, g1c, sh2c, sc2c, g2c = [mc[:, :, i] for i in range(6)]

        ysgu_c, qk_c, gv_c, gate_c, gr_c, q_c, k_c, v_c = _proj_call(xc, sh1c, sc1c, lw, cos_c, sin_c, tmc)
        zero = jnp.zeros((bsz, 2, GLA_VW, GLA_KW), F32)
        of_c, ob_c, s_ctx = _gla_call(qk_c, gv_c, gate_c, zero, tmc)

        ysgu, qk, gv, gate, gr, q, k, v = _proj_call(x, sh1, sc1, lw, cos, sin, tm)
        o_f, o_b, _ = _gla_call(qk, gv, gate, s_ctx, tm)
        ymla = _attn_call(q, jnp.concatenate([k, k_c], axis=2), jnp.concatenate([v, v_c], axis=2), tq, tk)
        x = _out_call(x, ysgu, o_f, o_b, gr, ymla, g1, sh2, sc2, g2, lw, fnw, tm, last)

        if not last:
            ymla_c = _attn_call(q_c, k_c, v_c, tmc, tmc)
            xc = _out_call(xc, ysgu_c, of_c, ob_c, gr_c, ymla_c, g1c, sh2c, sc2c, g2c, lw, fnw, tmc, False)
    return x
```

```python
import functools
import math

import numpy as np
import jax
import jax.numpy as jnp
from jax import lax
from jax.experimental import pallas as pl
from jax.experimental.pallas import tpu as pltpu

F32 = jnp.float32
BF16 = jnp.bfloat16

EPS = 1e-6
GRID_W = 64
ROPE_BASE = 10000.0
LANES = 128
SGU_HEADS = 4
SGU_HEAD_DIM = 64
SGU_WIDTH = 256
SGU_CHUNK = 128
GLA_HEADS = 4
GLA_DK = 32
GLA_DV = 64
GLA_KW = 128
GLA_VW = 256
GLA_RANK = 16
GLA_TAU = 16.0
GLA_BLOCK = 16
GLA_CHUNK = 64
GLA_SAFE_DECAY = 60.0
MLA_HEADS = 4
MLA_RANK = 256
MLA_NOPE = 128
MLA_ROPE = 64
MLA_V = 128
MLA_QK = 256
MLA_VROWS = 144
VMEM_LIMIT = 56 * 1024 * 1024

C_GQ, C_GK, C_GV, C_GR, C_SU, C_SV, C_CKV, C_DQ, C_ROPE_A, C_ROPE_B, C_END = (
    0, 128, 256, 512, 768, 1024, 1280, 1536, 1792, 1920, 2048)


def _cparams(sem):
    return pltpu.CompilerParams(dimension_semantics=sem, vmem_limit_bytes=VMEM_LIMIT)


def _dot(a, b):
    return jnp.dot(a, b, preferred_element_type=F32)


def _dot_nt(a, b):
    return lax.dot_general(a, b, (((1,), (1,)), ((), ())), preferred_element_type=F32)


def _dot_tn(a, b):
    return lax.dot_general(a, b, (((0,), (0,)), ((), ())), preferred_element_type=F32)


def _rms(x, w):
    return x * lax.rsqrt(jnp.mean(x * x, axis=-1, keepdims=True) + EPS) * w


def _gelu_tanh(x):
    return 0.5 * x * (1.0 + jnp.tanh(math.sqrt(2.0 / math.pi) * (x + 0.044715 * (x * x * x))))


def _silu(x):
    return x * jax.nn.sigmoid(x)


def _full(shape):
    return pl.BlockSpec(shape, lambda *_: (0,) * len(shape))


def _mod_kernel(c_ref, w_ref, b_ref, o_ref):
    s = _silu(c_ref[...]).astype(BF16)
    o_ref[...] = _dot(s, w_ref[...].astype(BF16)) + b_ref[...]


def _mod_call(cvecs, w_mod, b_mod):
    nl, d, d6 = w_mod.shape
    tn = 1536
    return pl.pallas_call(
        _mod_kernel,
        grid=(nl, d6 // tn),
        in_specs=[pl.BlockSpec((8, d), lambda l, j: (0, 0)),
                  pl.BlockSpec((None, d, tn), lambda l, j: (l, 0, j)),
                  pl.BlockSpec((None, 1, tn), lambda l, j: (l, 0, j))],
        out_specs=pl.BlockSpec((None, 8, tn), lambda l, j: (l, 0, j)),
        out_shape=jax.ShapeDtypeStruct((nl, 8, d6), F32),
        compiler_params=_cparams(("parallel", "parallel")),
        name="mod",
    )(cvecs, w_mod, b_mod.reshape(nl, 1, d6))


def _proj_kernel(x_ref, sh_ref, sc_ref, n1w_ref, win_ref, sgw_ref, sgb_ref, ws_ref, bs_ref, wg_ref, bg_ref,
                 qnw_ref, wuq_ref, kvnw_ref, wukv_ref, cos_ref, sin_ref,
                 ysgu_ref, qk_ref, gv_ref, gate_ref, gr_ref, q_ref, k_ref, v_ref, *, q_scale):
    tm = x_ref.shape[0]
    h = _rms(x_ref[...], n1w_ref[...]) * (1.0 + sc_ref[...]) + sh_ref[...]
    hb = h.astype(BF16)

    def proj(lo, hi):
        return _dot(hb, win_ref[:, lo:hi])

    qk_ref[:, 0:GLA_KW] = proj(C_GQ, C_GK) * (GLA_DK ** -0.5)
    qk_ref[:, GLA_KW:] = proj(C_GK, C_GV)
    gv_ref[...] = proj(C_GV, C_GR)
    gr_ref[...] = proj(C_GR, C_SU)

    u = _gelu_tanh(proj(C_SU, C_SV))
    g = _gelu_tanh(proj(C_SV, C_CKV))
    mu = jnp.mean(g, axis=-1, keepdims=True)
    gc = g - mu
    var = jnp.mean(gc * gc, axis=-1, keepdims=True)
    vln = (gc * lax.rsqrt(var + EPS) * sgw_ref[...] + sgb_ref[...]).astype(BF16)
    head_of_lane = lax.broadcasted_iota(jnp.int32, (SGU_CHUNK, SGU_WIDTH), 1) // SGU_HEAD_DIM
    ws = ws_ref[...]
    for c in range(tm // SGU_CHUNK):
        rows = slice(c * SGU_CHUNK, (c + 1) * SGU_CHUNK)
        r = _dot(ws, vln[rows, :])
        s = r[0:SGU_CHUNK]
        for hh in range(1, SGU_HEADS):
            s = jnp.where(head_of_lane == hh, r[hh * SGU_CHUNK:(hh + 1) * SGU_CHUNK], s)
        ysgu_ref[rows, :] = (u[rows, :] * (s + bs_ref[...])).astype(BF16)

    ga = proj(C_ROPE_A, C_ROPE_B)
    gb = proj(C_ROPE_B, C_END)
    z = _dot(ga.astype(BF16), wg_ref[...]) + bg_ref[...]
    log_sig = jnp.minimum(z, 0.0) - jnp.log1p(jnp.exp(-jnp.abs(z)))
    gate_ref[...] = log_sig * (1.0 / GLA_TAU)
    cos = cos_ref[...]
    sin = sin_ref[...]
    k_rope = (ga * cos + gb * sin).astype(BF16)

    ckv = _rms(proj(C_CKV, C_DQ), kvnw_ref[...]).astype(BF16)
    kv = _dot(ckv, wukv_ref[...])
    for hh in range(MLA_HEADS):
        k_ref[hh, :, 0:MLA_NOPE] = kv[:, hh * MLA_NOPE:(hh + 1) * MLA_NOPE].astype(BF16)
        k_ref[hh, :, MLA_NOPE:] = k_rope
        off = MLA_HEADS * MLA_NOPE + hh * MLA_V
        v_ref[hh] = kv[:, off:off + MLA_V].astype(BF16)

    cq = _rms(proj(C_DQ, C_ROPE_A), qnw_ref[...]).astype(BF16)
    qq = _dot(cq, wuq_ref[...])
    nq = MLA_HEADS * LANES
    for hh in range(MLA_HEADS):
        sl = slice(hh * LANES, (hh + 1) * LANES)
        q_ref[hh, :, 0:MLA_NOPE] = (qq[:, sl] * q_scale).astype(BF16)
        rot = qq[:, nq + hh * LANES:nq + (hh + 1) * LANES] * cos + qq[:, 2 * nq + hh * LANES:2 * nq + (hh + 1) * LANES] * sin
        q_ref[hh, :, MLA_NOPE:] = (rot * q_scale).astype(BF16)


def _proj_call(x, sh, sc, lw, cos, sin, tm):
    bsz, n, d = x.shape
    nh = MLA_HEADS
    q_scale = (MLA_NOPE + MLA_ROPE) ** -0.5 * math.log2(math.e)
    tok = lambda w: pl.BlockSpec((None, tm, w), lambda b, i: (b, i, 0))
    vec = pl.BlockSpec((None, 1, d), lambda b, i: (b, 0, 0))
    heads = lambda w: pl.BlockSpec((None, nh, tm, w), lambda b, i: (b, 0, i, 0))
    tab = pl.BlockSpec((tm, LANES), lambda b, i: (i, 0))
    in_specs = [tok(d), vec, vec, _full((1, d)), _full((d, C_END)),
                _full((1, SGU_WIDTH)), _full((1, SGU_WIDTH)), _full((SGU_HEADS * SGU_CHUNK, SGU_CHUNK)),
                _full((SGU_CHUNK, SGU_WIDTH)), _full((LANES, 2 * GLA_KW)), _full((1, 2 * GLA_KW)),
                _full((1, MLA_RANK)), _full((MLA_RANK, 3 * nh * LANES)),
                _full((1, MLA_RANK)), _full((MLA_RANK, nh * (MLA_NOPE + MLA_V))), tab, tab]
    out_specs = [tok(SGU_WIDTH), tok(2 * GLA_KW), tok(GLA_VW), tok(2 * GLA_KW), tok(GLA_VW),
                 heads(MLA_QK), heads(MLA_QK), heads(MLA_V)]
    out_shape = [jax.ShapeDtypeStruct((bsz, n, SGU_WIDTH), BF16),
                 jax.ShapeDtypeStruct((bsz, n, 2 * GLA_KW), F32),
                 jax.ShapeDtypeStruct((bsz, n, GLA_VW), F32),
                 jax.ShapeDtypeStruct((bsz, n, 2 * GLA_KW), F32),
                 jax.ShapeDtypeStruct((bsz, n, GLA_VW), F32),
                 jax.ShapeDtypeStruct((bsz, nh, n, MLA_QK), BF16),
                 jax.ShapeDtypeStruct((bsz, nh, n, MLA_QK), BF16),
                 jax.ShapeDtypeStruct((bsz, nh, n, MLA_V), BF16)]
    return pl.pallas_call(
        functools.partial(_proj_kernel, q_scale=q_scale),
        grid=(bsz, n // tm), in_specs=in_specs, out_specs=out_specs, out_shape=out_shape,
        compiler_params=_cparams(("parallel", "parallel")), name="proj",
    )(x, sh, sc, lw["n1w"], lw["w_in"], lw["sgw"], lw["sgb"], lw["ws"], lw["bs"], lw["wg"], lw["bg"],
      lw["qnw"], lw["wuq"], lw["kvnw"], lw["wukv"], cos, sin)


def _split3(x):
    hi = x.astype(BF16)
    r1 = x - hi.astype(F32)
    mid = r1.astype(BF16)
    lo = (r1 - mid.astype(F32)).astype(BF16)
    return hi, mid, lo


def _row_bcast(ref, r, n):
    return ref[pl.ds(r, n, stride=0), :]


def _gla_block(q_ref, k_ref, v0_ref, v1_ref, b_ref, st_ref, o_ref, r0, fwd, ind, mask_t):
    nb = GLA_BLOCK
    rows = pl.ds(r0, nb)
    bcast = lambda ref, r: _row_bcast(ref, r, nb)
    q = q_ref[rows, :]
    k = k_ref[rows, :]
    v = jnp.concatenate([v0_ref[rows, :], v1_ref[rows, :]], axis=1)
    b = b_ref[rows, :]
    b_end = bcast(b_ref, r0 + (nb - 1) if fwd else r0)
    st = st_ref[...]
    o = _dot_nt((q * jnp.exp(b)).astype(BF16), st.astype(BF16))
    kh = (k * jnp.exp(b_end - b)).astype(BF16)
    st_ref[...] = st * jnp.exp(b_end[0:1, :]) + mask_t * _dot_tn(v.astype(BF16), kh)

    row = lax.broadcasted_iota(jnp.int32, (nb, GLA_KW), 0)
    parts = []
    for j in range(nb):
        valid = (row >= j) if fwd else (row <= j)
        e = jnp.exp(jnp.where(valid, b - bcast(b_ref, r0 + j), -jnp.inf))
        parts.append((e * q * bcast(k_ref, r0 + j)).astype(BF16))
    r = _dot(jnp.concatenate(parts, axis=0), ind)
    for j in range(nb):
        vj = jnp.concatenate([bcast(v0_ref, r0 + j), bcast(v1_ref, r0 + j)], axis=1)
        o = o + r[j * nb:(j + 1) * nb] * vj
    o_ref[rows, :] = o


def _gla_chunk(q_ref, k_ref, v_ref, b_ref, st_ref, o_ref, r0, fwd, qmask, tri, mask_t):
    nc = GLA_CHUNK
    rows = pl.ds(r0, nc)
    q = q_ref[rows, :]
    k = k_ref[rows, :]
    vb = v_ref[rows, :].astype(BF16)
    b = b_ref[rows, :]
    e_end = jnp.exp(b_ref[pl.ds(r0 + (nc - 1) if fwd else r0, 1), :])
    qt = q * jnp.exp(b)
    kt = k * jnp.exp(-b)
    st = st_ref[...]
    o = _dot_nt(qt.astype(BF16), st.astype(BF16))
    st_ref[...] = st * e_end + mask_t * _dot_tn(vb, (kt * e_end).astype(BF16))
    q4 = (jnp.concatenate([qt] * GLA_HEADS, axis=0) * qmask).astype(BF16)
    a = jnp.where(tri, _dot_nt(q4, kt.astype(BF16)), 0.0)
    r = _dot(a.astype(BF16), vb)
    head_of_lane = lax.broadcasted_iota(jnp.int32, (nc, GLA_VW), 1) // GLA_DV
    intra = r[0:nc]
    for h in range(1, GLA_HEADS):
        intra = jnp.where(head_of_lane == h, r[h * nc:(h + 1) * nc], intra)
    o_ref[rows, :] = o + intra


def _cumsum_blocks(g_ref, tri_ref, b_sc):
    tri = tri_ref[...]
    span = tri.shape[0]
    for c in range(g_ref.shape[0] // span):
        rows = slice(c * span, (c + 1) * span)
        hi, mid, lo = _split3(g_ref[rows, :])
        b_sc[rows, :] = _dot(tri, hi) + _dot(tri, mid) + _dot(tri, lo)


def _gla_kernel(qf_ref, kf_ref, v0f_ref, v1f_ref, vf_ref, gf_ref, qb_ref, kb_ref, v0b_ref, v1b_ref, vb_ref, gb_ref,
                s0_ref, t16f_ref, t16b_ref, t64f_ref, t64b_ref, ind_ref, maskt_ref, qmask_ref,
                of_ref, ob_ref, sfin_ref, stf, stb, bf, bb):
    t = pl.program_id(1)
    tt = qf_ref.shape[0]

    @pl.when(t == 0)
    def _():
        stf[...] = s0_ref[0]
        stb[...] = s0_ref[1]

    _cumsum_blocks(gf_ref, t64f_ref, bf)
    _cumsum_blocks(gb_ref, t64b_ref, bb)
    lowest = jnp.minimum(jnp.min(bf[...]), jnp.min(bb[...]))
    mask_t = maskt_ref[...]

    @pl.when(lowest > -GLA_SAFE_DECAY)
    def _():
        nch = tt // GLA_CHUNK
        qmask = qmask_ref[...]
        row = lax.broadcasted_iota(jnp.int32, (GLA_HEADS * GLA_CHUNK, GLA_CHUNK), 0) % GLA_CHUNK
        col = lax.broadcasted_iota(jnp.int32, (GLA_HEADS * GLA_CHUNK, GLA_CHUNK), 1)
        lower, upper = col <= row, col >= row

        def step(i, carry):
            rf = pl.multiple_of(i * GLA_CHUNK, GLA_CHUNK)
            rb = pl.multiple_of((nch - 1 - i) * GLA_CHUNK, GLA_CHUNK)
            _gla_chunk(qf_ref, kf_ref, vf_ref, bf, stf, of_ref, rf, True, qmask, lower, mask_t)
            _gla_chunk(qb_ref, kb_ref, vb_ref, bb, stb, ob_ref, rb, False, qmask, upper, mask_t)
            return carry

        lax.fori_loop(0, nch, step, 0, unroll=2)

    @pl.when(jnp.logical_not(lowest > -GLA_SAFE_DECAY))
    def _():
        nblk = tt // GLA_BLOCK
        _cumsum_blocks(gf_ref, t16f_ref, bf)
        _cumsum_blocks(gb_ref, t16b_ref, bb)
        ind = ind_ref[...]

        def step(i, carry):
            rf = pl.multiple_of(i * GLA_BLOCK, GLA_BLOCK)
            rb = pl.multiple_of((nblk - 1 - i) * GLA_BLOCK, GLA_BLOCK)
            _gla_block(qf_ref, kf_ref, v0f_ref, v1f_ref, bf, stf, of_ref, rf, True, ind, mask_t)
            _gla_block(qb_ref, kb_ref, v0b_ref, v1b_ref, bb, stb, ob_ref, rb, False, ind, mask_t)
            return carry

        lax.fori_loop(0, nblk, step, 0)

    @pl.when(t == pl.num_programs(1) - 1)
    def _():
        sfin_ref[0] = stf[...]
        sfin_ref[1] = stb[...]


def _tri(span, blk):
    i = np.arange(span)
    same = (i[:, None] // blk) == (i[None, :] // blk)
    lower = (same & (i[None, :] <= i[:, None])).astype(np.float32)
    upper = (same & (i[None, :] >= i[:, None])).astype(np.float32)
    return jnp.asarray(lower, BF16), jnp.asarray(upper, BF16)


def _gla_call(qk, gv, gates, s0, tt):
    bsz, n, _ = qk.shape
    nt = n // tt
    span = min(tt, 256)
    t16f, t16b = _tri(span, GLA_BLOCK)
    t64f, t64b = _tri(span, GLA_CHUNK)
    kd = np.arange(GLA_KW) // GLA_DK
    ve = np.arange(GLA_VW) // GLA_DV
    ind = (kd[:, None] == ve[None, :]).astype(np.float32)
    qmask = (np.repeat(np.arange(GLA_HEADS), GLA_CHUNK)[:, None] == kd[None, :]).astype(np.float32)
    half = lambda rev, c: pl.BlockSpec((None, tt, LANES), lambda b, t: (b, nt - 1 - t if rev else t, c))
    full = lambda rev: pl.BlockSpec((None, tt, GLA_VW), lambda b, t: (b, nt - 1 - t if rev else t, 0))
    st = pl.BlockSpec((None, 2, GLA_VW, GLA_KW), lambda b, t: (b, 0, 0, 0))
    fwd_in = [half(False, 0), half(False, 1), half(False, 0), half(False, 1), full(False), half(False, 0)]
    bwd_in = [half(True, 0), half(True, 1), half(True, 0), half(True, 1), full(True), half(True, 1)]
    return pl.pallas_call(
        _gla_kernel,
        grid=(bsz, nt),
        in_specs=fwd_in + bwd_in + [st] + [_full((span, span))] * 4 + [
            _full((GLA_KW, GLA_VW)), _full((GLA_VW, GLA_KW)), _full((GLA_HEADS * GLA_CHUNK, GLA_KW))],
        out_specs=[full(False), full(True), st],
        out_shape=[jax.ShapeDtypeStruct((bsz, n, GLA_VW), F32), jax.ShapeDtypeStruct((bsz, n, GLA_VW), F32),
                   jax.ShapeDtypeStruct((bsz, 2, GLA_VW, GLA_KW), F32)],
        scratch_shapes=[pltpu.VMEM((GLA_VW, GLA_KW), F32), pltpu.VMEM((GLA_VW, GLA_KW), F32),
                        pltpu.VMEM((tt, GLA_KW), F32), pltpu.VMEM((tt, GLA_KW), F32)],
        compiler_params=_cparams(("parallel", "arbitrary")), name="gla",
    )(qk, qk, gv, gv, gv, gates, qk, qk, gv, gv, gv, gates, s0, t16f, t16b, t64f, t64b,
      jnp.asarray(ind, BF16), jnp.asarray(ind.T, F32), jnp.asarray(qmask, F32))


def _attn_kernel(q_ref, k_ref, vt_ref, o_ref, s_sc, m_sc, acc_sc, qt_sc, *, tk):
    nk = k_ref.shape[0] // tk
    qt_sc[...] = q_ref[...].T
    m_sc[...] = jnp.full(m_sc.shape, -jnp.inf, F32)
    acc_sc[...] = jnp.zeros(acc_sc.shape, F32)

    def scores(j, slot):
        r0 = pl.multiple_of(j * tk, tk)
        s_sc[slot] = _dot(k_ref[pl.ds(r0, tk), :], qt_sc[...])

    def update(j, slot):
        s = s_sc[slot]
        m_prev = m_sc[...]
        m_new = jnp.maximum(m_prev, jnp.max(s, axis=0, keepdims=True))
        alpha = jnp.exp2(m_prev - m_new)
        p = jnp.exp2(s - m_new).astype(BF16)
        acc_sc[...] = alpha * acc_sc[...] + _dot(vt_ref[j], p)
        m_sc[...] = m_new

    scores(0, 0)

    def pair(i, carry):
        j = 2 * i
        scores(j + 1, 1)
        update(j, 0)
        scores(j + 2, 0)
        update(j + 1, 1)
        return carry

    lax.fori_loop(0, (nk - 1) // 2, pair, 0)
    if nk % 2 == 0:
        scores(nk - 1, 1)
        update(nk - 2, 0)
        update(nk - 1, 1)
    else:
        update(nk - 1, 0)
    acc = acc_sc[...]
    o_t = acc[0:MLA_V] * (1.0 / acc[MLA_V:MLA_V + 1])
    o_ref[...] = o_t.T.astype(o_ref.dtype)


def _attn_call(q, k, v, tq, tk):
    bsz, nh, nq, dq = q.shape
    nk = k.shape[2]
    nkt = nk // tk
    pad = jnp.zeros(v.shape[:-1] + (MLA_VROWS - MLA_V - 1,), v.dtype)
    va = jnp.concatenate([v, jnp.ones(v.shape[:-1] + (1,), v.dtype), pad], axis=-1)
    vt = va.reshape(bsz, nh, nkt, tk, MLA_VROWS).transpose(0, 1, 2, 4, 3)
    return pl.pallas_call(
        functools.partial(_attn_kernel, tk=tk),
        grid=(bsz, nh, nq // tq),
        in_specs=[pl.BlockSpec((None, None, tq, dq), lambda b, h, i: (b, h, i, 0)),
                  pl.BlockSpec((None, None, nk, dq), lambda b, h, i: (b, h, 0, 0)),
                  pl.BlockSpec((None, None, nkt, MLA_VROWS, tk), lambda b, h, i: (b, h, 0, 0, 0))],
        out_specs=pl.BlockSpec((None, None, tq, MLA_V), lambda b, h, i: (b, h, i, 0)),
        out_shape=jax.ShapeDtypeStruct((bsz, nh, nq, MLA_V), BF16),
        scratch_shapes=[pltpu.VMEM((2, tk, tq), F32), pltpu.VMEM((1, tq), F32), pltpu.VMEM((MLA_VROWS, tq), F32),
                        pltpu.VMEM((dq, tq), BF16)],
        compiler_params=_cparams(("parallel", "parallel", "parallel")), name="attn",
    )(q, k, vt)


def _out_kernel(x_ref, ysgu_ref, of_ref, ob_ref, gr_ref, ymla_ref, g1_ref, sh2_ref, sc2_ref, g2_ref, glaw_ref,
                hsum_ref, wout_ref, n2w_ref, w1_ref, w2_ref, fnw_ref, o_ref, y_sc, *, final_norm):
    o = of_ref[...] + ob_ref[...]
    o2 = o * o
    hi = o2.astype(BF16)
    lo = (o2 - hi.astype(F32)).astype(BF16)
    ms = (_dot(hi, hsum_ref[...]) + _dot(lo, hsum_ref[...])) * (1.0 / GLA_DV)
    y_gla = o * lax.rsqrt(ms + EPS) * glaw_ref[...] * _silu(gr_ref[...])
    y_sc[:, 0:SGU_WIDTH] = ysgu_ref[...]
    y_sc[:, SGU_WIDTH:SGU_WIDTH + GLA_VW] = y_gla.astype(BF16)
    for hh in range(MLA_HEADS):
        off = SGU_WIDTH + GLA_VW + hh * MLA_V
        y_sc[:, off:off + MLA_V] = ymla_ref[hh]
    x1 = x_ref[...] + g1_ref[...] * _dot(y_sc[...], wout_ref[...])
    hn = (_rms(x1, n2w_ref[...]) * (1.0 + sc2_ref[...]) + sh2_ref[...]).astype(BF16)
    dff = w1_ref.shape[1]
    fc = 1024
    acc = jnp.zeros(x1.shape, F32)
    for c in range(dff // fc):
        a = jnp.maximum(_dot(hn, w1_ref[:, c * fc:(c + 1) * fc]), 0.0)
        acc = acc + _dot((a * a).astype(BF16), w2_ref[c * fc:(c + 1) * fc, :])
    x2 = x1 + g2_ref[...] * acc
    if final_norm:
        x2 = _rms(x2, fnw_ref[...])
    o_ref[...] = x2


def _out_call(x, ysgu, o_f, o_b, gr, ymla, g1, sh2, sc2, g2, lw, fnw, tm, final_norm):
    bsz, n, d = x.shape
    nh = MLA_HEADS
    mix = SGU_WIDTH + GLA_VW + nh * MLA_V
    dff = lw["w1"].shape[1]
    tok = lambda w: pl.BlockSpec((None, tm, w), lambda b, i: (b, i, 0))
    vec = pl.BlockSpec((None, 1, d), lambda b, i: (b, 0, 0))
    once = lambda shape: pl.BlockSpec(shape, lambda *_: (0,) * len(shape), pipeline_mode=pl.Buffered(1))
    in_specs = [tok(d), tok(SGU_WIDTH), tok(GLA_VW), tok(GLA_VW), tok(GLA_VW),
                pl.BlockSpec((None, nh, tm, MLA_V), lambda b, i: (b, 0, i, 0)),
                vec, vec, vec, vec, _full((1, GLA_VW)), _full((GLA_VW, GLA_VW)),
                once((mix, d)), _full((1, d)), once((d, dff)), once((dff, d)), _full((1, d))]
    return pl.pallas_call(
        functools.partial(_out_kernel, final_norm=final_norm),
        grid=(bsz, n // tm), in_specs=in_specs, out_specs=tok(d),
        out_shape=jax.ShapeDtypeStruct((bsz, n, d), F32),
        scratch_shapes=[pltpu.VMEM((tm, mix), BF16)],
        compiler_params=_cparams(("parallel", "parallel")), name="out_ffn",
    )(x, ysgu, o_f, o_b, gr, ymla, g1, sh2, sc2, g2, lw["glaw"], lw["hsum"], lw["w_out"], lw["n2w"],
      lw["w1"], lw["w2"], fnw)


def _rope_tables(n):
    half = MLA_ROPE // 2
    pos = np.arange(n)
    freq = ROPE_BASE ** (-np.arange(half // 2, dtype=np.float64) * 2.0 / half)
    ang_r = (pos // GRID_W)[:, None] * freq[None, :]
    ang_c = (pos % GRID_W)[:, None] * freq[None, :]
    cos = np.zeros((n, LANES), np.float32)
    sin = np.zeros((n, LANES), np.float32)
    cos[:, 0:64] = np.concatenate([np.cos(ang_r), np.cos(ang_r), np.cos(ang_c), np.cos(ang_c)], axis=1)
    sin[:, 0:64] = np.concatenate([-np.sin(ang_r), np.sin(ang_r), -np.sin(ang_c), np.sin(ang_c)], axis=1)
    return jnp.asarray(cos), jnp.asarray(sin)


def _identity_tables(n):
    cos = np.zeros((n, LANES), np.float32)
    cos[:, 0:64] = 1.0
    return jnp.asarray(cos), jnp.zeros((n, LANES), F32)


_ROT_PARTNER = np.concatenate([np.arange(16, 32), np.arange(0, 16), np.arange(48, 64), np.arange(32, 48)])


def _layer_weights(l, p):
    w_in = p["w_in"][l]
    d = w_in.shape[0]
    o_gk, o_gv, o_gf, o_gb, o_ckv, o_kr, o_su, o_sv, o_gq, o_gr, o_dq = (
        0, 128, 384, 400, 416, 672, 736, 992, 1248, 1376, 1632)
    col = lambda o, w: w_in[:, o:o + w]
    kr = col(o_kr, MLA_ROPE)
    z = lambda w: jnp.zeros((d, w), w_in.dtype)
    w_in_p = jnp.concatenate([
        col(o_gq, 128), col(o_gk, 128), col(o_gv, 256), col(o_gr, 256), col(o_su, 256), col(o_sv, 256),
        col(o_ckv, 256), col(o_dq, 256),
        kr, col(o_gf, GLA_RANK), col(o_gb, GLA_RANK), z(32),
        kr[:, _ROT_PARTNER], z(64)], axis=1).astype(BF16)

    wg = jnp.zeros((LANES, 2 * GLA_KW), F32)
    wg = wg.at[64:80, 0:GLA_KW].set(p["gla_wg_fwd"][l]).at[80:96, GLA_KW:].set(p["gla_wg_bwd"][l])
    bg = jnp.concatenate([p["gla_bg_fwd"][l], p["gla_bg_bwd"][l]])[None, :]

    w_uq = p["mla_w_uq"][l].reshape(MLA_RANK, MLA_HEADS, MLA_NOPE + MLA_ROPE)
    nope = w_uq[:, :, :MLA_NOPE].reshape(MLA_RANK, -1)
    rope = w_uq[:, :, MLA_NOPE:]
    pad = lambda t: jnp.pad(t, ((0, 0), (0, 0), (0, LANES - MLA_ROPE))).reshape(MLA_RANK, -1)
    wuq = jnp.concatenate([nope, pad(rope), pad(rope[:, :, _ROT_PARTNER])], axis=1).astype(BF16)

    w_ukv = p["mla_w_ukv"][l].reshape(MLA_RANK, MLA_HEADS, MLA_NOPE + MLA_V)
    wukv = jnp.concatenate([w_ukv[:, :, :MLA_NOPE].reshape(MLA_RANK, -1),
                            w_ukv[:, :, MLA_NOPE:].reshape(MLA_RANK, -1)], axis=1).astype(BF16)

    hv = np.arange(GLA_VW) // GLA_DV
    return dict(
        n1w=p["norm1_w"][l][None, :], w_in=w_in_p,
        sgw=p["sgu_norm_w"][l][None, :], sgb=p["sgu_norm_b"][l][None, :],
        ws=p["sgu_w"][l].reshape(SGU_HEADS * SGU_CHUNK, SGU_CHUNK).astype(BF16),
        bs=jnp.repeat(p["sgu_b"][l].T, SGU_HEAD_DIM, axis=1),
        wg=wg.astype(BF16), bg=bg,
        qnw=p["mla_q_norm_w"][l][None, :], wuq=wuq, kvnw=p["mla_kv_norm_w"][l][None, :], wukv=wukv,
        glaw=jnp.tile(p["gla_norm_w"][l], GLA_HEADS)[None, :],
        hsum=jnp.asarray((hv[:, None] == hv[None, :]).astype(np.float32), BF16),
        w_out=p["w_out"][l].astype(BF16), n2w=p["norm2_w"][l][None, :],
        w1=p["w_ff1"][l].astype(BF16), w2=p["w_ff2"][l].astype(BF16))


def _pick(n, pref):
    for t in pref:
        if n % t == 0:
            return t
    return n


def kernel(x, c, ctx, c_ctx, w_mod, b_mod, norm1_w, w_in, w_out, sgu_norm_w, sgu_norm_b, sgu_w, sgu_b, gla_wg_fwd, gla_bg_fwd, gla_wg_bwd, gla_bg_bwd, gla_norm_w, mla_q_norm_w, mla_w_uq, mla_kv_norm_w, mla_w_ukv, norm2_w, w_ff1, w_ff2, final_norm_w):
    p = dict(norm1_w=norm1_w, w_in=w_in, w_out=w_out, sgu_norm_w=sgu_norm_w, sgu_norm_b=sgu_norm_b, sgu_w=sgu_w,
             sgu_b=sgu_b, gla_wg_fwd=gla_wg_fwd, gla_bg_fwd=gla_bg_fwd, gla_wg_bwd=gla_wg_bwd, gla_bg_bwd=gla_bg_bwd,
             gla_norm_w=gla_norm_w, mla_q_norm_w=mla_q_norm_w, mla_w_uq=mla_w_uq, mla_kv_norm_w=mla_kv_norm_w,
             mla_w_ukv=mla_w_ukv, norm2_w=norm2_w, w_ff1=w_ff1, w_ff2=w_ff2)
    bsz, n, d = x.shape
    nc = ctx.shape[1]
    depth = w_mod.shape[0]
    fnw = final_norm_w[None, :]

    cvecs = jnp.zeros((8, d), F32).at[0:bsz].set(c).at[bsz].set(c_ctx)
    mod = _mod_call(cvecs, w_mod, b_mod)

    tm = _pick(n, (512, 256, 128))
    tmc = _pick(nc, (256, 128))
    tq = _pick(n, (1024, 512, 256, 128))
    tk = _pick(n + nc, (1408, 768, 512, 256, 128))
    cos, sin = _rope_tables(n)
    cos_c, sin_c = _identity_tables(nc)

    xc = ctx
    for l in range(depth):
        last = l == depth - 1
        lw = _layer_weights(l, p)
        m = mod[l, 0:bsz].reshape(bsz, 1, 6, d)
        sh1, sc1, g1, sh2, sc2, g2 = [m[:, :, i] for i in range(6)]
        mc = jnp.broadcast_to(mod[l, bsz].reshape(1, 1, 6, d), (bsz, 1, 6, d))
        sh1c, sc1c, g1c, sh2c, sc2c, g2c = [mc[:, :, i] for i in range(6)]

        ysgu_c, qk_c, gv_c, gate_c, gr_c, q_c, k_c, v_c = _proj_call(xc, sh1c, sc1c, lw, cos_c, sin_c, tmc)
        zero = jnp.zeros((bsz, 2, GLA_VW, GLA_KW), F32)
        of_c, ob_c, s_ctx = _gla_call(qk_c, gv_c, gate_c, zero, tmc)

        ysgu, qk, gv, gate, gr, q, k, v = _proj_call(x, sh1, sc1, lw, cos, sin, tm)
        o_f, o_b, _ = _gla_call(qk, gv, gate, s_ctx, tm)
        ymla = _attn_call(q, jnp.concatenate([k, k_c], axis=2), jnp.concatenate([v, v_c], axis=2), tq, tk)
        x = _out_call(x, ysgu, o_f, o_b, gr, ymla, g1, sh2, sc2, g2, lw, fnw, tm, last)

        if not last:
            ymla_c = _attn_call(q_c, k_c, v_c, tmc, tmc)
            xc = _out_call(xc, ysgu_c, of_c, ob_c, gr_c, ymla_c, g1c, sh2c, sc2c, g2c, lw, fnw, tmc, False)
    return x
```

```python
import functools
import math

import numpy as np
import jax
import jax.numpy as jnp
from jax import lax
from jax.experimental import pallas as pl
from jax.experimental.pallas import tpu as pltpu

F32 = jnp.float32
BF16 = jnp.bfloat16

EPS = 1e-6
GRID_W = 64
ROPE_BASE = 10000.0
LANES = 128
SGU_HEADS = 4
SGU_HEAD_DIM = 64
SGU_WIDTH = 256
SGU_CHUNK = 128
GLA_HEADS = 4
GLA_DK = 32
GLA_DV = 64
GLA_KW = 128
GLA_VW = 256
GLA_RANK = 16
GLA_TAU = 16.0
GLA_BLOCK = 16
GLA_CHUNK = 64
GLA_SAFE_DECAY = 60.0
MLA_HEADS = 4
MLA_RANK = 256
MLA_NOPE = 128
MLA_ROPE = 64
MLA_V = 128
MLA_QK = 256
MLA_VROWS = 144
VMEM_LIMIT = 56 * 1024 * 1024

C_GQ, C_GK, C_GV, C_GR, C_SU, C_SV, C_CKV, C_DQ, C_ROPE_A, C_ROPE_B, C_END = (
    0, 128, 256, 512, 768, 1024, 1280, 1536, 1792, 1920, 2048)


def _cparams(sem):
    return pltpu.CompilerParams(dimension_semantics=sem, vmem_limit_bytes=VMEM_LIMIT)


def _dot(a, b):
    return jnp.dot(a, b, preferred_element_type=F32)


def _dot_nt(a, b):
    return lax.dot_general(a, b, (((1,), (1,)), ((), ())), preferred_element_type=F32)


def _dot_tn(a, b):
    return lax.dot_general(a, b, (((0,), (0,)), ((), ())), preferred_element_type=F32)


def _rms(x, w):
    return x * lax.rsqrt(jnp.mean(x * x, axis=-1, keepdims=True) + EPS) * w


def _gelu_tanh(x):
    return 0.5 * x * (1.0 + jnp.tanh(math.sqrt(2.0 / math.pi) * (x + 0.044715 * (x * x * x))))


def _silu(x):
    return x * jax.nn.sigmoid(x)


def _full(shape):
    return pl.BlockSpec(shape, lambda *_: (0,) * len(shape))


def _mod_kernel(c_ref, w_ref, b_ref, o_ref):
    s = _silu(c_ref[...]).astype(BF16)
    o_ref[...] = _dot(s, w_ref[...].astype(BF16)) + b_ref[...]


def _mod_call(cvecs, w_mod, b_mod):
    nl, d, d6 = w_mod.shape
    tn = 1536
    return pl.pallas_call(
        _mod_kernel,
        grid=(nl, d6 // tn),
        in_specs=[pl.BlockSpec((8, d), lambda l, j: (0, 0)),
                  pl.BlockSpec((None, d, tn), lambda l, j: (l, 0, j)),
                  pl.BlockSpec((None, 1, tn), lambda l, j: (l, 0, j))],
        out_specs=pl.BlockSpec((None, 8, tn), lambda l, j: (l, 0, j)),
        out_shape=jax.ShapeDtypeStruct((nl, 8, d6), F32),
        compiler_params=_cparams(("parallel", "parallel")),
        name="mod",
    )(cvecs, w_mod, b_mod.reshape(nl, 1, d6))


def _proj_kernel(x_ref, sh_ref, sc_ref, n1w_ref, win_ref, sgw_ref, sgb_ref, ws_ref, bs_ref, wg_ref, bg_ref,
                 qnw_ref, wuq_ref, kvnw_ref, wukv_ref, cos_ref, sin_ref, *rest, q_scale):
    ysgu_ref, qk_ref, gv_ref, gate_ref, gr_ref, q_ref, k_ref, vt_ref = rest[-8:]
    tm = x_ref.shape[0]
    h = _rms(x_ref[...], n1w_ref[...]) * (1.0 + sc_ref[...]) + sh_ref[...]
    hb = h.astype(BF16)

    def proj(lo, hi):
        return _dot(hb, win_ref[:, lo:hi])

    qk_ref[:, 0:GLA_KW] = proj(C_GQ, C_GK) * (GLA_DK ** -0.5)
    qk_ref[:, GLA_KW:] = proj(C_GK, C_GV)
    gv_ref[...] = proj(C_GV, C_GR)
    gr_ref[...] = proj(C_GR, C_SU)

    u = _gelu_tanh(proj(C_SU, C_SV))
    g = _gelu_tanh(proj(C_SV, C_CKV))
    mu = jnp.mean(g, axis=-1, keepdims=True)
    gc = g - mu
    var = jnp.mean(gc * gc, axis=-1, keepdims=True)
    vln = (gc * lax.rsqrt(var + EPS) * sgw_ref[...] + sgb_ref[...]).astype(BF16)
    head_of_lane = lax.broadcasted_iota(jnp.int32, (SGU_CHUNK, SGU_WIDTH), 1) // SGU_HEAD_DIM
    ws = ws_ref[...]
    for c in range(tm // SGU_CHUNK):
        rows = slice(c * SGU_CHUNK, (c + 1) * SGU_CHUNK)
        r = _dot(ws, vln[rows, :])
        s = r[0:SGU_CHUNK]
        for hh in range(1, SGU_HEADS):
            s = jnp.where(head_of_lane == hh, r[hh * SGU_CHUNK:(hh + 1) * SGU_CHUNK], s)
        ysgu_ref[rows, :] = (u[rows, :] * (s + bs_ref[...])).astype(BF16)

    ga = proj(C_ROPE_A, C_ROPE_B)
    gb = proj(C_ROPE_B, C_END)
    z = _dot(ga.astype(BF16), wg_ref[...]) + bg_ref[...]
    log_sig = jnp.minimum(z, 0.0) - jnp.log1p(jnp.exp(-jnp.abs(z)))
    gate_ref[...] = log_sig * (1.0 / GLA_TAU)
    cos = cos_ref[...]
    sin = sin_ref[...]
    k_rope = (ga * cos + gb * sin).astype(BF16)

    ckv = _rms(proj(C_CKV, C_DQ), kvnw_ref[...]).astype(BF16)
    kv = _dot(ckv, wukv_ref[...])
    ones_row = (lax.broadcasted_iota(jnp.int32, (MLA_VROWS - MLA_V, tm), 0) == 0).astype(BF16)
    for hh in range(MLA_HEADS):
        k_ref[hh, :, 0:MLA_NOPE] = kv[:, hh * MLA_NOPE:(hh + 1) * MLA_NOPE].astype(BF16)
        k_ref[hh, :, MLA_NOPE:] = k_rope
        off = MLA_HEADS * MLA_NOPE + hh * MLA_V
        vt_ref[hh, 0:MLA_V, :] = kv[:, off:off + MLA_V].T.astype(BF16)
        vt_ref[hh, MLA_V:, :] = ones_row

    cq = _rms(proj(C_DQ, C_ROPE_A), qnw_ref[...]).astype(BF16)
    qq = _dot(cq, wuq_ref[...])
    nq = MLA_HEADS * LANES
    for hh in range(MLA_HEADS):
        sl = slice(hh * LANES, (hh + 1) * LANES)
        q_ref[hh, :, 0:MLA_NOPE] = (qq[:, sl] * q_scale).astype(BF16)
        rot = qq[:, nq + hh * LANES:nq + (hh + 1) * LANES] * cos + qq[:, 2 * nq + hh * LANES:2 * nq + (hh + 1) * LANES] * sin
        q_ref[hh, :, MLA_NOPE:] = (rot * q_scale).astype(BF16)


def _proj_call(x, sh, sc, lw, cos, sin, tm, nk_total, key_off, kv_prev=None):
    bsz, n, d = x.shape
    nh = MLA_HEADS
    kb = key_off // tm
    q_scale = (MLA_NOPE + MLA_ROPE) ** -0.5 * math.log2(math.e)
    tok = lambda w: pl.BlockSpec((None, tm, w), lambda b, i: (b, i, 0))
    vec = pl.BlockSpec((None, 1, d), lambda b, i: (b, 0, 0))
    heads = lambda w: pl.BlockSpec((None, nh, tm, w), lambda b, i: (b, 0, i, 0))
    tab = pl.BlockSpec((tm, LANES), lambda b, i: (i, 0))
    k_spec = pl.BlockSpec((None, nh, tm, MLA_QK), lambda b, i: (b, 0, kb + i, 0))
    vt_spec = pl.BlockSpec((None, nh, MLA_VROWS, tm), lambda b, i: (b, 0, 0, kb + i))
    in_specs = [tok(d), vec, vec, _full((1, d)), _full((d, C_END)),
                _full((1, SGU_WIDTH)), _full((1, SGU_WIDTH)), _full((SGU_HEADS * SGU_CHUNK, SGU_CHUNK)),
                _full((SGU_CHUNK, SGU_WIDTH)), _full((LANES, 2 * GLA_KW)), _full((1, 2 * GLA_KW)),
                _full((1, MLA_RANK)), _full((MLA_RANK, 3 * nh * LANES)),
                _full((1, MLA_RANK)), _full((MLA_RANK, nh * (MLA_NOPE + MLA_V))), tab, tab]
    out_specs = [tok(SGU_WIDTH), tok(2 * GLA_KW), tok(GLA_VW), tok(2 * GLA_KW), tok(GLA_VW),
                 heads(MLA_QK), k_spec, vt_spec]
    out_shape = [jax.ShapeDtypeStruct((bsz, n, SGU_WIDTH), BF16),
                 jax.ShapeDtypeStruct((bsz, n, 2 * GLA_KW), F32),
                 jax.ShapeDtypeStruct((bsz, n, GLA_VW), F32),
                 jax.ShapeDtypeStruct((bsz, n, 2 * GLA_KW), F32),
                 jax.ShapeDtypeStruct((bsz, n, GLA_VW), F32),
                 jax.ShapeDtypeStruct((bsz, nh, n, MLA_QK), BF16),
                 jax.ShapeDtypeStruct((bsz, nh, nk_total, MLA_QK), BF16),
                 jax.ShapeDtypeStruct((bsz, nh, MLA_VROWS, nk_total), BF16)]
    args = [x, sh, sc, lw["n1w"], lw["w_in"], lw["sgw"], lw["sgb"], lw["ws"], lw["bs"], lw["wg"], lw["bg"],
            lw["qnw"], lw["wuq"], lw["kvnw"], lw["wukv"], cos, sin]
    aliases = {}
    if kv_prev is not None:
        aliases = {len(args): 6, len(args) + 1: 7}
        in_specs = in_specs + [pl.BlockSpec(memory_space=pl.ANY)] * 2
        args = args + list(kv_prev)
    return pl.pallas_call(
        functools.partial(_proj_kernel, q_scale=q_scale),
        grid=(bsz, n // tm), in_specs=in_specs, out_specs=out_specs, out_shape=out_shape,
        input_output_aliases=aliases,
        compiler_params=_cparams(("parallel", "parallel")), name="proj",
    )(*args)


def _split3(x):
    hi = x.astype(BF16)
    r1 = x - hi.astype(F32)
    mid = r1.astype(BF16)
    lo = (r1 - mid.astype(F32)).astype(BF16)
    return hi, mid, lo


def _row_bcast(ref, r, n):
    return ref[pl.ds(r, n, stride=0), :]


def _gla_block(q_ref, k_ref, v0_ref, v1_ref, b_ref, st_ref, o_ref, r0, fwd, ind, mask_t):
    nb = GLA_BLOCK
    rows = pl.ds(r0, nb)
    bcast = lambda ref, r: _row_bcast(ref, r, nb)
    q = q_ref[rows, :]
    k = k_ref[rows, :]
    v = jnp.concatenate([v0_ref[rows, :], v1_ref[rows, :]], axis=1)
    b = b_ref[rows, :]
    b_end = bcast(b_ref, r0 + (nb - 1) if fwd else r0)
    st = st_ref[...]
    o = _dot_nt((q * jnp.exp(b)).astype(BF16), st.astype(BF16))
    kh = (k * jnp.exp(b_end - b)).astype(BF16)
    st_ref[...] = st * jnp.exp(b_end[0:1, :]) + mask_t * _dot_tn(v.astype(BF16), kh)

    row = lax.broadcasted_iota(jnp.int32, (nb, GLA_KW), 0)
    parts = []
    for j in range(nb):
        valid = (row >= j) if fwd else (row <= j)
        e = jnp.exp(jnp.where(valid, b - bcast(b_ref, r0 + j), -jnp.inf))
        parts.append((e * q * bcast(k_ref, r0 + j)).astype(BF16))
    r = _dot(jnp.concatenate(parts, axis=0), ind)
    for j in range(nb):
        vj = jnp.concatenate([bcast(v0_ref, r0 + j), bcast(v1_ref, r0 + j)], axis=1)
        o = o + r[j * nb:(j + 1) * nb] * vj
    o_ref[rows, :] = o


def _gla_chunk(q_ref, k_ref, v_ref, b_ref, st_ref, o_ref, r0, fwd, qmask, tri, mask_t):
    nc = GLA_CHUNK
    rows = pl.ds(r0, nc)
    q = q_ref[rows, :]
    k = k_ref[rows, :]
    vb = v_ref[rows, :].astype(BF16)
    b = b_ref[rows, :]
    e_end = jnp.exp(b_ref[pl.ds(r0 + (nc - 1) if fwd else r0, 1), :])
    qt = q * jnp.exp(b)
    kt = k * jnp.exp(-b)
    st = st_ref[...]
    o = _dot_nt(qt.astype(BF16), st.astype(BF16))
    st_ref[...] = st * e_end + mask_t * _dot_tn(vb, (kt * e_end).astype(BF16))
    q4 = (jnp.concatenate([qt] * GLA_HEADS, axis=0) * qmask).astype(BF16)
    a = jnp.where(tri, _dot_nt(q4, kt.astype(BF16)), 0.0)
    r = _dot(a.astype(BF16), vb)
    head_of_lane = lax.broadcasted_iota(jnp.int32, (nc, GLA_VW), 1) // GLA_DV
    intra = r[0:nc]
    for h in range(1, GLA_HEADS):
        intra = jnp.where(head_of_lane == h, r[h * nc:(h + 1) * nc], intra)
    o_ref[rows, :] = o + intra


def _cumsum_blocks(g_ref, tri_ref, b_sc):
    tri = tri_ref[...]
    span = tri.shape[0]
    for c in range(g_ref.shape[0] // span):
        rows = slice(c * span, (c + 1) * span)
        hi, mid, lo = _split3(g_ref[rows, :])
        b_sc[rows, :] = _dot(tri, hi) + _dot(tri, mid) + _dot(tri, lo)


def _gla_kernel(qf_ref, kf_ref, v0f_ref, v1f_ref, vf_ref, gf_ref, qb_ref, kb_ref, v0b_ref, v1b_ref, vb_ref, gb_ref,
                s0_ref, t16f_ref, t16b_ref, t64f_ref, t64b_ref, ind_ref, maskt_ref, qmask_ref,
                of_ref, ob_ref, sfin_ref, stf, stb, bf, bb):
    t = pl.program_id(1)
    tt = qf_ref.shape[0]

    @pl.when(t == 0)
    def _():
        stf[...] = s0_ref[0]
        stb[...] = s0_ref[1]

    _cumsum_blocks(gf_ref, t64f_ref, bf)
    _cumsum_blocks(gb_ref, t64b_ref, bb)
    lowest = jnp.minimum(jnp.min(bf[...]), jnp.min(bb[...]))
    mask_t = maskt_ref[...]

    @pl.when(lowest > -GLA_SAFE_DECAY)
    def _():
        nch = tt // GLA_CHUNK
        qmask = qmask_ref[...]
        row = lax.broadcasted_iota(jnp.int32, (GLA_HEADS * GLA_CHUNK, GLA_CHUNK), 0) % GLA_CHUNK
        col = lax.broadcasted_iota(jnp.int32, (GLA_HEADS * GLA_CHUNK, GLA_CHUNK), 1)
        lower, upper = col <= row, col >= row

        def step(i, carry):
            rf = pl.multiple_of(i * GLA_CHUNK, GLA_CHUNK)
            rb = pl.multiple_of((nch - 1 - i) * GLA_CHUNK, GLA_CHUNK)
            _gla_chunk(qf_ref, kf_ref, vf_ref, bf, stf, of_ref, rf, True, qmask, lower, mask_t)
            _gla_chunk(qb_ref, kb_ref, vb_ref, bb, stb, ob_ref, rb, False, qmask, upper, mask_t)
            return carry

        lax.fori_loop(0, nch, step, 0, unroll=2)

    @pl.when(jnp.logical_not(lowest > -GLA_SAFE_DECAY))
    def _():
        nblk = tt // GLA_BLOCK
        _cumsum_blocks(gf_ref, t16f_ref, bf)
        _cumsum_blocks(gb_ref, t16b_ref, bb)
        ind = ind_ref[...]

        def step(i, carry):
            rf = pl.multiple_of(i * GLA_BLOCK, GLA_BLOCK)
            rb = pl.multiple_of((nblk - 1 - i) * GLA_BLOCK, GLA_BLOCK)
            _gla_block(qf_ref, kf_ref, v0f_ref, v1f_ref, bf, stf, of_ref, rf, True, ind, mask_t)
            _gla_block(qb_ref, kb_ref, v0b_ref, v1b_ref, bb, stb, ob_ref, rb, False, ind, mask_t)
            return carry

        lax.fori_loop(0, nblk, step, 0)

    @pl.when(t == pl.num_programs(1) - 1)
    def _():
        sfin_ref[0] = stf[...]
        sfin_ref[1] = stb[...]


def _tri(span, blk):
    i = np.arange(span)
    same = (i[:, None] // blk) == (i[None, :] // blk)
    lower = (same & (i[None, :] <= i[:, None])).astype(np.float32)
    upper = (same & (i[None, :] >= i[:, None])).astype(np.float32)
    return jnp.asarray(lower, BF16), jnp.asarray(upper, BF16)


def _gla_call(qk, gv, gates, s0, tt):
    bsz, n, _ = qk.shape
    nt = n // tt
    span = min(tt, 256)
    t16f, t16b = _tri(span, GLA_BLOCK)
    t64f, t64b = _tri(span, GLA_CHUNK)
    kd = np.arange(GLA_KW) // GLA_DK
    ve = np.arange(GLA_VW) // GLA_DV
    ind = (kd[:, None] == ve[None, :]).astype(np.float32)
    qmask = (np.repeat(np.arange(GLA_HEADS), GLA_CHUNK)[:, None] == kd[None, :]).astype(np.float32)
    half = lambda rev, c: pl.BlockSpec((None, tt, LANES), lambda b, t: (b, nt - 1 - t if rev else t, c))
    full = lambda rev: pl.BlockSpec((None, tt, GLA_VW), lambda b, t: (b, nt - 1 - t if rev else t, 0))
    st = pl.BlockSpec((None, 2, GLA_VW, GLA_KW), lambda b, t: (b, 0, 0, 0))
    fwd_in = [half(False, 0), half(False, 1), half(False, 0), half(False, 1), full(False), half(False, 0)]
    bwd_in = [half(True, 0), half(True, 1), half(True, 0), half(True, 1), full(True), half(True, 1)]
    return pl.pallas_call(
        _gla_kernel,
        grid=(bsz, nt),
        in_specs=fwd_in + bwd_in + [st] + [_full((span, span))] * 4 + [
            _full((GLA_KW, GLA_VW)), _full((GLA_VW, GLA_KW)), _full((GLA_HEADS * GLA_CHUNK, GLA_KW))],
        out_specs=[full(False), full(True), st],
        out_shape=[jax.ShapeDtypeStruct((bsz, n, GLA_VW), F32), jax.ShapeDtypeStruct((bsz, n, GLA_VW), F32),
                   jax.ShapeDtypeStruct((bsz, 2, GLA_VW, GLA_KW), F32)],
        scratch_shapes=[pltpu.VMEM((GLA_VW, GLA_KW), F32), pltpu.VMEM((GLA_VW, GLA_KW), F32),
                        pltpu.VMEM((tt, GLA_KW), F32), pltpu.VMEM((tt, GLA_KW), F32)],
        compiler_params=_cparams(("parallel", "arbitrary")), name="gla",
    )(qk, qk, gv, gv, gv, gates, qk, qk, gv, gv, gv, gates, s0, t16f, t16b, t64f, t64b,
      jnp.asarray(ind, BF16), jnp.asarray(ind.T, F32), jnp.asarray(qmask, F32))


def _attn_kernel(q_ref, k_ref, vt_ref, o_ref, s_sc, m_sc, acc_sc, qt_sc, *, tk):
    nk = k_ref.shape[0] // tk
    qt_sc[...] = q_ref[...].T
    m_sc[...] = jnp.full(m_sc.shape, -jnp.inf, F32)
    acc_sc[...] = jnp.zeros(acc_sc.shape, F32)

    def scores(j, slot):
        r0 = pl.multiple_of(j * tk, tk)
        s_sc[slot] = _dot(k_ref[pl.ds(r0, tk), :], qt_sc[...])

    def update(j, slot):
        s = s_sc[slot]
        m_prev = m_sc[...]
        m_new = jnp.maximum(m_prev, jnp.max(s, axis=0, keepdims=True))
        alpha = jnp.exp2(m_prev - m_new)
        p = jnp.exp2(s - m_new).astype(BF16)
        r0 = pl.multiple_of(j * tk, tk)
        acc_sc[...] = alpha * acc_sc[...] + _dot(vt_ref[:, pl.ds(r0, tk)], p)
        m_sc[...] = m_new

    scores(0, 0)

    def pair(i, carry):
        j = 2 * i
        scores(j + 1, 1)
        update(j, 0)
        scores(j + 2, 0)
        update(j + 1, 1)
        return carry

    lax.fori_loop(0, (nk - 1) // 2, pair, 0)
    if nk % 2 == 0:
        scores(nk - 1, 1)
        update(nk - 2, 0)
        update(nk - 1, 1)
    else:
        update(nk - 1, 0)
    acc = acc_sc[...]
    o_t = acc[0:MLA_V] * (1.0 / acc[MLA_V:MLA_V + 1])
    o_ref[...] = o_t.T.astype(o_ref.dtype)


def _attn_call(q, k, vt, tq, tk, key_off, nk):
    bsz, nh, nq, dq = q.shape
    kb = key_off // nk
    return pl.pallas_call(
        functools.partial(_attn_kernel, tk=tk),
        grid=(bsz, nh, nq // tq),
        in_specs=[pl.BlockSpec((None, None, tq, dq), lambda b, h, i: (b, h, i, 0)),
                  pl.BlockSpec((None, None, nk, dq), lambda b, h, i: (b, h, kb, 0)),
                  pl.BlockSpec((None, None, MLA_VROWS, nk), lambda b, h, i: (b, h, 0, kb))],
        out_specs=pl.BlockSpec((None, None, tq, MLA_V), lambda b, h, i: (b, h, i, 0)),
        out_shape=jax.ShapeDtypeStruct((bsz, nh, nq, MLA_V), BF16),
        scratch_shapes=[pltpu.VMEM((2, tk, tq), F32), pltpu.VMEM((1, tq), F32), pltpu.VMEM((MLA_VROWS, tq), F32),
                        pltpu.VMEM((dq, tq), BF16)],
        compiler_params=_cparams(("parallel", "parallel", "parallel")), name="attn",
    )(q, k, vt)


def _out_kernel(x_ref, ysgu_ref, of_ref, ob_ref, gr_ref, ymla_ref, g1_ref, sh2_ref, sc2_ref, g2_ref, glaw_ref,
                hsum_ref, wout_ref, n2w_ref, w1_ref, w2_ref, fnw_ref, o_ref, y_sc, *, final_norm):
    o = of_ref[...] + ob_ref[...]
    o2 = o * o
    hi = o2.astype(BF16)
    lo = (o2 - hi.astype(F32)).astype(BF16)
    ms = (_dot(hi, hsum_ref[...]) + _dot(lo, hsum_ref[...])) * (1.0 / GLA_DV)
    y_gla = o * lax.rsqrt(ms + EPS) * glaw_ref[...] * _silu(gr_ref[...])
    y_sc[:, 0:SGU_WIDTH] = ysgu_ref[...]
    y_sc[:, SGU_WIDTH:SGU_WIDTH + GLA_VW] = y_gla.astype(BF16)
    for hh in range(MLA_HEADS):
        off = SGU_WIDTH + GLA_VW + hh * MLA_V
        y_sc[:, off:off + MLA_V] = ymla_ref[hh]
    x1 = x_ref[...] + g1_ref[...] * _dot(y_sc[...], wout_ref[...])
    hn = (_rms(x1, n2w_ref[...]) * (1.0 + sc2_ref[...]) + sh2_ref[...]).astype(BF16)
    dff = w1_ref.shape[1]
    fc = 1024
    acc = jnp.zeros(x1.shape, F32)
    for c in range(dff // fc):
        a = jnp.maximum(_dot(hn, w1_ref[:, c * fc:(c + 1) * fc]), 0.0)
        acc = acc + _dot((a * a).astype(BF16), w2_ref[c * fc:(c + 1) * fc, :])
    x2 = x1 + g2_ref[...] * acc
    if final_norm:
        x2 = _rms(x2, fnw_ref[...])
    o_ref[...] = x2


def _out_call(x, ysgu, o_f, o_b, gr, ymla, g1, sh2, sc2, g2, lw, fnw, tm, final_norm):
    bsz, n, d = x.shape
    nh = MLA_HEADS
    mix = SGU_WIDTH + GLA_VW + nh * MLA_V
    dff = lw["w1"].shape[1]
    tok = lambda w: pl.BlockSpec((None, tm, w), lambda b, i: (b, i, 0))
    vec = pl.BlockSpec((None, 1, d), lambda b, i: (b, 0, 0))
    once = lambda shape: pl.BlockSpec(shape, lambda *_: (0,) * len(shape), pipeline_mode=pl.Buffered(1))
    in_specs = [tok(d), tok(SGU_WIDTH), tok(GLA_VW), tok(GLA_VW), tok(GLA_VW),
                pl.BlockSpec((None, nh, tm, MLA_V), lambda b, i: (b, 0, i, 0)),
                vec, vec, vec, vec, _full((1, GLA_VW)), _full((GLA_VW, GLA_VW)),
                once((mix, d)), _full((1, d)), once((d, dff)), once((dff, d)), _full((1, d))]
    return pl.pallas_call(
        functools.partial(_out_kernel, final_norm=final_norm),
        grid=(bsz, n // tm), in_specs=in_specs, out_specs=tok(d),
        out_shape=jax.ShapeDtypeStruct((bsz, n, d), F32),
        scratch_shapes=[pltpu.VMEM((tm, mix), BF16)],
        compiler_params=_cparams(("parallel", "parallel")), name="out_ffn",
    )(x, ysgu, o_f, o_b, gr, ymla, g1, sh2, sc2, g2, lw["glaw"], lw["hsum"], lw["w_out"], lw["n2w"],
      lw["w1"], lw["w2"], fnw)


def _rope_tables(n):
    half = MLA_ROPE // 2
    pos = np.arange(n)
    freq = ROPE_BASE ** (-np.arange(half // 2, dtype=np.float64) * 2.0 / half)
    ang_r = (pos // GRID_W)[:, None] * freq[None, :]
    ang_c = (pos % GRID_W)[:, None] * freq[None, :]
    cos = np.zeros((n, LANES), np.float32)
    sin = np.zeros((n, LANES), np.float32)
    cos[:, 0:64] = np.concatenate([np.cos(ang_r), np.cos(ang_r), np.cos(ang_c), np.cos(ang_c)], axis=1)
    sin[:, 0:64] = np.concatenate([-np.sin(ang_r), np.sin(ang_r), -np.sin(ang_c), np.sin(ang_c)], axis=1)
    return jnp.asarray(cos), jnp.asarray(sin)


def _identity_tables(n):
    cos = np.zeros((n, LANES), np.float32)
    cos[:, 0:64] = 1.0
    return jnp.asarray(cos), jnp.zeros((n, LANES), F32)


_ROT_PARTNER = np.concatenate([np.arange(16, 32), np.arange(0, 16), np.arange(48, 64), np.arange(32, 48)])


def _layer_weights(l, p):
    w_in = p["w_in"][l]
    d = w_in.shape[0]
    o_gk, o_gv, o_gf, o_gb, o_ckv, o_kr, o_su, o_sv, o_gq, o_gr, o_dq = (
        0, 128, 384, 400, 416, 672, 736, 992, 1248, 1376, 1632)
    col = lambda o, w: w_in[:, o:o + w]
    kr = col(o_kr, MLA_ROPE)
    z = lambda w: jnp.zeros((d, w), w_in.dtype)
    w_in_p = jnp.concatenate([
        col(o_gq, 128), col(o_gk, 128), col(o_gv, 256), col(o_gr, 256), col(o_su, 256), col(o_sv, 256),
        col(o_ckv, 256), col(o_dq, 256),
        kr, col(o_gf, GLA_RANK), col(o_gb, GLA_RANK), z(32),
        kr[:, _ROT_PARTNER], z(64)], axis=1).astype(BF16)

    wg = jnp.zeros((LANES, 2 * GLA_KW), F32)
    wg = wg.at[64:80, 0:GLA_KW].set(p["gla_wg_fwd"][l]).at[80:96, GLA_KW:].set(p["gla_wg_bwd"][l])
    bg = jnp.concatenate([p["gla_bg_fwd"][l], p["gla_bg_bwd"][l]])[None, :]

    w_uq = p["mla_w_uq"][l].reshape(MLA_RANK, MLA_HEADS, MLA_NOPE + MLA_ROPE)
    nope = w_uq[:, :, :MLA_NOPE].reshape(MLA_RANK, -1)
    rope = w_uq[:, :, MLA_NOPE:]
    pad = lambda t: jnp.pad(t, ((0, 0), (0, 0), (0, LANES - MLA_ROPE))).reshape(MLA_RANK, -1)
    wuq = jnp.concatenate([nope, pad(rope), pad(rope[:, :, _ROT_PARTNER])], axis=1).astype(BF16)

    w_ukv = p["mla_w_ukv"][l].reshape(MLA_RANK, MLA_HEADS, MLA_NOPE + MLA_V)
    wukv = jnp.concatenate([w_ukv[:, :, :MLA_NOPE].reshape(MLA_RANK, -1),
                            w_ukv[:, :, MLA_NOPE:].reshape(MLA_RANK, -1)], axis=1).astype(BF16)

    hv = np.arange(GLA_VW) // GLA_DV
    return dict(
        n1w=p["norm1_w"][l][None, :], w_in=w_in_p,
        sgw=p["sgu_norm_w"][l][None, :], sgb=p["sgu_norm_b"][l][None, :],
        ws=p["sgu_w"][l].reshape(SGU_HEADS * SGU_CHUNK, SGU_CHUNK).astype(BF16),
        bs=jnp.repeat(p["sgu_b"][l].T, SGU_HEAD_DIM, axis=1),
        wg=wg.astype(BF16), bg=bg,
        qnw=p["mla_q_norm_w"][l][None, :], wuq=wuq, kvnw=p["mla_kv_norm_w"][l][None, :], wukv=wukv,
        glaw=jnp.tile(p["gla_norm_w"][l], GLA_HEADS)[None, :],
        hsum=jnp.asarray((hv[:, None] == hv[None, :]).astype(np.float32), BF16),
        w_out=p["w_out"][l].astype(BF16), n2w=p["norm2_w"][l][None, :],
        w1=p["w_ff1"][l].astype(BF16), w2=p["w_ff2"][l].astype(BF16))


def _pick(n, pref):
    for t in pref:
        if n % t == 0:
            return t
    return n


def kernel(x, c, ctx, c_ctx, w_mod, b_mod, norm1_w, w_in, w_out, sgu_norm_w, sgu_norm_b, sgu_w, sgu_b, gla_wg_fwd, gla_bg_fwd, gla_wg_bwd, gla_bg_bwd, gla_norm_w, mla_q_norm_w, mla_w_uq, mla_kv_norm_w, mla_w_ukv, norm2_w, w_ff1, w_ff2, final_norm_w):
    p = dict(norm1_w=norm1_w, w_in=w_in, w_out=w_out, sgu_norm_w=sgu_norm_w, sgu_norm_b=sgu_norm_b, sgu_w=sgu_w,
             sgu_b=sgu_b, gla_wg_fwd=gla_wg_fwd, gla_bg_fwd=gla_bg_fwd, gla_wg_bwd=gla_wg_bwd, gla_bg_bwd=gla_bg_bwd,
             gla_norm_w=gla_norm_w, mla_q_norm_w=mla_q_norm_w, mla_w_uq=mla_w_uq, mla_kv_norm_w=mla_kv_norm_w,
             mla_w_ukv=mla_w_ukv, norm2_w=norm2_w, w_ff1=w_ff1, w_ff2=w_ff2)
    bsz, n, d = x.shape
    nc = ctx.shape[1]
    depth = w_mod.shape[0]
    fnw = final_norm_w[None, :]

    cvecs = jnp.zeros((8, d), F32).at[0:bsz].set(c).at[bsz].set(c_ctx)
    mod = _mod_call(cvecs, w_mod, b_mod)

    tm = _pick(n, (512, 256, 128))
    tmp = _pick(n, (1024, 512, 256, 128))
    tmc = _pick(nc, (256, 128))
    tq = _pick(n, (1024, 512, 256, 128))
    tk = _pick(n + nc, (1408, 768, 512, 256, 128))
    cos, sin = _rope_tables(n)
    cos_c, sin_c = _identity_tables(nc)

    xc = ctx
    for l in range(depth):
        last = l == depth - 1
        lw = _layer_weights(l, p)
        m = mod[l, 0:bsz].reshape(bsz, 1, 6, d)
        sh1, sc1, g1, sh2, sc2, g2 = [m[:, :, i] for i in range(6)]
        mc = jnp.broadcast_to(mod[l, bsz].reshape(1, 1, 6, d), (bsz, 1, 6, d))
        sh1c, sc1c, g1c, sh2c, sc2c, g2c = [mc[:, :, i] for i in range(6)]

        ysgu, qk, gv, gate, gr, q, k, vt = _proj_call(x, sh1, sc1, lw, cos, sin, tmp, n + nc, 0)
        ysgu_c, qk_c, gv_c, gate_c, gr_c, q_c, k, vt = _proj_call(xc, sh1c, sc1c, lw, cos_c, sin_c, tmc, n + nc, n,
                                                                   kv_prev=(k, vt))
        zero = jnp.zeros((bsz, 2, GLA_VW, GLA_KW), F32)
        of_c, ob_c, s_ctx = _gla_call(qk_c, gv_c, gate_c, zero, tmc)
        o_f, o_b, _ = _gla_call(qk, gv, gate, s_ctx, tm)
        ymla = _attn_call(q, k, vt, tq, tk, 0, n + nc)
        x = _out_call(x, ysgu, o_f, o_b, gr, ymla, g1, sh2, sc2, g2, lw, fnw, tm, last)

        if not last:
            ymla_c = _attn_call(q_c, k, vt, tmc, tmc, n, nc)
            xc = _out_call(xc, ysgu_c, of_c, ob_c, gr_c, ymla_c, g1c, sh2c, sc2c, g2c, lw, fnw, tmc, False)
    return x
```

```python
import functools
import math

import numpy as np
import jax
import jax.numpy as jnp
from jax import lax
from jax.experimental import pallas as pl
from jax.experimental.pallas import tpu as pltpu

F32 = jnp.float32
BF16 = jnp.bfloat16

EPS = 1e-6
GRID_W = 64
ROPE_BASE = 10000.0
LANES = 128
SGU_HEADS = 4
SGU_HEAD_DIM = 64
SGU_WIDTH = 256
SGU_CHUNK = 128
GLA_HEADS = 4
GLA_DK = 32
GLA_DV = 64
GLA_KW = 128
GLA_VW = 256
GLA_RANK = 16
GLA_TAU = 16.0
GLA_BLOCK = 16
GLA_CHUNK = 64
GLA_SAFE_DECAY = 60.0
MLA_HEADS = 4
MLA_RANK = 256
MLA_NOPE = 128
MLA_ROPE = 64
MLA_V = 128
MLA_QK = 256
MLA_VROWS = 144
VMEM_LIMIT = 56 * 1024 * 1024

C_GQ, C_GK, C_GV, C_GR, C_SU, C_SV, C_CKV, C_DQ, C_ROPE_A, C_ROPE_B, C_END = (
    0, 128, 256, 512, 768, 1024, 1280, 1536, 1792, 1920, 2048)


def _cparams(sem):
    return pltpu.CompilerParams(dimension_semantics=sem, vmem_limit_bytes=VMEM_LIMIT)


def _dot(a, b):
    return jnp.dot(a, b, preferred_element_type=F32)


def _dot_nt(a, b):
    return lax.dot_general(a, b, (((1,), (1,)), ((), ())), preferred_element_type=F32)


def _dot_tn(a, b):
    return lax.dot_general(a, b, (((0,), (0,)), ((), ())), preferred_element_type=F32)


def _rms(x, w):
    return x * lax.rsqrt(jnp.mean(x * x, axis=-1, keepdims=True) + EPS) * w


def _gelu_tanh(x):
    return 0.5 * x * (1.0 + jnp.tanh(math.sqrt(2.0 / math.pi) * (x + 0.044715 * (x * x * x))))


def _silu(x):
    return x * jax.nn.sigmoid(x)


def _full(shape):
    return pl.BlockSpec(shape, lambda *_: (0,) * len(shape))


def _mod_kernel(c_ref, w_ref, b_ref, o_ref):
    s = _silu(c_ref[...]).astype(BF16)
    o_ref[...] = _dot(s, w_ref[...].astype(BF16)) + b_ref[...]


def _mod_call(cvecs, w_mod, b_mod):
    nl, d, d6 = w_mod.shape
    tn = 1536
    return pl.pallas_call(
        _mod_kernel,
        grid=(nl, d6 // tn),
        in_specs=[pl.BlockSpec((8, d), lambda l, j: (0, 0)),
                  pl.BlockSpec((None, d, tn), lambda l, j: (l, 0, j)),
                  pl.BlockSpec((None, 1, tn), lambda l, j: (l, 0, j))],
        out_specs=pl.BlockSpec((None, 8, tn), lambda l, j: (l, 0, j)),
        out_shape=jax.ShapeDtypeStruct((nl, 8, d6), F32),
        compiler_params=_cparams(("parallel", "parallel")),
        name="mod",
    )(cvecs, w_mod, b_mod.reshape(nl, 1, d6))


def _proj_kernel(x_ref, sh_ref, sc_ref, n1w_ref, win_ref, sgw_ref, sgb_ref, ws_ref, bs_ref, wg_ref, bg_ref,
                 qnw_ref, wuq_ref, kvnw_ref, wukv_ref, cos_ref, sin_ref, *rest, q_scale):
    ysgu_ref, qk_ref, gv_ref, gate_ref, gr_ref, q_ref, k_ref, vt_ref = rest[-8:]
    tm = x_ref.shape[0]
    h = _rms(x_ref[...], n1w_ref[...]) * (1.0 + sc_ref[...]) + sh_ref[...]
    hb = h.astype(BF16)

    def proj(lo, hi):
        return _dot(hb, win_ref[:, lo:hi])

    qk_ref[:, 0:GLA_KW] = proj(C_GQ, C_GK) * (GLA_DK ** -0.5)
    qk_ref[:, GLA_KW:] = proj(C_GK, C_GV)
    gv_ref[...] = proj(C_GV, C_GR)
    gr_ref[...] = proj(C_GR, C_SU)

    u = _gelu_tanh(proj(C_SU, C_SV))
    g = _gelu_tanh(proj(C_SV, C_CKV))
    mu = jnp.mean(g, axis=-1, keepdims=True)
    gc = g - mu
    var = jnp.mean(gc * gc, axis=-1, keepdims=True)
    vln = (gc * lax.rsqrt(var + EPS) * sgw_ref[...] + sgb_ref[...]).astype(BF16)
    head_of_lane = lax.broadcasted_iota(jnp.int32, (SGU_CHUNK, SGU_WIDTH), 1) // SGU_HEAD_DIM
    ws = ws_ref[...]
    for c in range(tm // SGU_CHUNK):
        rows = slice(c * SGU_CHUNK, (c + 1) * SGU_CHUNK)
        r = _dot(ws, vln[rows, :])
        s = r[0:SGU_CHUNK]
        for hh in range(1, SGU_HEADS):
            s = jnp.where(head_of_lane == hh, r[hh * SGU_CHUNK:(hh + 1) * SGU_CHUNK], s)
        ysgu_ref[rows, :] = (u[rows, :] * (s + bs_ref[...])).astype(BF16)

    ga = proj(C_ROPE_A, C_ROPE_B)
    gb = proj(C_ROPE_B, C_END)
    z = _dot(ga.astype(BF16), wg_ref[...]) + bg_ref[...]
    log_sig = jnp.minimum(z, 0.0) - jnp.log1p(jnp.exp(-jnp.abs(z)))
    gate_ref[...] = log_sig * (1.0 / GLA_TAU)
    cos = cos_ref[...]
    sin = sin_ref[...]
    k_rope = (ga * cos + gb * sin).astype(BF16)

    ckv = _rms(proj(C_CKV, C_DQ), kvnw_ref[...]).astype(BF16)
    kv = _dot(ckv, wukv_ref[...])
    ones_row = (lax.broadcasted_iota(jnp.int32, (MLA_VROWS - MLA_V, tm), 0) == 0).astype(BF16)
    for hh in range(MLA_HEADS):
        k_ref[hh, :, 0:MLA_NOPE] = kv[:, hh * MLA_NOPE:(hh + 1) * MLA_NOPE].astype(BF16)
        k_ref[hh, :, MLA_NOPE:] = k_rope
        off = MLA_HEADS * MLA_NOPE + hh * MLA_V
        vt_ref[hh, 0:MLA_V, :] = kv[:, off:off + MLA_V].T.astype(BF16)
        vt_ref[hh, MLA_V:, :] = ones_row

    cq = _rms(proj(C_DQ, C_ROPE_A), qnw_ref[...]).astype(BF16)
    qq = _dot(cq, wuq_ref[...])
    nq = MLA_HEADS * LANES
    for hh in range(MLA_HEADS):
        sl = slice(hh * LANES, (hh + 1) * LANES)
        q_ref[hh, :, 0:MLA_NOPE] = (qq[:, sl] * q_scale).astype(BF16)
        rot = qq[:, nq + hh * LANES:nq + (hh + 1) * LANES] * cos + qq[:, 2 * nq + hh * LANES:2 * nq + (hh + 1) * LANES] * sin
        q_ref[hh, :, MLA_NOPE:] = (rot * q_scale).astype(BF16)


def _proj_call(x, sh, sc, lw, cos, sin, tm, nk_total, key_off, kv_prev=None):
    bsz, n, d = x.shape
    nh = MLA_HEADS
    kb = key_off // tm
    q_scale = (MLA_NOPE + MLA_ROPE) ** -0.5 * math.log2(math.e)
    tok = lambda w: pl.BlockSpec((None, tm, w), lambda b, i: (b, i, 0))
    vec = pl.BlockSpec((None, 1, d), lambda b, i: (b, 0, 0))
    heads = lambda w: pl.BlockSpec((None, nh, tm, w), lambda b, i: (b, 0, i, 0))
    tab = pl.BlockSpec((tm, LANES), lambda b, i: (i, 0))
    k_spec = pl.BlockSpec((None, nh, tm, MLA_QK), lambda b, i: (b, 0, kb + i, 0))
    vt_spec = pl.BlockSpec((None, nh, MLA_VROWS, tm), lambda b, i: (b, 0, 0, kb + i))
    in_specs = [tok(d), vec, vec, _full((1, d)), _full((d, C_END)),
                _full((1, SGU_WIDTH)), _full((1, SGU_WIDTH)), _full((SGU_HEADS * SGU_CHUNK, SGU_CHUNK)),
                _full((SGU_CHUNK, SGU_WIDTH)), _full((LANES, 2 * GLA_KW)), _full((1, 2 * GLA_KW)),
                _full((1, MLA_RANK)), _full((MLA_RANK, 3 * nh * LANES)),
                _full((1, MLA_RANK)), _full((MLA_RANK, nh * (MLA_NOPE + MLA_V))), tab, tab]
    out_specs = [tok(SGU_WIDTH), tok(2 * GLA_KW), tok(GLA_VW), tok(2 * GLA_KW), tok(GLA_VW),
                 heads(MLA_QK), k_spec, vt_spec]
    out_shape = [jax.ShapeDtypeStruct((bsz, n, SGU_WIDTH), BF16),
                 jax.ShapeDtypeStruct((bsz, n, 2 * GLA_KW), F32),
                 jax.ShapeDtypeStruct((bsz, n, GLA_VW), F32),
                 jax.ShapeDtypeStruct((bsz, n, 2 * GLA_KW), F32),
                 jax.ShapeDtypeStruct((bsz, n, GLA_VW), F32),
                 jax.ShapeDtypeStruct((bsz, nh, n, MLA_QK), BF16),
                 jax.ShapeDtypeStruct((bsz, nh, nk_total, MLA_QK), BF16),
                 jax.ShapeDtypeStruct((bsz, nh, MLA_VROWS, nk_total), BF16)]
    args = [x, sh, sc, lw["n1w"], lw["w_in"], lw["sgw"], lw["sgb"], lw["ws"], lw["bs"], lw["wg"], lw["bg"],
            lw["qnw"], lw["wuq"], lw["kvnw"], lw["wukv"], cos, sin]
    aliases = {}
    if kv_prev is not None:
        aliases = {len(args): 6, len(args) + 1: 7}
        in_specs = in_specs + [pl.BlockSpec(memory_space=pl.ANY)] * 2
        args = args + list(kv_prev)
    return pl.pallas_call(
        functools.partial(_proj_kernel, q_scale=q_scale),
        grid=(bsz, n // tm), in_specs=in_specs, out_specs=out_specs, out_shape=out_shape,
        input_output_aliases=aliases,
        compiler_params=_cparams(("parallel", "parallel")), name="proj",
    )(*args)


def _split3(x):
    hi = x.astype(BF16)
    r1 = x - hi.astype(F32)
    mid = r1.astype(BF16)
    lo = (r1 - mid.astype(F32)).astype(BF16)
    return hi, mid, lo


def _row_bcast(ref, r, n):
    return ref[pl.ds(r, n, stride=0), :]


def _gla_block(q_ref, k_ref, v0_ref, v1_ref, b_ref, st_ref, o_ref, r0, fwd, ind, mask_t):
    nb = GLA_BLOCK
    rows = pl.ds(r0, nb)
    bcast = lambda ref, r: _row_bcast(ref, r, nb)
    q = q_ref[rows, :]
    k = k_ref[rows, :]
    v = jnp.concatenate([v0_ref[rows, :], v1_ref[rows, :]], axis=1)
    b = b_ref[rows, :]
    b_end = bcast(b_ref, r0 + (nb - 1) if fwd else r0)
    st = st_ref[...]
    o = _dot_nt((q * jnp.exp(b)).astype(BF16), st.astype(BF16))
    kh = (k * jnp.exp(b_end - b)).astype(BF16)
    st_ref[...] = st * jnp.exp(b_end[0:1, :]) + mask_t * _dot_tn(v.astype(BF16), kh)

    row = lax.broadcasted_iota(jnp.int32, (nb, GLA_KW), 0)
    parts = []
    for j in range(nb):
        valid = (row >= j) if fwd else (row <= j)
        e = jnp.exp(jnp.where(valid, b - bcast(b_ref, r0 + j), -jnp.inf))
        parts.append((e * q * bcast(k_ref, r0 + j)).astype(BF16))
    r = _dot(jnp.concatenate(parts, axis=0), ind)
    for j in range(nb):
        vj = jnp.concatenate([bcast(v0_ref, r0 + j), bcast(v1_ref, r0 + j)], axis=1)
        o = o + r[j * nb:(j + 1) * nb] * vj
    o_ref[rows, :] = o


def _chunk_prep(q_ref, k_ref, b_ref, a_sc, qt_sc, kh_sc, d, slot, r0, fwd, qmask, tri):
    nc = GLA_CHUNK
    rows = pl.ds(r0, nc)
    b = b_ref[rows, :]
    e_end = jnp.exp(b_ref[pl.ds(r0 + (nc - 1) if fwd else r0, 1), :])
    qt = q_ref[rows, :] * jnp.exp(b)
    kt = k_ref[rows, :] * jnp.exp(-b)
    q4 = (jnp.concatenate([qt] * GLA_HEADS, axis=0) * qmask).astype(BF16)
    a_sc[d, slot] = jnp.where(tri, _dot_nt(q4, kt.astype(BF16)), 0.0).astype(BF16)
    qt_sc[d, slot] = qt.astype(BF16)
    kh_sc[d, slot] = (kt * e_end).astype(BF16)


def _chunk_apply(v_ref, b_ref, st_ref, o_ref, a_sc, qt_sc, kh_sc, d, slot, r0, fwd, mask_t):
    nc = GLA_CHUNK
    rows = pl.ds(r0, nc)
    vb = v_ref[rows, :].astype(BF16)
    e_end = jnp.exp(b_ref[pl.ds(r0 + (nc - 1) if fwd else r0, 1), :])
    st = st_ref[...]
    o = _dot_nt(qt_sc[d, slot], st.astype(BF16))
    st_ref[...] = st * e_end + mask_t * _dot_tn(vb, kh_sc[d, slot])
    r = _dot(a_sc[d, slot], vb)
    head_of_lane = lax.broadcasted_iota(jnp.int32, (nc, GLA_VW), 1) // GLA_DV
    intra = r[0:nc]
    for h in range(1, GLA_HEADS):
        intra = jnp.where(head_of_lane == h, r[h * nc:(h + 1) * nc], intra)
    o_ref[rows, :] = o + intra


def _cumsum_blocks(g_ref, tri_ref, b_sc):
    tri = tri_ref[...]
    span = tri.shape[0]
    for c in range(g_ref.shape[0] // span):
        rows = slice(c * span, (c + 1) * span)
        hi, mid, lo = _split3(g_ref[rows, :])
        b_sc[rows, :] = _dot(tri, hi) + _dot(tri, mid) + _dot(tri, lo)


def _gla_kernel(qf_ref, kf_ref, v0f_ref, v1f_ref, vf_ref, gf_ref, qb_ref, kb_ref, v0b_ref, v1b_ref, vb_ref, gb_ref,
                s0_ref, t16f_ref, t16b_ref, t64f_ref, t64b_ref, ind_ref, maskt_ref, qmask_ref,
                of_ref, ob_ref, sfin_ref, stf, stb, bf, bb, a_sc, qt_sc, kh_sc):
    t = pl.program_id(1)
    tt = qf_ref.shape[0]

    @pl.when(t == 0)
    def _():
        stf[...] = s0_ref[0]
        stb[...] = s0_ref[1]

    _cumsum_blocks(gf_ref, t64f_ref, bf)
    _cumsum_blocks(gb_ref, t64b_ref, bb)
    lowest = jnp.minimum(jnp.min(bf[...]), jnp.min(bb[...]))
    mask_t = maskt_ref[...]

    @pl.when(lowest > -GLA_SAFE_DECAY)
    def _():
        nch = tt // GLA_CHUNK
        qmask = qmask_ref[...]
        row = lax.broadcasted_iota(jnp.int32, (GLA_HEADS * GLA_CHUNK, GLA_CHUNK), 0) % GLA_CHUNK
        col = lax.broadcasted_iota(jnp.int32, (GLA_HEADS * GLA_CHUNK, GLA_CHUNK), 1)
        lower, upper = col <= row, col >= row

        off = lambda c: pl.multiple_of(c * GLA_CHUNK, GLA_CHUNK)

        def prep(c_f, c_b, slot):
            _chunk_prep(qf_ref, kf_ref, bf, a_sc, qt_sc, kh_sc, 0, slot, off(c_f), True, qmask, lower)
            _chunk_prep(qb_ref, kb_ref, bb, a_sc, qt_sc, kh_sc, 1, slot, off(c_b), False, qmask, upper)

        prep(0, nch - 1, 0)

        def step(i, carry):
            slot = i % 2
            _chunk_apply(vf_ref, bf, stf, of_ref, a_sc, qt_sc, kh_sc, 0, slot, off(i), True, mask_t)
            _chunk_apply(vb_ref, bb, stb, ob_ref, a_sc, qt_sc, kh_sc, 1, slot, off(nch - 1 - i), False, mask_t)
            nxt = jnp.minimum(i + 1, nch - 1)
            prep(nxt, nch - 1 - nxt, 1 - slot)
            return carry

        lax.fori_loop(0, nch, step, 0, unroll=2)

    @pl.when(jnp.logical_not(lowest > -GLA_SAFE_DECAY))
    def _():
        nblk = tt // GLA_BLOCK
        _cumsum_blocks(gf_ref, t16f_ref, bf)
        _cumsum_blocks(gb_ref, t16b_ref, bb)
        ind = ind_ref[...]

        def step(i, carry):
            rf = pl.multiple_of(i * GLA_BLOCK, GLA_BLOCK)
            rb = pl.multiple_of((nblk - 1 - i) * GLA_BLOCK, GLA_BLOCK)
            _gla_block(qf_ref, kf_ref, v0f_ref, v1f_ref, bf, stf, of_ref, rf, True, ind, mask_t)
            _gla_block(qb_ref, kb_ref, v0b_ref, v1b_ref, bb, stb, ob_ref, rb, False, ind, mask_t)
            return carry

        lax.fori_loop(0, nblk, step, 0)

    @pl.when(t == pl.num_programs(1) - 1)
    def _():
        sfin_ref[0] = stf[...]
        sfin_ref[1] = stb[...]


def _tri(span, blk):
    i = np.arange(span)
    same = (i[:, None] // blk) == (i[None, :] // blk)
    lower = (same & (i[None, :] <= i[:, None])).astype(np.float32)
    upper = (same & (i[None, :] >= i[:, None])).astype(np.float32)
    return jnp.asarray(lower, BF16), jnp.asarray(upper, BF16)


def _gla_call(qk, gv, gates, s0, tt):
    bsz, n, _ = qk.shape
    nt = n // tt
    span = min(tt, 256)
    t16f, t16b = _tri(span, GLA_BLOCK)
    t64f, t64b = _tri(span, GLA_CHUNK)
    kd = np.arange(GLA_KW) // GLA_DK
    ve = np.arange(GLA_VW) // GLA_DV
    ind = (kd[:, None] == ve[None, :]).astype(np.float32)
    qmask = (np.repeat(np.arange(GLA_HEADS), GLA_CHUNK)[:, None] == kd[None, :]).astype(np.float32)
    half = lambda rev, c: pl.BlockSpec((None, tt, LANES), lambda b, t: (b, nt - 1 - t if rev else t, c))
    full = lambda rev: pl.BlockSpec((None, tt, GLA_VW), lambda b, t: (b, nt - 1 - t if rev else t, 0))
    st = pl.BlockSpec((None, 2, GLA_VW, GLA_KW), lambda b, t: (b, 0, 0, 0))
    fwd_in = [half(False, 0), half(False, 1), half(False, 0), half(False, 1), full(False), half(False, 0)]
    bwd_in = [half(True, 0), half(True, 1), half(True, 0), half(True, 1), full(True), half(True, 1)]
    return pl.pallas_call(
        _gla_kernel,
        grid=(bsz, nt),
        in_specs=fwd_in + bwd_in + [st] + [_full((span, span))] * 4 + [
            _full((GLA_KW, GLA_VW)), _full((GLA_VW, GLA_KW)), _full((GLA_HEADS * GLA_CHUNK, GLA_KW))],
        out_specs=[full(False), full(True), st],
        out_shape=[jax.ShapeDtypeStruct((bsz, n, GLA_VW), F32), jax.ShapeDtypeStruct((bsz, n, GLA_VW), F32),
                   jax.ShapeDtypeStruct((bsz, 2, GLA_VW, GLA_KW), F32)],
        scratch_shapes=[pltpu.VMEM((GLA_VW, GLA_KW), F32), pltpu.VMEM((GLA_VW, GLA_KW), F32),
                        pltpu.VMEM((tt, GLA_KW), F32), pltpu.VMEM((tt, GLA_KW), F32),
                        pltpu.VMEM((2, 2, GLA_HEADS * GLA_CHUNK, GLA_CHUNK), BF16),
                        pltpu.VMEM((2, 2, GLA_CHUNK, GLA_KW), BF16), pltpu.VMEM((2, 2, GLA_CHUNK, GLA_KW), BF16)],
        compiler_params=_cparams(("parallel", "arbitrary")), name="gla",
    )(qk, qk, gv, gv, gv, gates, qk, qk, gv, gv, gv, gates, s0, t16f, t16b, t64f, t64b,
      jnp.asarray(ind, BF16), jnp.asarray(ind.T, F32), jnp.asarray(qmask, F32))


def _attn_kernel(q_ref, qn_ref, k_ref, vt_ref, o_ref, s_sc, m_sc, acc_sc, qt_sc, *, tk):
    nk = k_ref.shape[0] // tk
    i = pl.program_id(2)
    carry_scores = nk % 2 == 0
    par = i % 2 if carry_scores else 0

    def scores(j, slot, qslot):
        r0 = pl.multiple_of(j * tk, tk)
        s_sc[slot] = _dot(k_ref[pl.ds(r0, tk), :], qt_sc[qslot])

    def first_scores():
        qt_sc[0] = q_ref[...].T
        scores(0, 0, 0)

    if carry_scores:
        pl.when(i == 0)(first_scores)
    else:
        first_scores()
    m_sc[...] = jnp.full(m_sc.shape, -jnp.inf, F32)
    acc_sc[...] = jnp.zeros(acc_sc.shape, F32)

    def update(j, slot):
        s = s_sc[slot]
        m_prev = m_sc[...]
        m_new = jnp.maximum(m_prev, jnp.max(s, axis=0, keepdims=True))
        alpha = jnp.exp2(m_prev - m_new)
        p = jnp.exp2(s - m_new).astype(BF16)
        r0 = pl.multiple_of(j * tk, tk)
        acc_sc[...] = alpha * acc_sc[...] + _dot(vt_ref[:, pl.ds(r0, tk)], p)
        m_sc[...] = m_new

    def pair(t, carry):
        j = 2 * t
        scores(j + 1, 1, par)
        update(j, 0)
        scores(j + 2, 0, par)
        update(j + 1, 1)
        return carry

    lax.fori_loop(0, (nk - 1) // 2, pair, 0)
    if carry_scores:
        scores(nk - 1, 1, par)
        update(nk - 2, 0)
        qt_sc[1 - par] = qn_ref[...].T
        scores(0, 0, 1 - par)
        update(nk - 1, 1)
    else:
        update(nk - 1, 0)
    acc = acc_sc[...]
    o_t = acc[0:MLA_V] * (1.0 / acc[MLA_V:MLA_V + 1])
    o_ref[...] = o_t.T.astype(o_ref.dtype)


def _attn_call(q, k, vt, tq, tk, key_off, nk):
    bsz, nh, nq, dq = q.shape
    kb = key_off // nk
    nqt = nq // tq
    return pl.pallas_call(
        functools.partial(_attn_kernel, tk=tk),
        grid=(bsz, nh, nqt),
        in_specs=[pl.BlockSpec((None, None, tq, dq), lambda b, h, i: (b, h, i, 0)),
                  pl.BlockSpec((None, None, tq, dq), lambda b, h, i: (b, h, jnp.minimum(i + 1, nqt - 1), 0)),
                  pl.BlockSpec((None, None, nk, dq), lambda b, h, i: (b, h, kb, 0)),
                  pl.BlockSpec((None, None, MLA_VROWS, nk), lambda b, h, i: (b, h, 0, kb))],
        out_specs=pl.BlockSpec((None, None, tq, MLA_V), lambda b, h, i: (b, h, i, 0)),
        out_shape=jax.ShapeDtypeStruct((bsz, nh, nq, MLA_V), BF16),
        scratch_shapes=[pltpu.VMEM((2, tk, tq), F32), pltpu.VMEM((1, tq), F32), pltpu.VMEM((MLA_VROWS, tq), F32),
                        pltpu.VMEM((2, dq, tq), BF16)],
        compiler_params=_cparams(("parallel", "parallel", "arbitrary")), name="attn",
    )(q, q, k, vt)


def _out_kernel(x_ref, ysgu_ref, of_ref, ob_ref, gr_ref, ymla_ref, g1_ref, sh2_ref, sc2_ref, g2_ref, glaw_ref,
                hsum_ref, wout_ref, n2w_ref, w1_ref, w2_ref, fnw_ref, o_ref, y_sc, *, final_norm):
    o = of_ref[...] + ob_ref[...]
    o2 = o * o
    hi = o2.astype(BF16)
    lo = (o2 - hi.astype(F32)).astype(BF16)
    ms = (_dot(hi, hsum_ref[...]) + _dot(lo, hsum_ref[...])) * (1.0 / GLA_DV)
    y_gla = o * lax.rsqrt(ms + EPS) * glaw_ref[...] * _silu(gr_ref[...])
    y_sc[:, 0:SGU_WIDTH] = ysgu_ref[...]
    y_sc[:, SGU_WIDTH:SGU_WIDTH + GLA_VW] = y_gla.astype(BF16)
    for hh in range(MLA_HEADS):
        off = SGU_WIDTH + GLA_VW + hh * MLA_V
        y_sc[:, off:off + MLA_V] = ymla_ref[hh]
    x1 = x_ref[...] + g1_ref[...] * _dot(y_sc[...], wout_ref[...])
    hn = (_rms(x1, n2w_ref[...]) * (1.0 + sc2_ref[...]) + sh2_ref[...]).astype(BF16)
    dff = w1_ref.shape[1]
    fc = 1024
    acc = jnp.zeros(x1.shape, F32)
    for c in range(dff // fc):
        a = jnp.maximum(_dot(hn, w1_ref[:, c * fc:(c + 1) * fc]), 0.0)
        acc = acc + _dot((a * a).astype(BF16), w2_ref[c * fc:(c + 1) * fc, :])
    x2 = x1 + g2_ref[...] * acc
    if final_norm:
        x2 = _rms(x2, fnw_ref[...])
    o_ref[...] = x2


def _out_call(x, ysgu, o_f, o_b, gr, ymla, g1, sh2, sc2, g2, lw, fnw, tm, final_norm):
    bsz, n, d = x.shape
    nh = MLA_HEADS
    mix = SGU_WIDTH + GLA_VW + nh * MLA_V
    dff = lw["w1"].shape[1]
    tok = lambda w: pl.BlockSpec((None, tm, w), lambda b, i: (b, i, 0))
    vec = pl.BlockSpec((None, 1, d), lambda b, i: (b, 0, 0))
    once = lambda shape: pl.BlockSpec(shape, lambda *_: (0,) * len(shape), pipeline_mode=pl.Buffered(1))
    in_specs = [tok(d), tok(SGU_WIDTH), tok(GLA_VW), tok(GLA_VW), tok(GLA_VW),
                pl.BlockSpec((None, nh, tm, MLA_V), lambda b, i: (b, 0, i, 0)),
                vec, vec, vec, vec, _full((1, GLA_VW)), _full((GLA_VW, GLA_VW)),
                once((mix, d)), _full((1, d)), once((d, dff)), once((dff, d)), _full((1, d))]
    return pl.pallas_call(
        functools.partial(_out_kernel, final_norm=final_norm),
        grid=(bsz, n // tm), in_specs=in_specs, out_specs=tok(d),
        out_shape=jax.ShapeDtypeStruct((bsz, n, d), F32),
        scratch_shapes=[pltpu.VMEM((tm, mix), BF16)],
        compiler_params=_cparams(("parallel", "parallel")), name="out_ffn",
    )(x, ysgu, o_f, o_b, gr, ymla, g1, sh2, sc2, g2, lw["glaw"], lw["hsum"], lw["w_out"], lw["n2w"],
      lw["w1"], lw["w2"], fnw)


def _rope_tables(n):
    half = MLA_ROPE // 2
    pos = np.arange(n)
    freq = ROPE_BASE ** (-np.arange(half // 2, dtype=np.float64) * 2.0 / half)
    ang_r = (pos // GRID_W)[:, None] * freq[None, :]
    ang_c = (pos % GRID_W)[:, None] * freq[None, :]
    cos = np.zeros((n, LANES), np.float32)
    sin = np.zeros((n, LANES), np.float32)
    cos[:, 0:64] = np.concatenate([np.cos(ang_r), np.cos(ang_r), np.cos(ang_c), np.cos(ang_c)], axis=1)
    sin[:, 0:64] = np.concatenate([-np.sin(ang_r), np.sin(ang_r), -np.sin(ang_c), np.sin(ang_c)], axis=1)
    return jnp.asarray(cos), jnp.asarray(sin)


def _identity_tables(n):
    cos = np.zeros((n, LANES), np.float32)
    cos[:, 0:64] = 1.0
    return jnp.asarray(cos), jnp.zeros((n, LANES), F32)


_ROT_PARTNER = np.concatenate([np.arange(16, 32), np.arange(0, 16), np.arange(48, 64), np.arange(32, 48)])


def _layer_weights(l, p):
    w_in = p["w_in"][l]
    d = w_in.shape[0]
    o_gk, o_gv, o_gf, o_gb, o_ckv, o_kr, o_su, o_sv, o_gq, o_gr, o_dq = (
        0, 128, 384, 400, 416, 672, 736, 992, 1248, 1376, 1632)
    col = lambda o, w: w_in[:, o:o + w]
    kr = col(o_kr, MLA_ROPE)
    z = lambda w: jnp.zeros((d, w), w_in.dtype)
    w_in_p = jnp.concatenate([
        col(o_gq, 128), col(o_gk, 128), col(o_gv, 256), col(o_gr, 256), col(o_su, 256), col(o_sv, 256),
        col(o_ckv, 256), col(o_dq, 256),
        kr, col(o_gf, GLA_RANK), col(o_gb, GLA_RANK), z(32),
        kr[:, _ROT_PARTNER], z(64)], axis=1).astype(BF16)

    wg = jnp.zeros((LANES, 2 * GLA_KW), F32)
    wg = wg.at[64:80, 0:GLA_KW].set(p["gla_wg_fwd"][l]).at[80:96, GLA_KW:].set(p["gla_wg_bwd"][l])
    bg = jnp.concatenate([p["gla_bg_fwd"][l], p["gla_bg_bwd"][l]])[None, :]

    w_uq = p["mla_w_uq"][l].reshape(MLA_RANK, MLA_HEADS, MLA_NOPE + MLA_ROPE)
    nope = w_uq[:, :, :MLA_NOPE].reshape(MLA_RANK, -1)
    rope = w_uq[:, :, MLA_NOPE:]
    pad = lambda t: jnp.pad(t, ((0, 0), (0, 0), (0, LANES - MLA_ROPE))).reshape(MLA_RANK, -1)
    wuq = jnp.concatenate([nope, pad(rope), pad(rope[:, :, _ROT_PARTNER])], axis=1).astype(BF16)

    w_ukv = p["mla_w_ukv"][l].reshape(MLA_RANK, MLA_HEADS, MLA_NOPE + MLA_V)
    wukv = jnp.concatenate([w_ukv[:, :, :MLA_NOPE].reshape(MLA_RANK, -1),
                            w_ukv[:, :, MLA_NOPE:].reshape(MLA_RANK, -1)], axis=1).astype(BF16)

    hv = np.arange(GLA_VW) // GLA_DV
    return dict(
        n1w=p["norm1_w"][l][None, :], w_in=w_in_p,
        sgw=p["sgu_norm_w"][l][None, :], sgb=p["sgu_norm_b"][l][None, :],
        ws=p["sgu_w"][l].reshape(SGU_HEADS * SGU_CHUNK, SGU_CHUNK).astype(BF16),
        bs=jnp.repeat(p["sgu_b"][l].T, SGU_HEAD_DIM, axis=1),
        wg=wg.astype(BF16), bg=bg,
        qnw=p["mla_q_norm_w"][l][None, :], wuq=wuq, kvnw=p["mla_kv_norm_w"][l][None, :], wukv=wukv,
        glaw=jnp.tile(p["gla_norm_w"][l], GLA_HEADS)[None, :],
        hsum=jnp.asarray((hv[:, None] == hv[None, :]).astype(np.float32), BF16),
        w_out=p["w_out"][l].astype(BF16), n2w=p["norm2_w"][l][None, :],
        w1=p["w_ff1"][l].astype(BF16), w2=p["w_ff2"][l].astype(BF16))


def _pick(n, pref):
    for t in pref:
        if n % t == 0:
            return t
    return n


def kernel(x, c, ctx, c_ctx, w_mod, b_mod, norm1_w, w_in, w_out, sgu_norm_w, sgu_norm_b, sgu_w, sgu_b, gla_wg_fwd, gla_bg_fwd, gla_wg_bwd, gla_bg_bwd, gla_norm_w, mla_q_norm_w, mla_w_uq, mla_kv_norm_w, mla_w_ukv, norm2_w, w_ff1, w_ff2, final_norm_w):
    p = dict(norm1_w=norm1_w, w_in=w_in, w_out=w_out, sgu_norm_w=sgu_norm_w, sgu_norm_b=sgu_norm_b, sgu_w=sgu_w,
             sgu_b=sgu_b, gla_wg_fwd=gla_wg_fwd, gla_bg_fwd=gla_bg_fwd, gla_wg_bwd=gla_wg_bwd, gla_bg_bwd=gla_bg_bwd,
             gla_norm_w=gla_norm_w, mla_q_norm_w=mla_q_norm_w, mla_w_uq=mla_w_uq, mla_kv_norm_w=mla_kv_norm_w,
             mla_w_ukv=mla_w_ukv, norm2_w=norm2_w, w_ff1=w_ff1, w_ff2=w_ff2)
    bsz, n, d = x.shape
    nc = ctx.shape[1]
    depth = w_mod.shape[0]
    fnw = final_norm_w[None, :]

    cvecs = jnp.zeros((8, d), F32).at[0:bsz].set(c).at[bsz].set(c_ctx)
    mod = _mod_call(cvecs, w_mod, b_mod)

    tm = _pick(n, (512, 256, 128))
    tmp = _pick(n, (1024, 512, 256, 128))
    tmc = _pick(nc, (256, 128))
    tq = _pick(n, (1024, 512, 256, 128))
    tk = _pick(n + nc, (1408, 768, 512, 256, 128))
    cos, sin = _rope_tables(n)
    cos_c, sin_c = _identity_tables(nc)

    xc = ctx
    for l in range(depth):
        last = l == depth - 1
        lw = _layer_weights(l, p)
        m = mod[l, 0:bsz].reshape(bsz, 1, 6, d)
        sh1, sc1, g1, sh2, sc2, g2 = [m[:, :, i] for i in range(6)]
        mc = jnp.broadcast_to(mod[l, bsz].reshape(1, 1, 6, d), (bsz, 1, 6, d))
        sh1c, sc1c, g1c, sh2c, sc2c, g2c = [mc[:, :, i] for i in range(6)]

        ysgu, qk, gv, gate, gr, q, k, vt = _proj_call(x, sh1, sc1, lw, cos, sin, tmp, n + nc, 0)
        ysgu_c, qk_c, gv_c, gate_c, gr_c, q_c, k, vt = _proj_call(xc, sh1c, sc1c, lw, cos_c, sin_c, tmc, n + nc, n,
                                                                   kv_prev=(k, vt))
        zero = jnp.zeros((bsz, 2, GLA_VW, GLA_KW), F32)
        of_c, ob_c, s_ctx = _gla_call(qk_c, gv_c, gate_c, zero, tmc)
        o_f, o_b, _ = _gla_call(qk, gv, gate, s_ctx, tm)
        ymla = _attn_call(q, k, vt, tq, tk, 0, n + nc)
        x = _out_call(x, ysgu, o_f, o_b, gr, ymla, g1, sh2, sc2, g2, lw, fnw, tm, last)

        if not last:
            ymla_c = _attn_call(q_c, k, vt, tmc, tmc, n, nc)
            xc = _out_call(xc, ysgu_c, of_c, ob_c, gr_c, ymla_c, g1c, sh2c, sc2c, g2c, lw, fnw, tmc, False)
    return x
```

```python
import functools
import math

import numpy as np
import jax
import jax.numpy as jnp
from jax import lax
from jax.experimental import pallas as pl
from jax.experimental.pallas import tpu as pltpu

F32 = jnp.float32
BF16 = jnp.bfloat16

EPS = 1e-6
GRID_W = 64
ROPE_BASE = 10000.0
LANES = 128
SGU_HEADS = 4
SGU_HEAD_DIM = 64
SGU_WIDTH = 256
SGU_CHUNK = 128
GLA_HEADS = 4
GLA_DK = 32
GLA_DV = 64
GLA_KW = 128
GLA_VW = 256
GLA_RANK = 16
GLA_TAU = 16.0
GLA_BLOCK = 16
GLA_CHUNK = 64
GLA_SAFE_DECAY = 60.0
MLA_HEADS = 4
MLA_RANK = 256
MLA_NOPE = 128
MLA_ROPE = 64
MLA_V = 128
MLA_QK = 256
MLA_VROWS = 144
VMEM_LIMIT = 56 * 1024 * 1024

C_QK_ROPE, C_GV_GR, C_SU_SV, C_CKV_DQ, C_END = 0, 512, 1024, 1536, 2048


def _cparams(sem):
    return pltpu.CompilerParams(dimension_semantics=sem, vmem_limit_bytes=VMEM_LIMIT)


def _dot(a, b):
    return jnp.dot(a, b, preferred_element_type=F32)


def _dot_nt(a, b):
    return lax.dot_general(a, b, (((1,), (1,)), ((), ())), preferred_element_type=F32)


def _dot_tn(a, b):
    return lax.dot_general(a, b, (((0,), (0,)), ((), ())), preferred_element_type=F32)


def _rms(x, w):
    return x * lax.rsqrt(jnp.mean(x * x, axis=-1, keepdims=True) + EPS) * w


def _gelu_tanh(x):
    return 0.5 * x * (1.0 + jnp.tanh(math.sqrt(2.0 / math.pi) * (x + 0.044715 * (x * x * x))))


def _silu(x):
    return x * jax.nn.sigmoid(x)


def _full(shape):
    return pl.BlockSpec(shape, lambda *_: (0,) * len(shape))


def _mod_kernel(c_ref, w_ref, b_ref, o_ref):
    s = _silu(c_ref[...]).astype(BF16)
    o_ref[...] = _dot(s, w_ref[...].astype(BF16)) + b_ref[...]


def _mod_call(cvecs, w_mod, b_mod):
    nl, d, d6 = w_mod.shape
    tn = 1536
    return pl.pallas_call(
        _mod_kernel,
        grid=(nl, d6 // tn),
        in_specs=[pl.BlockSpec((8, d), lambda l, j: (0, 0)),
                  pl.BlockSpec((None, d, tn), lambda l, j: (l, 0, j)),
                  pl.BlockSpec((None, 1, tn), lambda l, j: (l, 0, j))],
        out_specs=pl.BlockSpec((None, 8, tn), lambda l, j: (l, 0, j)),
        out_shape=jax.ShapeDtypeStruct((nl, 8, d6), F32),
        compiler_params=_cparams(("parallel", "parallel")),
        name="mod",
    )(cvecs, w_mod, b_mod.reshape(nl, 1, d6))


def _proj_kernel(x_ref, sh_ref, sc_ref, n1w_ref, win_ref, sgw_ref, sgb_ref, ws_ref, bs_ref, wg_ref, bg_ref,
                 qnw_ref, wuq_ref, kvnw_ref, wukv_ref, cos_ref, sin_ref, *rest, q_scale):
    ysgu_ref, qk_ref, gv_ref, gate_ref, gr_ref, q_ref, k_ref, vt_ref = rest[-8:]
    tm = x_ref.shape[0]
    h = _rms(x_ref[...], n1w_ref[...]) * (1.0 + sc_ref[...]) + sh_ref[...]
    hb = h.astype(BF16)

    def proj(lo, hi):
        return _dot(hb, win_ref[:, lo:hi])

    p0 = proj(C_QK_ROPE, C_GV_GR)
    qk_ref[:, 0:GLA_KW] = p0[:, 0:GLA_KW] * (GLA_DK ** -0.5)
    qk_ref[:, GLA_KW:] = p0[:, GLA_KW:2 * GLA_KW]
    ga, gb = p0[:, 2 * GLA_KW:2 * GLA_KW + LANES], p0[:, 2 * GLA_KW + LANES:]
    p1 = proj(C_GV_GR, C_SU_SV)
    gv_ref[...] = p1[:, 0:GLA_VW]
    gr_ref[...] = p1[:, GLA_VW:]

    p2 = proj(C_SU_SV, C_CKV_DQ)
    u = _gelu_tanh(p2[:, 0:SGU_WIDTH])
    g = _gelu_tanh(p2[:, SGU_WIDTH:])
    mu = jnp.mean(g, axis=-1, keepdims=True)
    gc = g - mu
    var = jnp.mean(gc * gc, axis=-1, keepdims=True)
    vln = (gc * lax.rsqrt(var + EPS) * sgw_ref[...] + sgb_ref[...]).astype(BF16)
    head_of_lane = lax.broadcasted_iota(jnp.int32, (SGU_CHUNK, SGU_WIDTH), 1) // SGU_HEAD_DIM
    ws = ws_ref[...]
    for c in range(0, tm // SGU_CHUNK, 2):
        r2 = _dot(ws, jnp.concatenate([vln[c * SGU_CHUNK:(c + 1) * SGU_CHUNK, :],
                                       vln[(c + 1) * SGU_CHUNK:(c + 2) * SGU_CHUNK, :]], axis=1))
        for half in range(2):
            rows = slice((c + half) * SGU_CHUNK, (c + half + 1) * SGU_CHUNK)
            r = r2[:, half * SGU_WIDTH:(half + 1) * SGU_WIDTH]
            s = r[0:SGU_CHUNK]
            for hh in range(1, SGU_HEADS):
                s = jnp.where(head_of_lane == hh, r[hh * SGU_CHUNK:(hh + 1) * SGU_CHUNK], s)
            ysgu_ref[rows, :] = (u[rows, :] * (s + bs_ref[...])).astype(BF16)

    z = _dot(ga.astype(BF16), wg_ref[...]) + bg_ref[...]
    log_sig = jnp.minimum(z, 0.0) - jnp.log1p(jnp.exp(-jnp.abs(z)))
    gate_ref[...] = log_sig * (1.0 / GLA_TAU)
    cos = cos_ref[...]
    sin = sin_ref[...]
    k_rope = (ga * cos + gb * sin).astype(BF16)

    p3 = proj(C_CKV_DQ, C_END)
    ckv = _rms(p3[:, 0:MLA_RANK], kvnw_ref[...]).astype(BF16)
    kv = _dot(ckv, wukv_ref[...])
    ones_row = (lax.broadcasted_iota(jnp.int32, (MLA_VROWS - MLA_V, tm), 0) == 0).astype(BF16)
    for hh in range(MLA_HEADS):
        k_ref[hh, :, 0:MLA_NOPE] = kv[:, hh * MLA_NOPE:(hh + 1) * MLA_NOPE].astype(BF16)
        k_ref[hh, :, MLA_NOPE:] = k_rope
        off = MLA_HEADS * MLA_NOPE + hh * MLA_V
        vt_ref[hh, 0:MLA_V, :] = kv[:, off:off + MLA_V].T.astype(BF16)
        vt_ref[hh, MLA_V:, :] = ones_row

    cq = _rms(p3[:, MLA_RANK:], qnw_ref[...]).astype(BF16)
    qq = _dot(cq, wuq_ref[...])
    nq = MLA_HEADS * LANES
    for hh in range(MLA_HEADS):
        sl = slice(hh * LANES, (hh + 1) * LANES)
        q_ref[hh, :, 0:MLA_NOPE] = (qq[:, sl] * q_scale).astype(BF16)
        rot = qq[:, nq + hh * LANES:nq + (hh + 1) * LANES] * cos + qq[:, 2 * nq + hh * LANES:2 * nq + (hh + 1) * LANES] * sin
        q_ref[hh, :, MLA_NOPE:] = (rot * q_scale).astype(BF16)


def _proj_call(x, sh, sc, lw, cos, sin, tm, nk_total, key_off, kv_prev=None):
    bsz, n, d = x.shape
    nh = MLA_HEADS
    kb = key_off // tm
    q_scale = (MLA_NOPE + MLA_ROPE) ** -0.5 * math.log2(math.e)
    tok = lambda w: pl.BlockSpec((None, tm, w), lambda b, i: (b, i, 0))
    vec = pl.BlockSpec((None, 1, d), lambda b, i: (b, 0, 0))
    heads = lambda w: pl.BlockSpec((None, nh, tm, w), lambda b, i: (b, 0, i, 0))
    tab = pl.BlockSpec((tm, LANES), lambda b, i: (i, 0))
    k_spec = pl.BlockSpec((None, nh, tm, MLA_QK), lambda b, i: (b, 0, kb + i, 0))
    vt_spec = pl.BlockSpec((None, nh, MLA_VROWS, tm), lambda b, i: (b, 0, 0, kb + i))
    in_specs = [tok(d), vec, vec, _full((1, d)), _full((d, C_END)),
                _full((1, SGU_WIDTH)), _full((1, SGU_WIDTH)), _full((SGU_HEADS * SGU_CHUNK, SGU_CHUNK)),
                _full((SGU_CHUNK, SGU_WIDTH)), _full((LANES, 2 * GLA_KW)), _full((1, 2 * GLA_KW)),
                _full((1, MLA_RANK)), _full((MLA_RANK, 3 * nh * LANES)),
                _full((1, MLA_RANK)), _full((MLA_RANK, nh * (MLA_NOPE + MLA_V))), tab, tab]
    out_specs = [tok(SGU_WIDTH), tok(2 * GLA_KW), tok(GLA_VW), tok(2 * GLA_KW), tok(GLA_VW),
                 heads(MLA_QK), k_spec, vt_spec]
    out_shape = [jax.ShapeDtypeStruct((bsz, n, SGU_WIDTH), BF16),
                 jax.ShapeDtypeStruct((bsz, n, 2 * GLA_KW), F32),
                 jax.ShapeDtypeStruct((bsz, n, GLA_VW), F32),
                 jax.ShapeDtypeStruct((bsz, n, 2 * GLA_KW), F32),
                 jax.ShapeDtypeStruct((bsz, n, GLA_VW), F32),
                 jax.ShapeDtypeStruct((bsz, nh, n, MLA_QK), BF16),
                 jax.ShapeDtypeStruct((bsz, nh, nk_total, MLA_QK), BF16),
                 jax.ShapeDtypeStruct((bsz, nh, MLA_VROWS, nk_total), BF16)]
    args = [x, sh, sc, lw["n1w"], lw["w_in"], lw["sgw"], lw["sgb"], lw["ws"], lw["bs"], lw["wg"], lw["bg"],
            lw["qnw"], lw["wuq"], lw["kvnw"], lw["wukv"], cos, sin]
    aliases = {}
    if kv_prev is not None:
        aliases = {len(args): 6, len(args) + 1: 7}
        in_specs = in_specs + [pl.BlockSpec(memory_space=pl.ANY)] * 2
        args = args + list(kv_prev)
    return pl.pallas_call(
        functools.partial(_proj_kernel, q_scale=q_scale),
        grid=(bsz, n // tm), in_specs=in_specs, out_specs=out_specs, out_shape=out_shape,
        input_output_aliases=aliases,
        compiler_params=_cparams(("parallel", "parallel")), name="proj",
    )(*args)


def _split3(x):
    hi = x.astype(BF16)
    r1 = x - hi.astype(F32)
    mid = r1.astype(BF16)
    lo = (r1 - mid.astype(F32)).astype(BF16)
    return hi, mid, lo


def _row_bcast(ref, r, n):
    return ref[pl.ds(r, n, stride=0), :]


def _gla_block(q_ref, k_ref, v0_ref, v1_ref, b_ref, st_ref, o_ref, r0, fwd, ind, mask_t):
    nb = GLA_BLOCK
    rows = pl.ds(r0, nb)
    bcast = lambda ref, r: _row_bcast(ref, r, nb)
    q = q_ref[rows, :]
    k = k_ref[rows, :]
    v = jnp.concatenate([v0_ref[rows, :], v1_ref[rows, :]], axis=1)
    b = b_ref[rows, :]
    b_end = bcast(b_ref, r0 + (nb - 1) if fwd else r0)
    st = st_ref[...]
    o = _dot_nt((q * jnp.exp(b)).astype(BF16), st.astype(BF16))
    kh = (k * jnp.exp(b_end - b)).astype(BF16)
    st_ref[...] = st * jnp.exp(b_end[0:1, :]) + mask_t * _dot_tn(v.astype(BF16), kh)

    row = lax.broadcasted_iota(jnp.int32, (nb, GLA_KW), 0)
    parts = []
    for j in range(nb):
        valid = (row >= j) if fwd else (row <= j)
        e = jnp.exp(jnp.where(valid, b - bcast(b_ref, r0 + j), -jnp.inf))
        parts.append((e * q * bcast(k_ref, r0 + j)).astype(BF16))
    r = _dot(jnp.concatenate(parts, axis=0), ind)
    for j in range(nb):
        vj = jnp.concatenate([bcast(v0_ref, r0 + j), bcast(v1_ref, r0 + j)], axis=1)
        o = o + r[j * nb:(j + 1) * nb] * vj
    o_ref[rows, :] = o


def _chunk_prep(q_ref, k_ref, b_ref, a_sc, qt_sc, kh_sc, d, slot, r0, fwd, qmask, tri):
    nc = GLA_CHUNK
    rows = pl.ds(r0, nc)
    b = b_ref[rows, :]
    e_end = jnp.exp(b_ref[pl.ds(r0 + (nc - 1) if fwd else r0, 1), :])
    qt = q_ref[rows, :] * jnp.exp(b)
    kt = k_ref[rows, :] * jnp.exp(-b)
    q4 = (jnp.concatenate([qt] * GLA_HEADS, axis=0) * qmask).astype(BF16)
    a_sc[d, slot] = jnp.where(tri, _dot_nt(q4, kt.astype(BF16)), 0.0).astype(BF16)
    qt_sc[d, slot] = qt.astype(BF16)
    kh_sc[d, slot] = (kt * e_end).astype(BF16)


def _chunk_apply(v_ref, b_ref, st_ref, o_ref, a_sc, qt_sc, kh_sc, d, slot, r0, fwd, mask_t):
    nc = GLA_CHUNK
    rows = pl.ds(r0, nc)
    vb = v_ref[rows, :].astype(BF16)
    e_end = jnp.exp(b_ref[pl.ds(r0 + (nc - 1) if fwd else r0, 1), :])
    st = st_ref[...]
    o = _dot_nt(qt_sc[d, slot], st.astype(BF16))
    st_ref[...] = st * e_end + mask_t * _dot_tn(vb, kh_sc[d, slot])
    r = _dot(a_sc[d, slot], vb)
    head_of_lane = lax.broadcasted_iota(jnp.int32, (nc, GLA_VW), 1) // GLA_DV
    intra = r[0:nc]
    for h in range(1, GLA_HEADS):
        intra = jnp.where(head_of_lane == h, r[h * nc:(h + 1) * nc], intra)
    o_ref[rows, :] = o + intra


def _cumsum_blocks(g_ref, tri_ref, b_sc):
    tri = tri_ref[...]
    span = tri.shape[0]
    for c in range(g_ref.shape[0] // span):
        rows = slice(c * span, (c + 1) * span)
        hi, mid, lo = _split3(g_ref[rows, :])
        b_sc[rows, :] = _dot(tri, hi) + _dot(tri, mid) + _dot(tri, lo)


def _gla_kernel(qf_ref, kf_ref, v0f_ref, v1f_ref, vf_ref, gf_ref, qb_ref, kb_ref, v0b_ref, v1b_ref, vb_ref, gb_ref,
                s0_ref, t16f_ref, t16b_ref, t64f_ref, t64b_ref, ind_ref, maskt_ref, qmask_ref,
                of_ref, ob_ref, sfin_ref, stf, stb, bf, bb, a_sc, qt_sc, kh_sc):
    t = pl.program_id(1)
    tt = qf_ref.shape[0]

    @pl.when(t == 0)
    def _():
        stf[...] = s0_ref[0]
        stb[...] = s0_ref[1]

    _cumsum_blocks(gf_ref, t64f_ref, bf)
    _cumsum_blocks(gb_ref, t64b_ref, bb)
    lowest = jnp.minimum(jnp.min(bf[...]), jnp.min(bb[...]))
    mask_t = maskt_ref[...]

    @pl.when(lowest > -GLA_SAFE_DECAY)
    def _():
        nch = tt // GLA_CHUNK
        qmask = qmask_ref[...]
        row = lax.broadcasted_iota(jnp.int32, (GLA_HEADS * GLA_CHUNK, GLA_CHUNK), 0) % GLA_CHUNK
        col = lax.broadcasted_iota(jnp.int32, (GLA_HEADS * GLA_CHUNK, GLA_CHUNK), 1)
        lower, upper = col <= row, col >= row

        off = lambda c: pl.multiple_of(c * GLA_CHUNK, GLA_CHUNK)

        def prep(c_f, c_b, slot):
            _chunk_prep(qf_ref, kf_ref, bf, a_sc, qt_sc, kh_sc, 0, slot, off(c_f), True, qmask, lower)
            _chunk_prep(qb_ref, kb_ref, bb, a_sc, qt_sc, kh_sc, 1, slot, off(c_b), False, qmask, upper)

        prep(0, nch - 1, 0)

        def step(i, carry):
            slot = i % 2
            _chunk_apply(vf_ref, bf, stf, of_ref, a_sc, qt_sc, kh_sc, 0, slot, off(i), True, mask_t)
            _chunk_apply(vb_ref, bb, stb, ob_ref, a_sc, qt_sc, kh_sc, 1, slot, off(nch - 1 - i), False, mask_t)
            nxt = jnp.minimum(i + 1, nch - 1)
            prep(nxt, nch - 1 - nxt, 1 - slot)
            return carry

        lax.fori_loop(0, nch, step, 0, unroll=2)

    @pl.when(jnp.logical_not(lowest > -GLA_SAFE_DECAY))
    def _():
        nblk = tt // GLA_BLOCK
        _cumsum_blocks(gf_ref, t16f_ref, bf)
        _cumsum_blocks(gb_ref, t16b_ref, bb)
        ind = ind_ref[...]

        def step(i, carry):
            rf = pl.multiple_of(i * GLA_BLOCK, GLA_BLOCK)
            rb = pl.multiple_of((nblk - 1 - i) * GLA_BLOCK, GLA_BLOCK)
            _gla_block(qf_ref, kf_ref, v0f_ref, v1f_ref, bf, stf, of_ref, rf, True, ind, mask_t)
            _gla_block(qb_ref, kb_ref, v0b_ref, v1b_ref, bb, stb, ob_ref, rb, False, ind, mask_t)
            return carry

        lax.fori_loop(0, nblk, step, 0)

    @pl.when(t == pl.num_programs(1) - 1)
    def _():
        sfin_ref[0] = stf[...]
        sfin_ref[1] = stb[...]


def _tri(span, blk):
    i = np.arange(span)
    same = (i[:, None] // blk) == (i[None, :] // blk)
    lower = (same & (i[None, :] <= i[:, None])).astype(np.float32)
    upper = (same & (i[None, :] >= i[:, None])).astype(np.float32)
    return jnp.asarray(lower, BF16), jnp.asarray(upper, BF16)


def _gla_call(qk, gv, gates, s0, tt):
    bsz, n, _ = qk.shape
    nt = n // tt
    span = min(tt, 256)
    t16f, t16b = _tri(span, GLA_BLOCK)
    t64f, t64b = _tri(span, GLA_CHUNK)
    kd = np.arange(GLA_KW) // GLA_DK
    ve = np.arange(GLA_VW) // GLA_DV
    ind = (kd[:, None] == ve[None, :]).astype(np.float32)
    qmask = (np.repeat(np.arange(GLA_HEADS), GLA_CHUNK)[:, None] == kd[None, :]).astype(np.float32)
    half = lambda rev, c: pl.BlockSpec((None, tt, LANES), lambda b, t: (b, nt - 1 - t if rev else t, c))
    full = lambda rev: pl.BlockSpec((None, tt, GLA_VW), lambda b, t: (b, nt - 1 - t if rev else t, 0))
    st = pl.BlockSpec((None, 2, GLA_VW, GLA_KW), lambda b, t: (b, 0, 0, 0))
    fwd_in = [half(False, 0), half(False, 1), half(False, 0), half(False, 1), full(False), half(False, 0)]
    bwd_in = [half(True, 0), half(True, 1), half(True, 0), half(True, 1), full(True), half(True, 1)]
    return pl.pallas_call(
        _gla_kernel,
        grid=(bsz, nt),
        in_specs=fwd_in + bwd_in + [st] + [_full((span, span))] * 4 + [
            _full((GLA_KW, GLA_VW)), _full((GLA_VW, GLA_KW)), _full((GLA_HEADS * GLA_CHUNK, GLA_KW))],
        out_specs=[full(False), full(True), st],
        out_shape=[jax.ShapeDtypeStruct((bsz, n, GLA_VW), F32), jax.ShapeDtypeStruct((bsz, n, GLA_VW), F32),
                   jax.ShapeDtypeStruct((bsz, 2, GLA_VW, GLA_KW), F32)],
        scratch_shapes=[pltpu.VMEM((GLA_VW, GLA_KW), F32), pltpu.VMEM((GLA_VW, GLA_KW), F32),
                        pltpu.VMEM((tt, GLA_KW), F32), pltpu.VMEM((tt, GLA_KW), F32),
                        pltpu.VMEM((2, 2, GLA_HEADS * GLA_CHUNK, GLA_CHUNK), BF16),
                        pltpu.VMEM((2, 2, GLA_CHUNK, GLA_KW), BF16), pltpu.VMEM((2, 2, GLA_CHUNK, GLA_KW), BF16)],
        compiler_params=_cparams(("parallel", "arbitrary")), name="gla",
    )(qk, qk, gv, gv, gv, gates, qk, qk, gv, gv, gv, gates, s0, t16f, t16b, t64f, t64b,
      jnp.asarray(ind, BF16), jnp.asarray(ind.T, F32), jnp.asarray(qmask, F32))


def _attn_kernel(q_ref, qn_ref, k_ref, vt_ref, o_ref, s_sc, m_sc, acc_sc, qt_sc, *, tk):
    nk = k_ref.shape[0] // tk
    i = pl.program_id(2)
    carry_scores = nk % 2 == 0
    par = i % 2 if carry_scores else 0

    def scores(j, slot, qslot):
        r0 = pl.multiple_of(j * tk, tk)
        s_sc[slot] = _dot(k_ref[pl.ds(r0, tk), :], qt_sc[qslot])

    def first_scores():
        qt_sc[0] = q_ref[...].T
        scores(0, 0, 0)

    if carry_scores:
        pl.when(i == 0)(first_scores)
    else:
        first_scores()
    m_sc[...] = jnp.full(m_sc.shape, -jnp.inf, F32)
    acc_sc[...] = jnp.zeros(acc_sc.shape, F32)

    def update(j, slot):
        s = s_sc[slot]
        m_prev = m_sc[...]
        m_new = jnp.maximum(m_prev, jnp.max(s, axis=0, keepdims=True))
        alpha = jnp.exp2(m_prev - m_new)
        p = jnp.exp2(s - m_new).astype(BF16)
        r0 = pl.multiple_of(j * tk, tk)
        acc_sc[...] = alpha * acc_sc[...] + _dot(vt_ref[:, pl.ds(r0, tk)], p)
        m_sc[...] = m_new

    def pair(t, carry):
        j = 2 * t
        scores(j + 1, 1, par)
        update(j, 0)
        scores(j + 2, 0, par)
        update(j + 1, 1)
        return carry

    lax.fori_loop(0, (nk - 1) // 2, pair, 0)
    if carry_scores:
        scores(nk - 1, 1, par)
        update(nk - 2, 0)
        qt_sc[1 - par] = qn_ref[...].T
        scores(0, 0, 1 - par)
        update(nk - 1, 1)
    else:
        update(nk - 1, 0)
    acc = acc_sc[...]
    o_t = acc[0:MLA_V] * (1.0 / acc[MLA_V:MLA_V + 1])
    o_ref[...] = o_t.T.astype(o_ref.dtype)


def _attn_call(q, k, vt, tq, tk, key_off, nk):
    bsz, nh, nq, dq = q.shape
    kb = key_off // nk
    nqt = nq // tq
    return pl.pallas_call(
        functools.partial(_attn_kernel, tk=tk),
        grid=(bsz, nh, nqt),
        in_specs=[pl.BlockSpec((None, None, tq, dq), lambda b, h, i: (b, h, i, 0)),
                  pl.BlockSpec((None, None, tq, dq), lambda b, h, i: (b, h, jnp.minimum(i + 1, nqt - 1), 0)),
                  pl.BlockSpec((None, None, nk, dq), lambda b, h, i: (b, h, kb, 0)),
                  pl.BlockSpec((None, None, MLA_VROWS, nk), lambda b, h, i: (b, h, 0, kb))],
        out_specs=pl.BlockSpec((None, None, tq, MLA_V), lambda b, h, i: (b, h, i, 0)),
        out_shape=jax.ShapeDtypeStruct((bsz, nh, nq, MLA_V), BF16),
        scratch_shapes=[pltpu.VMEM((2, tk, tq), F32), pltpu.VMEM((1, tq), F32), pltpu.VMEM((MLA_VROWS, tq), F32),
                        pltpu.VMEM((2, dq, tq), BF16)],
        compiler_params=_cparams(("parallel", "parallel", "arbitrary")), name="attn",
    )(q, q, k, vt)


def _out_kernel(x_ref, ysgu_ref, of_ref, ob_ref, gr_ref, ymla_ref, g1_ref, sh2_ref, sc2_ref, g2_ref, glaw_ref,
                hsum_ref, wout_ref, n2w_ref, w1_ref, w2_ref, fnw_ref, o_ref, y_sc, *, final_norm):
    o = of_ref[...] + ob_ref[...]
    o2 = o * o
    hi = o2.astype(BF16)
    lo = (o2 - hi.astype(F32)).astype(BF16)
    ms = (_dot(hi, hsum_ref[...]) + _dot(lo, hsum_ref[...])) * (1.0 / GLA_DV)
    y_gla = o * lax.rsqrt(ms + EPS) * glaw_ref[...] * _silu(gr_ref[...])
    y_sc[:, 0:SGU_WIDTH] = ysgu_ref[...]
    y_sc[:, SGU_WIDTH:SGU_WIDTH + GLA_VW] = y_gla.astype(BF16)
    for hh in range(MLA_HEADS):
        off = SGU_WIDTH + GLA_VW + hh * MLA_V
        y_sc[:, off:off + MLA_V] = ymla_ref[hh]
    x1 = x_ref[...] + g1_ref[...] * _dot(y_sc[...], wout_ref[...])
    hn = (_rms(x1, n2w_ref[...]) * (1.0 + sc2_ref[...]) + sh2_ref[...]).astype(BF16)
    dff = w1_ref.shape[1]
    fc = 1024
    acc = jnp.zeros(x1.shape, F32)
    for c in range(dff // fc):
        a = jnp.maximum(_dot(hn, w1_ref[:, c * fc:(c + 1) * fc]), 0.0)
        acc = acc + _dot((a * a).astype(BF16), w2_ref[c * fc:(c + 1) * fc, :])
    x2 = x1 + g2_ref[...] * acc
    if final_norm:
        x2 = _rms(x2, fnw_ref[...])
    o_ref[...] = x2


def _out_call(x, ysgu, o_f, o_b, gr, ymla, g1, sh2, sc2, g2, lw, fnw, tm, final_norm):
    bsz, n, d = x.shape
    nh = MLA_HEADS
    mix = SGU_WIDTH + GLA_VW + nh * MLA_V
    dff = lw["w1"].shape[1]
    tok = lambda w: pl.BlockSpec((None, tm, w), lambda b, i: (b, i, 0))
    vec = pl.BlockSpec((None, 1, d), lambda b, i: (b, 0, 0))
    once = lambda shape: pl.BlockSpec(shape, lambda *_: (0,) * len(shape), pipeline_mode=pl.Buffered(1))
    in_specs = [tok(d), tok(SGU_WIDTH), tok(GLA_VW), tok(GLA_VW), tok(GLA_VW),
                pl.BlockSpec((None, nh, tm, MLA_V), lambda b, i: (b, 0, i, 0)),
                vec, vec, vec, vec, _full((1, GLA_VW)), _full((GLA_VW, GLA_VW)),
                once((mix, d)), _full((1, d)), once((d, dff)), once((dff, d)), _full((1, d))]
    return pl.pallas_call(
        functools.partial(_out_kernel, final_norm=final_norm),
        grid=(bsz, n // tm), in_specs=in_specs, out_specs=tok(d),
        out_shape=jax.ShapeDtypeStruct((bsz, n, d), F32),
        scratch_shapes=[pltpu.VMEM((tm, mix), BF16)],
        compiler_params=_cparams(("parallel", "parallel")), name="out_ffn",
    )(x, ysgu, o_f, o_b, gr, ymla, g1, sh2, sc2, g2, lw["glaw"], lw["hsum"], lw["w_out"], lw["n2w"],
      lw["w1"], lw["w2"], fnw)


def _rope_tables(n):
    half = MLA_ROPE // 2
    pos = np.arange(n)
    freq = ROPE_BASE ** (-np.arange(half // 2, dtype=np.float64) * 2.0 / half)
    ang_r = (pos // GRID_W)[:, None] * freq[None, :]
    ang_c = (pos % GRID_W)[:, None] * freq[None, :]
    cos = np.zeros((n, LANES), np.float32)
    sin = np.zeros((n, LANES), np.float32)
    cos[:, 0:64] = np.concatenate([np.cos(ang_r), np.cos(ang_r), np.cos(ang_c), np.cos(ang_c)], axis=1)
    sin[:, 0:64] = np.concatenate([-np.sin(ang_r), np.sin(ang_r), -np.sin(ang_c), np.sin(ang_c)], axis=1)
    return jnp.asarray(cos), jnp.asarray(sin)


def _identity_tables(n):
    cos = np.zeros((n, LANES), np.float32)
    cos[:, 0:64] = 1.0
    return jnp.asarray(cos), jnp.zeros((n, LANES), F32)


_ROT_PARTNER = np.concatenate([np.arange(16, 32), np.arange(0, 16), np.arange(48, 64), np.arange(32, 48)])


def _layer_weights(l, p):
    w_in = p["w_in"][l]
    d = w_in.shape[0]
    o_gk, o_gv, o_gf, o_gb, o_ckv, o_kr, o_su, o_sv, o_gq, o_gr, o_dq = (
        0, 128, 384, 400, 416, 672, 736, 992, 1248, 1376, 1632)
    col = lambda o, w: w_in[:, o:o + w]
    kr = col(o_kr, MLA_ROPE)
    z = lambda w: jnp.zeros((d, w), w_in.dtype)
    w_in_p = jnp.concatenate([
        col(o_gq, 128), col(o_gk, 128),
        kr, col(o_gf, GLA_RANK), col(o_gb, GLA_RANK), z(32), kr[:, _ROT_PARTNER], z(64),
        col(o_gv, 256), col(o_gr, 256), col(o_su, 256), col(o_sv, 256),
        col(o_ckv, 256), col(o_dq, 256)], axis=1).astype(BF16)

    wg = jnp.zeros((LANES, 2 * GLA_KW), F32)
    wg = wg.at[64:80, 0:GLA_KW].set(p["gla_wg_fwd"][l]).at[80:96, GLA_KW:].set(p["gla_wg_bwd"][l])
    bg = jnp.concatenate([p["gla_bg_fwd"][l], p["gla_bg_bwd"][l]])[None, :]

    w_uq = p["mla_w_uq"][l].reshape(MLA_RANK, MLA_HEADS, MLA_NOPE + MLA_ROPE)
    nope = w_uq[:, :, :MLA_NOPE].reshape(MLA_RANK, -1)
    rope = w_uq[:, :, MLA_NOPE:]
    pad = lambda t: jnp.pad(t, ((0, 0), (0, 0), (0, LANES - MLA_ROPE))).reshape(MLA_RANK, -1)
    wuq = jnp.concatenate([nope, pad(rope), pad(rope[:, :, _ROT_PARTNER])], axis=1).astype(BF16)

    w_ukv = p["mla_w_ukv"][l].reshape(MLA_RANK, MLA_HEADS, MLA_NOPE + MLA_V)
    wukv = jnp.concatenate([w_ukv[:, :, :MLA_NOPE].reshape(MLA_RANK, -1),
                            w_ukv[:, :, MLA_NOPE:].reshape(MLA_RANK, -1)], axis=1).astype(BF16)

    hv = np.arange(GLA_VW) // GLA_DV
    return dict(
        n1w=p["norm1_w"][l][None, :], w_in=w_in_p,
        sgw=p["sgu_norm_w"][l][None, :], sgb=p["sgu_norm_b"][l][None, :],
        ws=p["sgu_w"][l].reshape(SGU_HEADS * SGU_CHUNK, SGU_CHUNK).astype(BF16),
        bs=jnp.repeat(p["sgu_b"][l].T, SGU_HEAD_DIM, axis=1),
        wg=wg.astype(BF16), bg=bg,
        qnw=p["mla_q_norm_w"][l][None, :], wuq=wuq, kvnw=p["mla_kv_norm_w"][l][None, :], wukv=wukv,
        glaw=jnp.tile(p["gla_norm_w"][l], GLA_HEADS)[None, :],
        hsum=jnp.asarray((hv[:, None] == hv[None, :]).astype(np.float32), BF16),
        w_out=p["w_out"][l].astype(BF16), n2w=p["norm2_w"][l][None, :],
        w1=p["w_ff1"][l].astype(BF16), w2=p["w_ff2"][l].astype(BF16))


def _pick(n, pref):
    for t in pref:
        if n % t == 0:
            return t
    return n


def kernel(x, c, ctx, c_ctx, w_mod, b_mod, norm1_w, w_in, w_out, sgu_norm_w, sgu_norm_b, sgu_w, sgu_b, gla_wg_fwd, gla_bg_fwd, gla_wg_bwd, gla_bg_bwd, gla_norm_w, mla_q_norm_w, mla_w_uq, mla_kv_norm_w, mla_w_ukv, norm2_w, w_ff1, w_ff2, final_norm_w):
    p = dict(norm1_w=norm1_w, w_in=w_in, w_out=w_out, sgu_norm_w=sgu_norm_w, sgu_norm_b=sgu_norm_b, sgu_w=sgu_w,
             sgu_b=sgu_b, gla_wg_fwd=gla_wg_fwd, gla_bg_fwd=gla_bg_fwd, gla_wg_bwd=gla_wg_bwd, gla_bg_bwd=gla_bg_bwd,
             gla_norm_w=gla_norm_w, mla_q_norm_w=mla_q_norm_w, mla_w_uq=mla_w_uq, mla_kv_norm_w=mla_kv_norm_w,
             mla_w_ukv=mla_w_ukv, norm2_w=norm2_w, w_ff1=w_ff1, w_ff2=w_ff2)
    bsz, n, d = x.shape
    nc = ctx.shape[1]
    depth = w_mod.shape[0]
    fnw = final_norm_w[None, :]

    cvecs = jnp.zeros((8, d), F32).at[0:bsz].set(c).at[bsz].set(c_ctx)
    mod = _mod_call(cvecs, w_mod, b_mod)

    tm = _pick(n, (512, 256, 128))
    tmp = _pick(n, (1024, 512, 256, 128))
    tmc = _pick(nc, (256, 128))
    tq = _pick(n, (1024, 512, 256, 128))
    tk = _pick(n + nc, (1408, 768, 512, 256, 128))
    cos, sin = _rope_tables(n)
    cos_c, sin_c = _identity_tables(nc)

    xc = ctx
    for l in range(depth):
        last = l == depth - 1
        lw = _layer_weights(l, p)
        m = mod[l, 0:bsz].reshape(bsz, 1, 6, d)
        sh1, sc1, g1, sh2, sc2, g2 = [m[:, :, i] for i in range(6)]
        mc = jnp.broadcast_to(mod[l, bsz].reshape(1, 1, 6, d), (bsz, 1, 6, d))
        sh1c, sc1c, g1c, sh2c, sc2c, g2c = [mc[:, :, i] for i in range(6)]

        ysgu, qk, gv, gate, gr, q, k, vt = _proj_call(x, sh1, sc1, lw, cos, sin, tmp, n + nc, 0)
        ysgu_c, qk_c, gv_c, gate_c, gr_c, q_c, k, vt = _proj_call(xc, sh1c, sc1c, lw, cos_c, sin_c, tmc, n + nc, n,
                                                                   kv_prev=(k, vt))
        zero = jnp.zeros((bsz, 2, GLA_VW, GLA_KW), F32)
        of_c, ob_c, s_ctx = _gla_call(qk_c, gv_c, gate_c, zero, tmc)
        o_f, o_b, _ = _gla_call(qk, gv, gate, s_ctx, tm)
        ymla = _attn_call(q, k, vt, tq, tk, 0, n + nc)
        x = _out_call(x, ysgu, o_f, o_b, gr, ymla, g1, sh2, sc2, g2, lw, fnw, tm, last)

        if not last:
            ymla_c = _attn_call(q_c, k, vt, tmc, tmc, n, nc)
            xc = _out_call(xc, ysgu_c, of_c, ob_c, gr_c, ymla_c, g1c, sh2c, sc2c, g2c, lw, fnw, tmc, False)
    return x
```

```python
import functools
import math

import numpy as np
import jax
import jax.numpy as jnp
from jax import lax
from jax.experimental import pallas as pl
from jax.experimental.pallas import tpu as pltpu

F32 = jnp.float32
BF16 = jnp.bfloat16

EPS = 1e-6
GRID_W = 64
ROPE_BASE = 10000.0
LANES = 128
SGU_HEADS = 4
SGU_HEAD_DIM = 64
SGU_WIDTH = 256
SGU_CHUNK = 128
GLA_HEADS = 4
GLA_DK = 32
GLA_DV = 64
GLA_KW = 128
GLA_VW = 256
GLA_RANK = 16
GLA_TAU = 16.0
GLA_BLOCK = 16
GLA_CHUNK = 64
GLA_SAFE_DECAY = 60.0
MLA_HEADS = 4
MLA_RANK = 256
MLA_NOPE = 128
MLA_ROPE = 64
MLA_V = 128
MLA_QK = 256
MLA_VROWS = 144
VMEM_LIMIT = 56 * 1024 * 1024

C_QK_ROPE, C_GV_GR, C_SU_SV, C_CKV_DQ, C_END = 0, 512, 1024, 1536, 2048


def _cparams(sem):
    return pltpu.CompilerParams(dimension_semantics=sem, vmem_limit_bytes=VMEM_LIMIT)


def _dot(a, b):
    return jnp.dot(a, b, preferred_element_type=F32)


def _dot_nt(a, b):
    return lax.dot_general(a, b, (((1,), (1,)), ((), ())), preferred_element_type=F32)


def _dot_tn(a, b):
    return lax.dot_general(a, b, (((0,), (0,)), ((), ())), preferred_element_type=F32)


def _rms(x, w):
    return x * lax.rsqrt(jnp.mean(x * x, axis=-1, keepdims=True) + EPS) * w


def _gelu_tanh(x):
    return 0.5 * x * (1.0 + jnp.tanh(math.sqrt(2.0 / math.pi) * (x + 0.044715 * (x * x * x))))


def _silu(x):
    return x * jax.nn.sigmoid(x)


def _full(shape):
    return pl.BlockSpec(shape, lambda *_: (0,) * len(shape))


def _mod_kernel(c_ref, w_ref, b_ref, o_ref):
    s = _silu(c_ref[...]).astype(BF16)
    o_ref[...] = _dot(s, w_ref[...].astype(BF16)) + b_ref[...]


def _mod_call(cvecs, w_mod, b_mod):
    nl, d, d6 = w_mod.shape
    tn = 1536
    return pl.pallas_call(
        _mod_kernel,
        grid=(nl, d6 // tn),
        in_specs=[pl.BlockSpec((8, d), lambda l, j: (0, 0)),
                  pl.BlockSpec((None, d, tn), lambda l, j: (l, 0, j)),
                  pl.BlockSpec((None, 1, tn), lambda l, j: (l, 0, j))],
        out_specs=pl.BlockSpec((None, 8, tn), lambda l, j: (l, 0, j)),
        out_shape=jax.ShapeDtypeStruct((nl, 8, d6), F32),
        compiler_params=_cparams(("parallel", "parallel")),
        name="mod",
    )(cvecs, w_mod, b_mod.reshape(nl, 1, d6))


def _proj_kernel(x_ref, sh_ref, sc_ref, n1w_ref, win_ref, sgw_ref, sgb_ref, ws_ref, bs_ref, wg_ref, bg_ref,
                 qnw_ref, wuq_ref, kvnw_ref, wukv_ref, cos_ref, sin_ref, *rest, q_scale):
    ysgu_ref, qk_ref, gv_ref, gate_ref, gr_ref, q_ref, k_ref, vt_ref = rest[-8:]
    tm = x_ref.shape[0]
    h = _rms(x_ref[...], n1w_ref[...]) * (1.0 + sc_ref[...]) + sh_ref[...]
    hb = h.astype(BF16)

    def proj(lo, hi):
        return _dot(hb, win_ref[:, lo:hi])

    p0 = proj(C_QK_ROPE, C_GV_GR)
    qk_ref[:, 0:GLA_KW] = p0[:, 0:GLA_KW] * (GLA_DK ** -0.5)
    qk_ref[:, GLA_KW:] = p0[:, GLA_KW:2 * GLA_KW]
    ga, gb = p0[:, 2 * GLA_KW:2 * GLA_KW + LANES], p0[:, 2 * GLA_KW + LANES:]
    p1 = proj(C_GV_GR, C_SU_SV)
    gv_ref[...] = p1[:, 0:GLA_VW]
    gr_ref[...] = p1[:, GLA_VW:]

    p2 = proj(C_SU_SV, C_CKV_DQ)
    u = _gelu_tanh(p2[:, 0:SGU_WIDTH])
    g = _gelu_tanh(p2[:, SGU_WIDTH:])
    mu = jnp.mean(g, axis=-1, keepdims=True)
    gc = g - mu
    var = jnp.mean(gc * gc, axis=-1, keepdims=True)
    vln = (gc * lax.rsqrt(var + EPS) * sgw_ref[...] + sgb_ref[...]).astype(BF16)
    head_of_lane = lax.broadcasted_iota(jnp.int32, (SGU_CHUNK, SGU_WIDTH), 1) // SGU_HEAD_DIM
    ws = ws_ref[...]
    for c in range(0, tm // SGU_CHUNK, 2):
        r2 = _dot(ws, jnp.concatenate([vln[c * SGU_CHUNK:(c + 1) * SGU_CHUNK, :],
                                       vln[(c + 1) * SGU_CHUNK:(c + 2) * SGU_CHUNK, :]], axis=1))
        for half in range(2):
            rows = slice((c + half) * SGU_CHUNK, (c + half + 1) * SGU_CHUNK)
            r = r2[:, half * SGU_WIDTH:(half + 1) * SGU_WIDTH]
            s = r[0:SGU_CHUNK]
            for hh in range(1, SGU_HEADS):
                s = jnp.where(head_of_lane == hh, r[hh * SGU_CHUNK:(hh + 1) * SGU_CHUNK], s)
            ysgu_ref[rows, :] = (u[rows, :] * (s + bs_ref[...])).astype(BF16)

    z = _dot(ga.astype(BF16), wg_ref[...]) + bg_ref[...]
    log_sig = jnp.minimum(z, 0.0) - jnp.log1p(jnp.exp(-jnp.abs(z)))
    gate_ref[...] = log_sig * (1.0 / GLA_TAU)
    cos = cos_ref[...]
    sin = sin_ref[...]
    k_rope = (ga * cos + gb * sin).astype(BF16)

    p3 = proj(C_CKV_DQ, C_END)
    ckv = _rms(p3[:, 0:MLA_RANK], kvnw_ref[...]).astype(BF16)
    kv = _dot(ckv, wukv_ref[...])
    ones_row = (lax.broadcasted_iota(jnp.int32, (MLA_VROWS - MLA_V, tm), 0) == 0).astype(BF16)
    for hh in range(MLA_HEADS):
        k_ref[hh, :, 0:MLA_NOPE] = kv[:, hh * MLA_NOPE:(hh + 1) * MLA_NOPE].astype(BF16)
        k_ref[hh, :, MLA_NOPE:] = k_rope
        off = MLA_HEADS * MLA_NOPE + hh * MLA_V
        vt_ref[hh, 0:MLA_V, :] = kv[:, off:off + MLA_V].T.astype(BF16)
        vt_ref[hh, MLA_V:, :] = ones_row

    cq = _rms(p3[:, MLA_RANK:], qnw_ref[...]).astype(BF16)
    qq = _dot(cq, wuq_ref[...])
    nq = MLA_HEADS * LANES
    for hh in range(MLA_HEADS):
        sl = slice(hh * LANES, (hh + 1) * LANES)
        q_ref[hh, :, 0:MLA_NOPE] = (qq[:, sl] * q_scale).astype(BF16)
        rot = qq[:, nq + hh * LANES:nq + (hh + 1) * LANES] * cos + qq[:, 2 * nq + hh * LANES:2 * nq + (hh + 1) * LANES] * sin
        q_ref[hh, :, MLA_NOPE:] = (rot * q_scale).astype(BF16)


def _proj_call(x, sh, sc, lw, cos, sin, tm, nk_total, key_off, kv_prev=None):
    bsz, n, d = x.shape
    nh = MLA_HEADS
    kb = key_off // tm
    q_scale = (MLA_NOPE + MLA_ROPE) ** -0.5 * math.log2(math.e)
    tok = lambda w: pl.BlockSpec((None, tm, w), lambda b, i: (b, i, 0))
    vec = pl.BlockSpec((None, 1, d), lambda b, i: (b, 0, 0))
    heads = lambda w: pl.BlockSpec((None, nh, tm, w), lambda b, i: (b, 0, i, 0))
    tab = pl.BlockSpec((tm, LANES), lambda b, i: (i, 0))
    k_spec = pl.BlockSpec((None, nh, tm, MLA_QK), lambda b, i: (b, 0, kb + i, 0))
    vt_spec = pl.BlockSpec((None, nh, MLA_VROWS, tm), lambda b, i: (b, 0, 0, kb + i))
    in_specs = [tok(d), vec, vec, _full((1, d)), _full((d, C_END)),
                _full((1, SGU_WIDTH)), _full((1, SGU_WIDTH)), _full((SGU_HEADS * SGU_CHUNK, SGU_CHUNK)),
                _full((SGU_CHUNK, SGU_WIDTH)), _full((LANES, 2 * GLA_KW)), _full((1, 2 * GLA_KW)),
                _full((1, MLA_RANK)), _full((MLA_RANK, 3 * nh * LANES)),
                _full((1, MLA_RANK)), _full((MLA_RANK, nh * (MLA_NOPE + MLA_V))), tab, tab]
    out_specs = [tok(SGU_WIDTH), tok(2 * GLA_KW), tok(GLA_VW), tok(2 * GLA_KW), tok(GLA_VW),
                 heads(MLA_QK), k_spec, vt_spec]
    out_shape = [jax.ShapeDtypeStruct((bsz, n, SGU_WIDTH), BF16),
                 jax.ShapeDtypeStruct((bsz, n, 2 * GLA_KW), F32),
                 jax.ShapeDtypeStruct((bsz, n, GLA_VW), F32),
                 jax.ShapeDtypeStruct((bsz, n, 2 * GLA_KW), F32),
                 jax.ShapeDtypeStruct((bsz, n, GLA_VW), F32),
                 jax.ShapeDtypeStruct((bsz, nh, n, MLA_QK), BF16),
                 jax.ShapeDtypeStruct((bsz, nh, nk_total, MLA_QK), BF16),
                 jax.ShapeDtypeStruct((bsz, nh, MLA_VROWS, nk_total), BF16)]
    args = [x, sh, sc, lw["n1w"], lw["w_in"], lw["sgw"], lw["sgb"], lw["ws"], lw["bs"], lw["wg"], lw["bg"],
            lw["qnw"], lw["wuq"], lw["kvnw"], lw["wukv"], cos, sin]
    aliases = {}
    if kv_prev is not None:
        aliases = {len(args): 6, len(args) + 1: 7}
        in_specs = in_specs + [pl.BlockSpec(memory_space=pl.ANY)] * 2
        args = args + list(kv_prev)
    return pl.pallas_call(
        functools.partial(_proj_kernel, q_scale=q_scale),
        grid=(bsz, n // tm), in_specs=in_specs, out_specs=out_specs, out_shape=out_shape,
        input_output_aliases=aliases,
        compiler_params=_cparams(("parallel", "parallel")), name="proj",
    )(*args)


def _split3(x):
    hi = x.astype(BF16)
    r1 = x - hi.astype(F32)
    mid = r1.astype(BF16)
    lo = (r1 - mid.astype(F32)).astype(BF16)
    return hi, mid, lo


def _row_bcast(ref, r, n):
    return ref[pl.ds(r, n, stride=0), :]


def _gla_block(q_ref, k_ref, v0_ref, v1_ref, b_ref, st_ref, o_ref, r0, fwd, ind, mask_t):
    nb = GLA_BLOCK
    rows = pl.ds(r0, nb)
    bcast = lambda ref, r: _row_bcast(ref, r, nb)
    q = q_ref[rows, :]
    k = k_ref[rows, :]
    v = jnp.concatenate([v0_ref[rows, :], v1_ref[rows, :]], axis=1)
    b = b_ref[rows, :]
    b_end = bcast(b_ref, r0 + (nb - 1) if fwd else r0)
    st = st_ref[...]
    o = _dot_nt((q * jnp.exp(b)).astype(BF16), st.astype(BF16))
    kh = (k * jnp.exp(b_end - b)).astype(BF16)
    st_ref[...] = st * jnp.exp(b_end[0:1, :]) + mask_t * _dot_tn(v.astype(BF16), kh)

    row = lax.broadcasted_iota(jnp.int32, (nb, GLA_KW), 0)
    parts = []
    for j in range(nb):
        valid = (row >= j) if fwd else (row <= j)
        e = jnp.exp(jnp.where(valid, b - bcast(b_ref, r0 + j), -jnp.inf))
        parts.append((e * q * bcast(k_ref, r0 + j)).astype(BF16))
    r = _dot(jnp.concatenate(parts, axis=0), ind)
    for j in range(nb):
        vj = jnp.concatenate([bcast(v0_ref, r0 + j), bcast(v1_ref, r0 + j)], axis=1)
        o = o + r[j * nb:(j + 1) * nb] * vj
    o_ref[rows, :] = o


def _chunk_prep(q_ref, k_ref, b_ref, a_sc, qt_sc, kh_sc, d, slot, r0, fwd, qmask, tri):
    nc = GLA_CHUNK
    rows = pl.ds(r0, nc)
    b = b_ref[rows, :]
    e_end = jnp.exp(b_ref[pl.ds(r0 + (nc - 1) if fwd else r0, 1), :])
    qt = q_ref[rows, :] * jnp.exp(b)
    kt = k_ref[rows, :] * jnp.exp(-b)
    q4 = (jnp.concatenate([qt] * GLA_HEADS, axis=0) * qmask).astype(BF16)
    a_sc[d, slot] = jnp.where(tri, _dot_nt(q4, kt.astype(BF16)), 0.0).astype(BF16)
    qt_sc[d, slot] = qt.astype(BF16)
    kh_sc[d, slot] = (kt * e_end).astype(BF16)


def _chunk_apply(v_ref, b_ref, st_ref, o_ref, a_sc, qt_sc, kh_sc, d, slot, r0, fwd, mask_t):
    nc = GLA_CHUNK
    rows = pl.ds(r0, nc)
    vb = v_ref[rows, :].astype(BF16)
    e_end = jnp.exp(b_ref[pl.ds(r0 + (nc - 1) if fwd else r0, 1), :])
    st = st_ref[...]
    o = _dot_nt(qt_sc[d, slot], st.astype(BF16))
    st_ref[...] = st * e_end + mask_t * _dot_tn(vb, kh_sc[d, slot])
    r = _dot(a_sc[d, slot], vb)
    head_of_lane = lax.broadcasted_iota(jnp.int32, (nc, GLA_VW), 1) // GLA_DV
    intra = r[0:nc]
    for h in range(1, GLA_HEADS):
        intra = jnp.where(head_of_lane == h, r[h * nc:(h + 1) * nc], intra)
    o_ref[rows, :] = o + intra


def _cumsum_blocks(g_ref, tri_ref, b_sc):
    tri = tri_ref[...]
    span = tri.shape[0]
    for c in range(g_ref.shape[0] // span):
        rows = slice(c * span, (c + 1) * span)
        hi, mid, lo = _split3(g_ref[rows, :])
        b_sc[rows, :] = _dot(tri, hi) + _dot(tri, mid) + _dot(tri, lo)


def _gla_kernel(qf_ref, kf_ref, v0f_ref, v1f_ref, vf_ref, gf_ref, qb_ref, kb_ref, v0b_ref, v1b_ref, vb_ref, gb_ref,
                s0_ref, t16f_ref, t16b_ref, t64f_ref, t64b_ref, ind_ref, maskt_ref, qmask_ref,
                of_ref, ob_ref, sfin_ref, stf, stb, bf, bb, a_sc, qt_sc, kh_sc):
    t = pl.program_id(1)
    tt = qf_ref.shape[0]

    @pl.when(t == 0)
    def _():
        stf[...] = s0_ref[0]
        stb[...] = s0_ref[1]

    _cumsum_blocks(gf_ref, t64f_ref, bf)
    _cumsum_blocks(gb_ref, t64b_ref, bb)
    lowest = jnp.minimum(jnp.min(bf[...]), jnp.min(bb[...]))
    mask_t = maskt_ref[...]

    @pl.when(lowest > -GLA_SAFE_DECAY)
    def _():
        nch = tt // GLA_CHUNK
        qmask = qmask_ref[...]
        row = lax.broadcasted_iota(jnp.int32, (GLA_HEADS * GLA_CHUNK, GLA_CHUNK), 0) % GLA_CHUNK
        col = lax.broadcasted_iota(jnp.int32, (GLA_HEADS * GLA_CHUNK, GLA_CHUNK), 1)
        lower, upper = col <= row, col >= row

        off = lambda c: pl.multiple_of(c * GLA_CHUNK, GLA_CHUNK)

        def prep(c_f, c_b, slot):
            _chunk_prep(qf_ref, kf_ref, bf, a_sc, qt_sc, kh_sc, 0, slot, off(c_f), True, qmask, lower)
            _chunk_prep(qb_ref, kb_ref, bb, a_sc, qt_sc, kh_sc, 1, slot, off(c_b), False, qmask, upper)

        prep(0, nch - 1, 0)

        def step(i, carry):
            slot = i % 2
            _chunk_apply(vf_ref, bf, stf, of_ref, a_sc, qt_sc, kh_sc, 0, slot, off(i), True, mask_t)
            _chunk_apply(vb_ref, bb, stb, ob_ref, a_sc, qt_sc, kh_sc, 1, slot, off(nch - 1 - i), False, mask_t)
            nxt = jnp.minimum(i + 1, nch - 1)
            prep(nxt, nch - 1 - nxt, 1 - slot)
            return carry

        lax.fori_loop(0, nch, step, 0, unroll=4)

    @pl.when(jnp.logical_not(lowest > -GLA_SAFE_DECAY))
    def _():
        nblk = tt // GLA_BLOCK
        _cumsum_blocks(gf_ref, t16f_ref, bf)
        _cumsum_blocks(gb_ref, t16b_ref, bb)
        ind = ind_ref[...]

        def step(i, carry):
            rf = pl.multiple_of(i * GLA_BLOCK, GLA_BLOCK)
            rb = pl.multiple_of((nblk - 1 - i) * GLA_BLOCK, GLA_BLOCK)
            _gla_block(qf_ref, kf_ref, v0f_ref, v1f_ref, bf, stf, of_ref, rf, True, ind, mask_t)
            _gla_block(qb_ref, kb_ref, v0b_ref, v1b_ref, bb, stb, ob_ref, rb, False, ind, mask_t)
            return carry

        lax.fori_loop(0, nblk, step, 0)

    @pl.when(t == pl.num_programs(1) - 1)
    def _():
        sfin_ref[0] = stf[...]
        sfin_ref[1] = stb[...]


def _tri(span, blk):
    i = np.arange(span)
    same = (i[:, None] // blk) == (i[None, :] // blk)
    lower = (same & (i[None, :] <= i[:, None])).astype(np.float32)
    upper = (same & (i[None, :] >= i[:, None])).astype(np.float32)
    return jnp.asarray(lower, BF16), jnp.asarray(upper, BF16)


def _gla_call(qk, gv, gates, s0, tt):
    bsz, n, _ = qk.shape
    nt = n // tt
    span = min(tt, 256)
    t16f, t16b = _tri(span, GLA_BLOCK)
    t64f, t64b = _tri(span, GLA_CHUNK)
    kd = np.arange(GLA_KW) // GLA_DK
    ve = np.arange(GLA_VW) // GLA_DV
    ind = (kd[:, None] == ve[None, :]).astype(np.float32)
    qmask = (np.repeat(np.arange(GLA_HEADS), GLA_CHUNK)[:, None] == kd[None, :]).astype(np.float32)
    half = lambda rev, c: pl.BlockSpec((None, tt, LANES), lambda b, t: (b, nt - 1 - t if rev else t, c))
    full = lambda rev: pl.BlockSpec((None, tt, GLA_VW), lambda b, t: (b, nt - 1 - t if rev else t, 0))
    st = pl.BlockSpec((None, 2, GLA_VW, GLA_KW), lambda b, t: (b, 0, 0, 0))
    fwd_in = [half(False, 0), half(False, 1), half(False, 0), half(False, 1), full(False), half(False, 0)]
    bwd_in = [half(True, 0), half(True, 1), half(True, 0), half(True, 1), full(True), half(True, 1)]
    return pl.pallas_call(
        _gla_kernel,
        grid=(bsz, nt),
        in_specs=fwd_in + bwd_in + [st] + [_full((span, span))] * 4 + [
            _full((GLA_KW, GLA_VW)), _full((GLA_VW, GLA_KW)), _full((GLA_HEADS * GLA_CHUNK, GLA_KW))],
        out_specs=[full(False), full(True), st],
        out_shape=[jax.ShapeDtypeStruct((bsz, n, GLA_VW), F32), jax.ShapeDtypeStruct((bsz, n, GLA_VW), F32),
                   jax.ShapeDtypeStruct((bsz, 2, GLA_VW, GLA_KW), F32)],
        scratch_shapes=[pltpu.VMEM((GLA_VW, GLA_KW), F32), pltpu.VMEM((GLA_VW, GLA_KW), F32),
                        pltpu.VMEM((tt, GLA_KW), F32), pltpu.VMEM((tt, GLA_KW), F32),
                        pltpu.VMEM((2, 2, GLA_HEADS * GLA_CHUNK, GLA_CHUNK), BF16),
                        pltpu.VMEM((2, 2, GLA_CHUNK, GLA_KW), BF16), pltpu.VMEM((2, 2, GLA_CHUNK, GLA_KW), BF16)],
        compiler_params=_cparams(("parallel", "arbitrary")), name="gla",
    )(qk, qk, gv, gv, gv, gates, qk, qk, gv, gv, gv, gates, s0, t16f, t16b, t64f, t64b,
      jnp.asarray(ind, BF16), jnp.asarray(ind.T, F32), jnp.asarray(qmask, F32))


def _attn_kernel(q_ref, qn_ref, k_ref, vt_ref, o_ref, s_sc, m_sc, acc_sc, qt_sc, *, tk):
    nk = k_ref.shape[0] // tk
    i = pl.program_id(2)
    carry_scores = nk % 2 == 0
    par = i % 2 if carry_scores else 0

    def scores(j, slot, qslot):
        r0 = pl.multiple_of(j * tk, tk)
        s_sc[slot] = _dot(k_ref[pl.ds(r0, tk), :], qt_sc[qslot])

    def first_scores():
        qt_sc[0] = q_ref[...].T
        scores(0, 0, 0)

    if carry_scores:
        pl.when(i == 0)(first_scores)
    else:
        first_scores()
    m_sc[...] = jnp.full(m_sc.shape, -jnp.inf, F32)
    acc_sc[...] = jnp.zeros(acc_sc.shape, F32)

    def update(j, slot):
        s = s_sc[slot]
        m_prev = m_sc[...]
        m_new = jnp.maximum(m_prev, jnp.max(s, axis=0, keepdims=True))
        alpha = jnp.exp2(m_prev - m_new)
        p = jnp.exp2(s - m_new).astype(BF16)
        r0 = pl.multiple_of(j * tk, tk)
        acc_sc[...] = alpha * acc_sc[...] + _dot(vt_ref[:, pl.ds(r0, tk)], p)
        m_sc[...] = m_new

    def pair(t, carry):
        j = 2 * t
        scores(j + 1, 1, par)
        update(j, 0)
        scores(j + 2, 0, par)
        update(j + 1, 1)
        return carry

    lax.fori_loop(0, (nk - 1) // 2, pair, 0)
    if carry_scores:
        scores(nk - 1, 1, par)
        update(nk - 2, 0)
        qt_sc[1 - par] = qn_ref[...].T
        scores(0, 0, 1 - par)
        update(nk - 1, 1)
    else:
        update(nk - 1, 0)
    acc = acc_sc[...]
    o_t = acc[0:MLA_V] * (1.0 / acc[MLA_V:MLA_V + 1])
    o_ref[...] = o_t.T.astype(o_ref.dtype)


def _attn_call(q, k, vt, tq, tk, key_off, nk):
    bsz, nh, nq, dq = q.shape
    kb = key_off // nk
    nqt = nq // tq
    return pl.pallas_call(
        functools.partial(_attn_kernel, tk=tk),
        grid=(bsz, nh, nqt),
        in_specs=[pl.BlockSpec((None, None, tq, dq), lambda b, h, i: (b, h, i, 0)),
                  pl.BlockSpec((None, None, tq, dq), lambda b, h, i: (b, h, jnp.minimum(i + 1, nqt - 1), 0)),
                  pl.BlockSpec((None, None, nk, dq), lambda b, h, i: (b, h, kb, 0)),
                  pl.BlockSpec((None, None, MLA_VROWS, nk), lambda b, h, i: (b, h, 0, kb))],
        out_specs=pl.BlockSpec((None, None, tq, MLA_V), lambda b, h, i: (b, h, i, 0)),
        out_shape=jax.ShapeDtypeStruct((bsz, nh, nq, MLA_V), BF16),
        scratch_shapes=[pltpu.VMEM((2, tk, tq), F32), pltpu.VMEM((1, tq), F32), pltpu.VMEM((MLA_VROWS, tq), F32),
                        pltpu.VMEM((2, dq, tq), BF16)],
        compiler_params=_cparams(("parallel", "parallel", "arbitrary")), name="attn",
    )(q, q, k, vt)


def _out_kernel(x_ref, ysgu_ref, of_ref, ob_ref, gr_ref, ymla_ref, g1_ref, sh2_ref, sc2_ref, g2_ref, glaw_ref,
                hsum_ref, wout_ref, n2w_ref, w1_ref, w2_ref, fnw_ref, o_ref, y_sc, *, final_norm):
    o = of_ref[...] + ob_ref[...]
    o2 = o * o
    hi = o2.astype(BF16)
    lo = (o2 - hi.astype(F32)).astype(BF16)
    ms = (_dot(hi, hsum_ref[...]) + _dot(lo, hsum_ref[...])) * (1.0 / GLA_DV)
    y_gla = o * lax.rsqrt(ms + EPS) * glaw_ref[...] * _silu(gr_ref[...])
    y_sc[:, 0:SGU_WIDTH] = ysgu_ref[...]
    y_sc[:, SGU_WIDTH:SGU_WIDTH + GLA_VW] = y_gla.astype(BF16)
    for hh in range(MLA_HEADS):
        off = SGU_WIDTH + GLA_VW + hh * MLA_V
        y_sc[:, off:off + MLA_V] = ymla_ref[hh]
    x1 = x_ref[...] + g1_ref[...] * _dot(y_sc[...], wout_ref[...])
    hn = (_rms(x1, n2w_ref[...]) * (1.0 + sc2_ref[...]) + sh2_ref[...]).astype(BF16)
    dff = w1_ref.shape[1]
    fc = 1024
    acc = jnp.zeros(x1.shape, F32)
    for c in range(dff // fc):
        a = jnp.maximum(_dot(hn, w1_ref[:, c * fc:(c + 1) * fc]), 0.0)
        acc = acc + _dot((a * a).astype(BF16), w2_ref[c * fc:(c + 1) * fc, :])
    x2 = x1 + g2_ref[...] * acc
    if final_norm:
        x2 = _rms(x2, fnw_ref[...])
    o_ref[...] = x2


def _out_call(x, ysgu, o_f, o_b, gr, ymla, g1, sh2, sc2, g2, lw, fnw, tm, final_norm):
    bsz, n, d = x.shape
    nh = MLA_HEADS
    mix = SGU_WIDTH + GLA_VW + nh * MLA_V
    dff = lw["w1"].shape[1]
    tok = lambda w: pl.BlockSpec((None, tm, w), lambda b, i: (b, i, 0))
    vec = pl.BlockSpec((None, 1, d), lambda b, i: (b, 0, 0))
    once = lambda shape: pl.BlockSpec(shape, lambda *_: (0,) * len(shape), pipeline_mode=pl.Buffered(1))
    in_specs = [tok(d), tok(SGU_WIDTH), tok(GLA_VW), tok(GLA_VW), tok(GLA_VW),
                pl.BlockSpec((None, nh, tm, MLA_V), lambda b, i: (b, 0, i, 0)),
                vec, vec, vec, vec, _full((1, GLA_VW)), _full((GLA_VW, GLA_VW)),
                once((mix, d)), _full((1, d)), once((d, dff)), once((dff, d)), _full((1, d))]
    return pl.pallas_call(
        functools.partial(_out_kernel, final_norm=final_norm),
        grid=(bsz, n // tm), in_specs=in_specs, out_specs=tok(d),
        out_shape=jax.ShapeDtypeStruct((bsz, n, d), F32),
        scratch_shapes=[pltpu.VMEM((tm, mix), BF16)],
        compiler_params=_cparams(("parallel", "parallel")), name="out_ffn",
    )(x, ysgu, o_f, o_b, gr, ymla, g1, sh2, sc2, g2, lw["glaw"], lw["hsum"], lw["w_out"], lw["n2w"],
      lw["w1"], lw["w2"], fnw)


def _rope_tables(n):
    half = MLA_ROPE // 2
    pos = np.arange(n)
    freq = ROPE_BASE ** (-np.arange(half // 2, dtype=np.float64) * 2.0 / half)
    ang_r = (pos // GRID_W)[:, None] * freq[None, :]
    ang_c = (pos % GRID_W)[:, None] * freq[None, :]
    cos = np.zeros((n, LANES), np.float32)
    sin = np.zeros((n, LANES), np.float32)
    cos[:, 0:64] = np.concatenate([np.cos(ang_r), np.cos(ang_r), np.cos(ang_c), np.cos(ang_c)], axis=1)
    sin[:, 0:64] = np.concatenate([-np.sin(ang_r), np.sin(ang_r), -np.sin(ang_c), np.sin(ang_c)], axis=1)
    return jnp.asarray(cos), jnp.asarray(sin)


def _identity_tables(n):
    cos = np.zeros((n, LANES), np.float32)
    cos[:, 0:64] = 1.0
    return jnp.asarray(cos), jnp.zeros((n, LANES), F32)


_ROT_PARTNER = np.concatenate([np.arange(16, 32), np.arange(0, 16), np.arange(48, 64), np.arange(32, 48)])


def _layer_weights(l, p):
    w_in = p["w_in"][l]
    d = w_in.shape[0]
    o_gk, o_gv, o_gf, o_gb, o_ckv, o_kr, o_su, o_sv, o_gq, o_gr, o_dq = (
        0, 128, 384, 400, 416, 672, 736, 992, 1248, 1376, 1632)
    col = lambda o, w: w_in[:, o:o + w]
    kr = col(o_kr, MLA_ROPE)
    z = lambda w: jnp.zeros((d, w), w_in.dtype)
    w_in_p = jnp.concatenate([
        col(o_gq, 128), col(o_gk, 128),
        kr, col(o_gf, GLA_RANK), col(o_gb, GLA_RANK), z(32), kr[:, _ROT_PARTNER], z(64),
        col(o_gv, 256), col(o_gr, 256), col(o_su, 256), col(o_sv, 256),
        col(o_ckv, 256), col(o_dq, 256)], axis=1).astype(BF16)

    wg = jnp.zeros((LANES, 2 * GLA_KW), F32)
    wg = wg.at[64:80, 0:GLA_KW].set(p["gla_wg_fwd"][l]).at[80:96, GLA_KW:].set(p["gla_wg_bwd"][l])
    bg = jnp.concatenate([p["gla_bg_fwd"][l], p["gla_bg_bwd"][l]])[None, :]

    w_uq = p["mla_w_uq"][l].reshape(MLA_RANK, MLA_HEADS, MLA_NOPE + MLA_ROPE)
    nope = w_uq[:, :, :MLA_NOPE].reshape(MLA_RANK, -1)
    rope = w_uq[:, :, MLA_NOPE:]
    pad = lambda t: jnp.pad(t, ((0, 0), (0, 0), (0, LANES - MLA_ROPE))).reshape(MLA_RANK, -1)
    wuq = jnp.concatenate([nope, pad(rope), pad(rope[:, :, _ROT_PARTNER])], axis=1).astype(BF16)

    w_ukv = p["mla_w_ukv"][l].reshape(MLA_RANK, MLA_HEADS, MLA_NOPE + MLA_V)
    wukv = jnp.concatenate([w_ukv[:, :, :MLA_NOPE].reshape(MLA_RANK, -1),
                            w_ukv[:, :, MLA_NOPE:].reshape(MLA_RANK, -1)], axis=1).astype(BF16)

    hv = np.arange(GLA_VW) // GLA_DV
    return dict(
        n1w=p["norm1_w"][l][None, :], w_in=w_in_p,
        sgw=p["sgu_norm_w"][l][None, :], sgb=p["sgu_norm_b"][l][None, :],
        ws=p["sgu_w"][l].reshape(SGU_HEADS * SGU_CHUNK, SGU_CHUNK).astype(BF16),
        bs=jnp.repeat(p["sgu_b"][l].T, SGU_HEAD_DIM, axis=1),
        wg=wg.astype(BF16), bg=bg,
        qnw=p["mla_q_norm_w"][l][None, :], wuq=wuq, kvnw=p["mla_kv_norm_w"][l][None, :], wukv=wukv,
        glaw=jnp.tile(p["gla_norm_w"][l], GLA_HEADS)[None, :],
        hsum=jnp.asarray((hv[:, None] == hv[None, :]).astype(np.float32), BF16),
        w_out=p["w_out"][l].astype(BF16), n2w=p["norm2_w"][l][None, :],
        w1=p["w_ff1"][l].astype(BF16), w2=p["w_ff2"][l].astype(BF16))


def _pick(n, pref):
    for t in pref:
        if n % t == 0:
            return t
    return n


def kernel(x, c, ctx, c_ctx, w_mod, b_mod, norm1_w, w_in, w_out, sgu_norm_w, sgu_norm_b, sgu_w, sgu_b, gla_wg_fwd, gla_bg_fwd, gla_wg_bwd, gla_bg_bwd, gla_norm_w, mla_q_norm_w, mla_w_uq, mla_kv_norm_w, mla_w_ukv, norm2_w, w_ff1, w_ff2, final_norm_w):
    p = dict(norm1_w=norm1_w, w_in=w_in, w_out=w_out, sgu_norm_w=sgu_norm_w, sgu_norm_b=sgu_norm_b, sgu_w=sgu_w,
             sgu_b=sgu_b, gla_wg_fwd=gla_wg_fwd, gla_bg_fwd=gla_bg_fwd, gla_wg_bwd=gla_wg_bwd, gla_bg_bwd=gla_bg_bwd,
             gla_norm_w=gla_norm_w, mla_q_norm_w=mla_q_norm_w, mla_w_uq=mla_w_uq, mla_kv_norm_w=mla_kv_norm_w,
             mla_w_ukv=mla_w_ukv, norm2_w=norm2_w, w_ff1=w_ff1, w_ff2=w_ff2)
    bsz, n, d = x.shape
    nc = ctx.shape[1]
    depth = w_mod.shape[0]
    fnw = final_norm_w[None, :]

    cvecs = jnp.zeros((8, d), F32).at[0:bsz].set(c).at[bsz].set(c_ctx)
    mod = _mod_call(cvecs, w_mod, b_mod)

    tm = _pick(n, (512, 256, 128))
    tmp = _pick(n, (1024, 512, 256, 128))
    tmc = _pick(nc, (256, 128))
    tq = _pick(n, (1024, 512, 256, 128))
    tk = _pick(n + nc, (1408, 768, 512, 256, 128))
    cos, sin = _rope_tables(n)
    cos_c, sin_c = _identity_tables(nc)

    xc = ctx
    for l in range(depth):
        last = l == depth - 1
        lw = _layer_weights(l, p)
        m = mod[l, 0:bsz].reshape(bsz, 1, 6, d)
        sh1, sc1, g1, sh2, sc2, g2 = [m[:, :, i] for i in range(6)]
        mc = jnp.broadcast_to(mod[l, bsz].reshape(1, 1, 6, d), (bsz, 1, 6, d))
        sh1c, sc1c, g1c, sh2c, sc2c, g2c = [mc[:, :, i] for i in range(6)]

        ysgu, qk, gv, gate, gr, q, k, vt = _proj_call(x, sh1, sc1, lw, cos, sin, tmp, n + nc, 0)
        ysgu_c, qk_c, gv_c, gate_c, gr_c, q_c, k, vt = _proj_call(xc, sh1c, sc1c, lw, cos_c, sin_c, tmc, n + nc, n,
                                                                   kv_prev=(k, vt))
        zero = jnp.zeros((bsz, 2, GLA_VW, GLA_KW), F32)
        of_c, ob_c, s_ctx = _gla_call(qk_c, gv_c, gate_c, zero, tmc)
        o_f, o_b, _ = _gla_call(qk, gv, gate, s_ctx, tmp)
        ymla = _attn_call(q, k, vt, tq, tk, 0, n + nc)
        x = _out_call(x, ysgu, o_f, o_b, gr, ymla, g1, sh2, sc2, g2, lw, fnw, tm, last)

        if not last:
            ymla_c = _attn_call(q_c, k, vt, tmc, tmc, n, nc)
            xc = _out_call(xc, ysgu_c, of_c, ob_c, gr_c, ymla_c, g1c, sh2c, sc2c, g2c, lw, fnw, tmc, False)
    return x
```

```python
import functools
import math
from typing import NamedTuple

import numpy as np
import jax
import jax.numpy as jnp
from jax import lax
from jax.experimental import pallas as pl
from jax.experimental.pallas import tpu as pltpu

F32 = jnp.float32
BF16 = jnp.bfloat16

EPS = 1e-6
GRID_W = 64
ROPE_BASE = 10000.0
LANES = 128
SGU_HEADS = 4
SGU_HEAD_DIM = 64
SGU_WIDTH = 256
SGU_CHUNK = 128
GLA_HEADS = 4
GLA_DK = 32
GLA_DV = 64
GLA_KW = 128
GLA_VW = 256
GLA_RANK = 16
GLA_TAU = 16.0
GLA_BLOCK = 16
GLA_CHUNK = 64
GLA_SAFE_DECAY = 60.0
MLA_HEADS = 4
MLA_RANK = 256
MLA_NOPE = 128
MLA_ROPE = 64
MLA_V = 128
MLA_QK = 256
MLA_VROWS = 144
VMEM_LIMIT = 56 * 1024 * 1024

C_QK_ROPE, C_GV_GR, C_SU_SV, C_CKV_DQ, C_END = 0, 512, 1024, 1536, 2048


def _cparams(sem):
    return pltpu.CompilerParams(dimension_semantics=sem, vmem_limit_bytes=VMEM_LIMIT)


def _dot(a, b):
    return jnp.dot(a, b, preferred_element_type=F32)


def _dot_nt(a, b):
    return lax.dot_general(a, b, (((1,), (1,)), ((), ())), preferred_element_type=F32)


def _dot_tn(a, b):
    return lax.dot_general(a, b, (((0,), (0,)), ((), ())), preferred_element_type=F32)


def _rms(x, w):
    return x * lax.rsqrt(jnp.mean(x * x, axis=-1, keepdims=True) + EPS) * w


def _gelu_tanh(x):
    return 0.5 * x * (1.0 + jnp.tanh(math.sqrt(2.0 / math.pi) * (x + 0.044715 * (x * x * x))))


def _silu(x):
    return x * jax.nn.sigmoid(x)


def _full(shape):
    return pl.BlockSpec(shape, lambda *_: (0,) * len(shape))


def _mod_kernel(c_ref, w_ref, b_ref, o_ref):
    s = _silu(c_ref[...]).astype(BF16)
    o_ref[...] = _dot(s, w_ref[...].astype(BF16)) + b_ref[...]


def _mod_call(cvecs, w_mod, b_mod):
    nl, d, d6 = w_mod.shape
    tn = 1536
    return pl.pallas_call(
        _mod_kernel,
        grid=(nl, d6 // tn),
        in_specs=[pl.BlockSpec((8, d), lambda l, j: (0, 0)),
                  pl.BlockSpec((None, d, tn), lambda l, j: (l, 0, j)),
                  pl.BlockSpec((None, 1, tn), lambda l, j: (l, 0, j))],
        out_specs=pl.BlockSpec((None, 8, tn), lambda l, j: (l, 0, j)),
        out_shape=jax.ShapeDtypeStruct((nl, 8, d6), F32),
        compiler_params=_cparams(("parallel", "parallel")),
        name="mod",
    )(cvecs, w_mod, b_mod.reshape(nl, 1, d6))


def _proj_kernel(x_ref, sh_ref, sc_ref, n1w_ref, win_ref, sgw_ref, sgb_ref, ws_ref, bs_ref, wg_ref, bg_ref,
                 qnw_ref, wuq_ref, kvnw_ref, wukv_ref, cos_ref, sin_ref, *rest, q_scale):
    ysgu_ref, qk_ref, gv_ref, gate_ref, gr_ref, q_ref, k_ref, vt_ref = rest[-8:]
    tm = x_ref.shape[0]
    h = _rms(x_ref[...], n1w_ref[...]) * (1.0 + sc_ref[...]) + sh_ref[...]
    hb = h.astype(BF16)

    def proj(lo, hi):
        return _dot(hb, win_ref[:, lo:hi])

    p2 = proj(C_SU_SV, C_CKV_DQ)
    p0 = proj(C_QK_ROPE, C_GV_GR)
    qk_ref[:, 0:GLA_KW] = p0[:, 0:GLA_KW] * (GLA_DK ** -0.5)
    qk_ref[:, GLA_KW:] = p0[:, GLA_KW:2 * GLA_KW]
    ga, gb = p0[:, 2 * GLA_KW:2 * GLA_KW + LANES], p0[:, 2 * GLA_KW + LANES:]
    p1 = proj(C_GV_GR, C_SU_SV)
    gv_ref[...] = p1[:, 0:GLA_VW]
    gr_ref[...] = p1[:, GLA_VW:]

    z = _dot(ga.astype(BF16), wg_ref[...]) + bg_ref[...]
    log_sig = jnp.minimum(z, 0.0) - jnp.log1p(jnp.exp(-jnp.abs(z)))
    gate_ref[...] = log_sig * (1.0 / GLA_TAU)
    cos = cos_ref[...]
    sin = sin_ref[...]
    k_rope = (ga * cos + gb * sin).astype(BF16)

    p3 = proj(C_CKV_DQ, C_END)
    ckv = _rms(p3[:, 0:MLA_RANK], kvnw_ref[...]).astype(BF16)
    kv = _dot(ckv, wukv_ref[...])
    ones_row = (lax.broadcasted_iota(jnp.int32, (MLA_VROWS - MLA_V, tm), 0) == 0).astype(BF16)
    for hh in range(MLA_HEADS):
        k_ref[hh, :, 0:MLA_NOPE] = kv[:, hh * MLA_NOPE:(hh + 1) * MLA_NOPE].astype(BF16)
        k_ref[hh, :, MLA_NOPE:] = k_rope
        off = MLA_HEADS * MLA_NOPE + hh * MLA_V
        vt_ref[hh, 0:MLA_V, :] = kv[:, off:off + MLA_V].T.astype(BF16)
        vt_ref[hh, MLA_V:, :] = ones_row

    cq = _rms(p3[:, MLA_RANK:], qnw_ref[...]).astype(BF16)
    qq = _dot(cq, wuq_ref[...])
    nq = MLA_HEADS * LANES
    for hh in range(MLA_HEADS):
        sl = slice(hh * LANES, (hh + 1) * LANES)
        q_ref[hh, :, 0:MLA_NOPE] = (qq[:, sl] * q_scale).astype(BF16)
        rot = qq[:, nq + hh * LANES:nq + (hh + 1) * LANES] * cos + qq[:, 2 * nq + hh * LANES:2 * nq + (hh + 1) * LANES] * sin
        q_ref[hh, :, MLA_NOPE:] = (rot * q_scale).astype(BF16)

    u = _gelu_tanh(p2[:, 0:SGU_WIDTH])
    g = _gelu_tanh(p2[:, SGU_WIDTH:])
    mu = jnp.mean(g, axis=-1, keepdims=True)
    gc = g - mu
    var = jnp.mean(gc * gc, axis=-1, keepdims=True)
    vln = (gc * lax.rsqrt(var + EPS) * sgw_ref[...] + sgb_ref[...]).astype(BF16)
    head_of_lane = lax.broadcasted_iota(jnp.int32, (SGU_CHUNK, SGU_WIDTH), 1) // SGU_HEAD_DIM
    ws = ws_ref[...]
    for c in range(0, tm // SGU_CHUNK, 2):
        r2 = _dot(ws, jnp.concatenate([vln[c * SGU_CHUNK:(c + 1) * SGU_CHUNK, :],
                                       vln[(c + 1) * SGU_CHUNK:(c + 2) * SGU_CHUNK, :]], axis=1))
        for half in range(2):
            rows = slice((c + half) * SGU_CHUNK, (c + half + 1) * SGU_CHUNK)
            r = r2[:, half * SGU_WIDTH:(half + 1) * SGU_WIDTH]
            s = r[0:SGU_CHUNK]
            for hh in range(1, SGU_HEADS):
                s = jnp.where(head_of_lane == hh, r[hh * SGU_CHUNK:(hh + 1) * SGU_CHUNK], s)
            ysgu_ref[rows, :] = (u[rows, :] * (s + bs_ref[...])).astype(BF16)


def _proj_call(x, sh, sc, lw, cos, sin, tm, nk_total, key_off, kv_prev=None):
    bsz, n, d = x.shape
    nh = MLA_HEADS
    kb = key_off // tm
    q_scale = (MLA_NOPE + MLA_ROPE) ** -0.5 * math.log2(math.e)
    tok = lambda w: pl.BlockSpec((None, tm, w), lambda b, i: (b, i, 0))
    vec = pl.BlockSpec((None, 1, d), lambda b, i: (b, 0, 0))
    heads = lambda w: pl.BlockSpec((None, nh, tm, w), lambda b, i: (b, 0, i, 0))
    tab = pl.BlockSpec((tm, LANES), lambda b, i: (i, 0))
    k_spec = pl.BlockSpec((None, nh, tm, MLA_QK), lambda b, i: (b, 0, kb + i, 0))
    vt_spec = pl.BlockSpec((None, nh, MLA_VROWS, tm), lambda b, i: (b, 0, 0, kb + i))
    in_specs = [tok(d), vec, vec, _full((1, d)), _full((d, C_END)),
                _full((1, SGU_WIDTH)), _full((1, SGU_WIDTH)), _full((SGU_HEADS * SGU_CHUNK, SGU_CHUNK)),
                _full((SGU_CHUNK, SGU_WIDTH)), _full((LANES, 2 * GLA_KW)), _full((1, 2 * GLA_KW)),
                _full((1, MLA_RANK)), _full((MLA_RANK, 3 * nh * LANES)),
                _full((1, MLA_RANK)), _full((MLA_RANK, nh * (MLA_NOPE + MLA_V))), tab, tab]
    out_specs = [tok(SGU_WIDTH), tok(2 * GLA_KW), tok(GLA_VW), tok(2 * GLA_KW), tok(GLA_VW),
                 heads(MLA_QK), k_spec, vt_spec]
    out_shape = [jax.ShapeDtypeStruct((bsz, n, SGU_WIDTH), BF16),
                 jax.ShapeDtypeStruct((bsz, n, 2 * GLA_KW), F32),
                 jax.ShapeDtypeStruct((bsz, n, GLA_VW), F32),
                 jax.ShapeDtypeStruct((bsz, n, 2 * GLA_KW), F32),
                 jax.ShapeDtypeStruct((bsz, n, GLA_VW), F32),
                 jax.ShapeDtypeStruct((bsz, nh, n, MLA_QK), BF16),
                 jax.ShapeDtypeStruct((bsz, nh, nk_total, MLA_QK), BF16),
                 jax.ShapeDtypeStruct((bsz, nh, MLA_VROWS, nk_total), BF16)]
    args = [x, sh, sc, lw["n1w"], lw["w_in"], lw["sgw"], lw["sgb"], lw["ws"], lw["bs"], lw["wg"], lw["bg"],
            lw["qnw"], lw["wuq"], lw["kvnw"], lw["wukv"], cos, sin]
    aliases = {}
    if kv_prev is not None:
        aliases = {len(args): 6, len(args) + 1: 7}
        in_specs = in_specs + [pl.BlockSpec(memory_space=pl.ANY)] * 2
        args = args + list(kv_prev)
    return pl.pallas_call(
        functools.partial(_proj_kernel, q_scale=q_scale),
        grid=(bsz, n // tm), in_specs=in_specs, out_specs=out_specs, out_shape=out_shape,
        input_output_aliases=aliases,
        compiler_params=_cparams(("parallel", "parallel")), name="proj",
    )(*args)


def _split3(x):
    hi = x.astype(BF16)
    r1 = x - hi.astype(F32)
    mid = r1.astype(BF16)
    lo = (r1 - mid.astype(F32)).astype(BF16)
    return hi, mid, lo


def _row_bcast(ref, r, n):
    return ref[pl.ds(r, n, stride=0), :]


def _gla_block(q_ref, k_ref, v0_ref, v1_ref, b_ref, st_ref, o_ref, r0, fwd, ind, mask_t):
    nb = GLA_BLOCK
    rows = pl.ds(r0, nb)
    bcast = lambda ref, r: _row_bcast(ref, r, nb)
    q = q_ref[rows, :]
    k = k_ref[rows, :]
    v = jnp.concatenate([v0_ref[rows, :], v1_ref[rows, :]], axis=1)
    b = b_ref[rows, :]
    b_end = bcast(b_ref, r0 + (nb - 1) if fwd else r0)
    st = st_ref[...]
    o = _dot_nt((q * jnp.exp(b)).astype(BF16), st.astype(BF16))
    kh = (k * jnp.exp(b_end - b)).astype(BF16)
    st_ref[...] = st * jnp.exp(b_end[0:1, :]) + mask_t * _dot_tn(v.astype(BF16), kh)

    row = lax.broadcasted_iota(jnp.int32, (nb, GLA_KW), 0)
    parts = []
    for j in range(nb):
        valid = (row >= j) if fwd else (row <= j)
        e = jnp.exp(jnp.where(valid, b - bcast(b_ref, r0 + j), -jnp.inf))
        parts.append((e * q * bcast(k_ref, r0 + j)).astype(BF16))
    r = _dot(jnp.concatenate(parts, axis=0), ind)
    for j in range(nb):
        vj = jnp.concatenate([bcast(v0_ref, r0 + j), bcast(v1_ref, r0 + j)], axis=1)
        o = o + r[j * nb:(j + 1) * nb] * vj
    o_ref[rows, :] = o


def _chunk_prep(q_ref, k_ref, b_ref, a_sc, qt_sc, kh_sc, d, slot, r0, fwd, qmask, tri):
    nc = GLA_CHUNK
    rows = pl.ds(r0, nc)
    b = b_ref[rows, :]
    e_end = jnp.exp(b_ref[pl.ds(r0 + (nc - 1) if fwd else r0, 1), :])
    qt = q_ref[rows, :] * jnp.exp(b)
    kt = k_ref[rows, :] * jnp.exp(-b)
    q4 = (jnp.concatenate([qt] * GLA_HEADS, axis=0) * qmask).astype(BF16)
    a_sc[d, slot] = jnp.where(tri, _dot_nt(q4, kt.astype(BF16)), 0.0).astype(BF16)
    qt_sc[d, slot] = qt.astype(BF16)
    kh_sc[d, slot] = (kt * e_end).astype(BF16)


def _chunk_apply(v_ref, b_ref, st_ref, o_ref, a_sc, qt_sc, kh_sc, d, slot, r0, fwd, mask_t):
    nc = GLA_CHUNK
    rows = pl.ds(r0, nc)
    vb = v_ref[rows, :].astype(BF16)
    e_end = jnp.exp(b_ref[pl.ds(r0 + (nc - 1) if fwd else r0, 1), :])
    st = st_ref[...]
    o = _dot_nt(qt_sc[d, slot], st.astype(BF16))
    st_ref[...] = st * e_end + mask_t * _dot_tn(vb, kh_sc[d, slot])
    r = _dot(a_sc[d, slot], vb)
    head_of_lane = lax.broadcasted_iota(jnp.int32, (nc, GLA_VW), 1) // GLA_DV
    intra = r[0:nc]
    for h in range(1, GLA_HEADS):
        intra = jnp.where(head_of_lane == h, r[h * nc:(h + 1) * nc], intra)
    o_ref[rows, :] = o + intra


def _cumsum_blocks(g_ref, tri_ref, b_sc):
    tri = tri_ref[...]
    span = tri.shape[0]
    for c in range(g_ref.shape[0] // span):
        rows = slice(c * span, (c + 1) * span)
        hi, mid, lo = _split3(g_ref[rows, :])
        b_sc[rows, :] = _dot(tri, hi) + _dot(tri, mid) + _dot(tri, lo)


def _gla_kernel(qf_ref, kf_ref, v0f_ref, v1f_ref, vf_ref, gf_ref, qb_ref, kb_ref, v0b_ref, v1b_ref, vb_ref, gb_ref,
                s0_ref, t16f_ref, t16b_ref, t64f_ref, t64b_ref, ind_ref, maskt_ref, qmask_ref,
                of_ref, ob_ref, sfin_ref, stf, stb, bf, bb, a_sc, qt_sc, kh_sc):
    t = pl.program_id(1)
    tt = qf_ref.shape[0]

    @pl.when(t == 0)
    def _():
        stf[...] = s0_ref[0]
        stb[...] = s0_ref[1]

    _cumsum_blocks(gf_ref, t64f_ref, bf)
    _cumsum_blocks(gb_ref, t64b_ref, bb)
    lowest = jnp.minimum(jnp.min(bf[...]), jnp.min(bb[...]))
    mask_t = maskt_ref[...]

    @pl.when(lowest > -GLA_SAFE_DECAY)
    def _():
        nch = tt // GLA_CHUNK
        qmask = qmask_ref[...]
        row = lax.broadcasted_iota(jnp.int32, (GLA_HEADS * GLA_CHUNK, GLA_CHUNK), 0) % GLA_CHUNK
        col = lax.broadcasted_iota(jnp.int32, (GLA_HEADS * GLA_CHUNK, GLA_CHUNK), 1)
        lower, upper = col <= row, col >= row

        off = lambda c: pl.multiple_of(c * GLA_CHUNK, GLA_CHUNK)

        def prep(c_f, c_b, slot):
            _chunk_prep(qf_ref, kf_ref, bf, a_sc, qt_sc, kh_sc, 0, slot, off(c_f), True, qmask, lower)
            _chunk_prep(qb_ref, kb_ref, bb, a_sc, qt_sc, kh_sc, 1, slot, off(c_b), False, qmask, upper)

        prep(0, nch - 1, 0)

        def step(i, carry):
            slot = i % 2
            nxt = jnp.minimum(i + 1, nch - 1)
            prep(nxt, nch - 1 - nxt, 1 - slot)
            _chunk_apply(vf_ref, bf, stf, of_ref, a_sc, qt_sc, kh_sc, 0, slot, off(i), True, mask_t)
            _chunk_apply(vb_ref, bb, stb, ob_ref, a_sc, qt_sc, kh_sc, 1, slot, off(nch - 1 - i), False, mask_t)
            return carry

        lax.fori_loop(0, nch, step, 0, unroll=4)

    @pl.when(jnp.logical_not(lowest > -GLA_SAFE_DECAY))
    def _():
        nblk = tt // GLA_BLOCK
        _cumsum_blocks(gf_ref, t16f_ref, bf)
        _cumsum_blocks(gb_ref, t16b_ref, bb)
        ind = ind_ref[...]

        def step(i, carry):
            rf = pl.multiple_of(i * GLA_BLOCK, GLA_BLOCK)
            rb = pl.multiple_of((nblk - 1 - i) * GLA_BLOCK, GLA_BLOCK)
            _gla_block(qf_ref, kf_ref, v0f_ref, v1f_ref, bf, stf, of_ref, rf, True, ind, mask_t)
            _gla_block(qb_ref, kb_ref, v0b_ref, v1b_ref, bb, stb, ob_ref, rb, False, ind, mask_t)
            return carry

        lax.fori_loop(0, nblk, step, 0)

    @pl.when(t == pl.num_programs(1) - 1)
    def _():
        sfin_ref[0] = stf[...]
        sfin_ref[1] = stb[...]


def _tri(span, blk):
    i = np.arange(span)
    same = (i[:, None] // blk) == (i[None, :] // blk)
    lower = (same & (i[None, :] <= i[:, None])).astype(np.float32)
    upper = (same & (i[None, :] >= i[:, None])).astype(np.float32)
    return jnp.asarray(lower, BF16), jnp.asarray(upper, BF16)


def _gla_call(qk, gv, gates, s0, tt):
    bsz, n, _ = qk.shape
    nt = n // tt
    span = min(tt, 256)
    t16f, t16b = _tri(span, GLA_BLOCK)
    t64f, t64b = _tri(span, GLA_CHUNK)
    kd = np.arange(GLA_KW) // GLA_DK
    ve = np.arange(GLA_VW) // GLA_DV
    ind = (kd[:, None] == ve[None, :]).astype(np.float32)
    qmask = (np.repeat(np.arange(GLA_HEADS), GLA_CHUNK)[:, None] == kd[None, :]).astype(np.float32)
    half = lambda rev, c: pl.BlockSpec((None, tt, LANES), lambda b, t: (b, nt - 1 - t if rev else t, c))
    full = lambda rev: pl.BlockSpec((None, tt, GLA_VW), lambda b, t: (b, nt - 1 - t if rev else t, 0))
    st = pl.BlockSpec((None, 2, GLA_VW, GLA_KW), lambda b, t: (b, 0, 0, 0))
    fwd_in = [half(False, 0), half(False, 1), half(False, 0), half(False, 1), full(False), half(False, 0)]
    bwd_in = [half(True, 0), half(True, 1), half(True, 0), half(True, 1), full(True), half(True, 1)]
    return pl.pallas_call(
        _gla_kernel,
        grid=(bsz, nt),
        in_specs=fwd_in + bwd_in + [st] + [_full((span, span))] * 4 + [
            _full((GLA_KW, GLA_VW)), _full((GLA_VW, GLA_KW)), _full((GLA_HEADS * GLA_CHUNK, GLA_KW))],
        out_specs=[full(False), full(True), st],
        out_shape=[jax.ShapeDtypeStruct((bsz, n, GLA_VW), F32), jax.ShapeDtypeStruct((bsz, n, GLA_VW), F32),
                   jax.ShapeDtypeStruct((bsz, 2, GLA_VW, GLA_KW), F32)],
        scratch_shapes=[pltpu.VMEM((GLA_VW, GLA_KW), F32), pltpu.VMEM((GLA_VW, GLA_KW), F32),
                        pltpu.VMEM((tt, GLA_KW), F32), pltpu.VMEM((tt, GLA_KW), F32),
                        pltpu.VMEM((2, 2, GLA_HEADS * GLA_CHUNK, GLA_CHUNK), BF16),
                        pltpu.VMEM((2, 2, GLA_CHUNK, GLA_KW), BF16), pltpu.VMEM((2, 2, GLA_CHUNK, GLA_KW), BF16)],
        compiler_params=_cparams(("parallel", "arbitrary")), name="gla",
    )(qk, qk, gv, gv, gv, gates, qk, qk, gv, gv, gv, gates, s0, t16f, t16b, t64f, t64b,
      jnp.asarray(ind, BF16), jnp.asarray(ind.T, F32), jnp.asarray(qmask, F32))


def _attn_kernel(q_ref, qn_ref, k_ref, vt_ref, o_ref, s_sc, m_sc, acc_sc, qt_sc, *, tk):
    nk = k_ref.shape[0] // tk
    i = pl.program_id(2)
    carry_scores = nk % 2 == 0
    par = i % 2 if carry_scores else 0

    def scores(j, slot, qslot):
        r0 = pl.multiple_of(j * tk, tk)
        s_sc[slot] = _dot(k_ref[pl.ds(r0, tk), :], qt_sc[qslot])

    def first_scores():
        qt_sc[0] = q_ref[...].T
        scores(0, 0, 0)

    if carry_scores:
        pl.when(i == 0)(first_scores)
    else:
        first_scores()
    m_sc[...] = jnp.full(m_sc.shape, -jnp.inf, F32)
    acc_sc[...] = jnp.zeros(acc_sc.shape, F32)

    def update(j, slot):
        s = s_sc[slot]
        m_prev = m_sc[...]
        m_new = jnp.maximum(m_prev, jnp.max(s, axis=0, keepdims=True))
        alpha = jnp.exp2(m_prev - m_new)
        p = jnp.exp2(s - m_new).astype(BF16)
        r0 = pl.multiple_of(j * tk, tk)
        acc_sc[...] = alpha * acc_sc[...] + _dot(vt_ref[:, pl.ds(r0, tk)], p)
        m_sc[...] = m_new

    def pair(t, carry):
        j = 2 * t
        scores(j + 1, 1, par)
        update(j, 0)
        scores(j + 2, 0, par)
        update(j + 1, 1)
        return carry

    lax.fori_loop(0, (nk - 1) // 2, pair, 0)
    if carry_scores:
        scores(nk - 1, 1, par)
        update(nk - 2, 0)
        qt_sc[1 - par] = qn_ref[...].T
        scores(0, 0, 1 - par)
        update(nk - 1, 1)
    else:
        update(nk - 1, 0)
    acc = acc_sc[...]
    o_t = acc[0:MLA_V] * (1.0 / acc[MLA_V:MLA_V + 1])
    o_ref[...] = o_t.T.astype(o_ref.dtype)


def _attn_call(q, k, vt, tq, tk, key_off, nk):
    bsz, nh, nq, dq = q.shape
    kb = key_off // nk
    nqt = nq // tq
    return pl.pallas_call(
        functools.partial(_attn_kernel, tk=tk),
        grid=(bsz, nh, nqt),
        in_specs=[pl.BlockSpec((None, None, tq, dq), lambda b, h, i: (b, h, i, 0)),
                  pl.BlockSpec((None, None, tq, dq), lambda b, h, i: (b, h, jnp.minimum(i + 1, nqt - 1), 0)),
                  pl.BlockSpec((None, None, nk, dq), lambda b, h, i: (b, h, kb, 0)),
                  pl.BlockSpec((None, None, MLA_VROWS, nk), lambda b, h, i: (b, h, 0, kb))],
        out_specs=pl.BlockSpec((None, None, tq, MLA_V), lambda b, h, i: (b, h, i, 0)),
        out_shape=jax.ShapeDtypeStruct((bsz, nh, nq, MLA_V), BF16),
        scratch_shapes=[pltpu.VMEM((2, tk, tq), F32), pltpu.VMEM((1, tq), F32), pltpu.VMEM((MLA_VROWS, tq), F32),
                        pltpu.VMEM((2, dq, tq), BF16)],
        compiler_params=_cparams(("parallel", "parallel", "arbitrary")), name="attn",
    )(q, q, k, vt)


def _out_kernel(x_ref, ysgu_ref, of_ref, ob_ref, gr_ref, ymla_ref, g1_ref, sh2_ref, sc2_ref, g2_ref, glaw_ref,
                hsum_ref, wout_ref, n2w_ref, w1_ref, w2_ref, fnw_ref, o_ref, y_sc, *, final_norm):
    o = of_ref[...] + ob_ref[...]
    o2 = o * o
    hi = o2.astype(BF16)
    lo = (o2 - hi.astype(F32)).astype(BF16)
    ms = (_dot(hi, hsum_ref[...]) + _dot(lo, hsum_ref[...])) * (1.0 / GLA_DV)
    y_gla = o * lax.rsqrt(ms + EPS) * glaw_ref[...] * _silu(gr_ref[...])
    y_sc[:, 0:SGU_WIDTH] = ysgu_ref[...]
    y_sc[:, SGU_WIDTH:SGU_WIDTH + GLA_VW] = y_gla.astype(BF16)
    for hh in range(MLA_HEADS):
        off = SGU_WIDTH + GLA_VW + hh * MLA_V
        y_sc[:, off:off + MLA_V] = ymla_ref[hh]
    x1 = x_ref[...] + g1_ref[...] * _dot(y_sc[...], wout_ref[...])
    hn = (_rms(x1, n2w_ref[...]) * (1.0 + sc2_ref[...]) + sh2_ref[...]).astype(BF16)
    dff = w1_ref.shape[1]
    fc = 1024
    acc = jnp.zeros(x1.shape, F32)
    for c in range(dff // fc):
        a = jnp.maximum(_dot(hn, w1_ref[:, c * fc:(c + 1) * fc]), 0.0)
        acc = acc + _dot((a * a).astype(BF16), w2_ref[c * fc:(c + 1) * fc, :])
    x2 = x1 + g2_ref[...] * acc
    if final_norm:
        x2 = _rms(x2, fnw_ref[...])
    o_ref[...] = x2


def _out_call(x, ysgu, o_f, o_b, gr, ymla, g1, sh2, sc2, g2, lw, fnw, tm, final_norm):
    bsz, n, d = x.shape
    nh = MLA_HEADS
    mix = SGU_WIDTH + GLA_VW + nh * MLA_V
    dff = lw["w1"].shape[1]
    tok = lambda w: pl.BlockSpec((None, tm, w), lambda b, i: (b, i, 0))
    vec = pl.BlockSpec((None, 1, d), lambda b, i: (b, 0, 0))
    once = lambda shape: pl.BlockSpec(shape, lambda *_: (0,) * len(shape), pipeline_mode=pl.Buffered(1))
    in_specs = [tok(d), tok(SGU_WIDTH), tok(GLA_VW), tok(GLA_VW), tok(GLA_VW),
                pl.BlockSpec((None, nh, tm, MLA_V), lambda b, i: (b, 0, i, 0)),
                vec, vec, vec, vec, _full((1, GLA_VW)), _full((GLA_VW, GLA_VW)),
                once((mix, d)), _full((1, d)), once((d, dff)), once((dff, d)), _full((1, d))]
    return pl.pallas_call(
        functools.partial(_out_kernel, final_norm=final_norm),
        grid=(bsz, n // tm), in_specs=in_specs, out_specs=tok(d),
        out_shape=jax.ShapeDtypeStruct((bsz, n, d), F32),
        scratch_shapes=[pltpu.VMEM((tm, mix), BF16)],
        compiler_params=_cparams(("parallel", "parallel")), name="out_ffn",
    )(x, ysgu, o_f, o_b, gr, ymla, g1, sh2, sc2, g2, lw["glaw"], lw["hsum"], lw["w_out"], lw["n2w"],
      lw["w1"], lw["w2"], fnw)


def _rope_tables(n):
    half = MLA_ROPE // 2
    pos = np.arange(n)
    freq = ROPE_BASE ** (-np.arange(half // 2, dtype=np.float64) * 2.0 / half)
    ang_r = (pos // GRID_W)[:, None] * freq[None, :]
    ang_c = (pos % GRID_W)[:, None] * freq[None, :]
    cos = np.zeros((n, LANES), np.float32)
    sin = np.zeros((n, LANES), np.float32)
    cos[:, 0:64] = np.concatenate([np.cos(ang_r), np.cos(ang_r), np.cos(ang_c), np.cos(ang_c)], axis=1)
    sin[:, 0:64] = np.concatenate([-np.sin(ang_r), np.sin(ang_r), -np.sin(ang_c), np.sin(ang_c)], axis=1)
    return jnp.asarray(cos), jnp.asarray(sin)


def _identity_tables(n):
    cos = np.zeros((n, LANES), np.float32)
    cos[:, 0:64] = 1.0
    return jnp.asarray(cos), jnp.zeros((n, LANES), F32)


_ROT_PARTNER = np.concatenate([np.arange(16, 32), np.arange(0, 16), np.arange(48, 64), np.arange(32, 48)])


def _layer_weights(l, p):
    w_in = p["w_in"][l]
    d = w_in.shape[0]
    o_gk, o_gv, o_gf, o_gb, o_ckv, o_kr, o_su, o_sv, o_gq, o_gr, o_dq = (
        0, 128, 384, 400, 416, 672, 736, 992, 1248, 1376, 1632)
    col = lambda o, w: w_in[:, o:o + w]
    kr = col(o_kr, MLA_ROPE)
    z = lambda w: jnp.zeros((d, w), w_in.dtype)
    w_in_p = jnp.concatenate([
        col(o_gq, 128), col(o_gk, 128),
        kr, col(o_gf, GLA_RANK), col(o_gb, GLA_RANK), z(32), kr[:, _ROT_PARTNER], z(64),
        col(o_gv, 256), col(o_gr, 256), col(o_su, 256), col(o_sv, 256),
        col(o_ckv, 256), col(o_dq, 256)], axis=1).astype(BF16)

    wg = jnp.zeros((LANES, 2 * GLA_KW), F32)
    wg = wg.at[64:80, 0:GLA_KW].set(p["gla_wg_fwd"][l]).at[80:96, GLA_KW:].set(p["gla_wg_bwd"][l])
    bg = jnp.concatenate([p["gla_bg_fwd"][l], p["gla_bg_bwd"][l]])[None, :]

    w_uq = p["mla_w_uq"][l].reshape(MLA_RANK, MLA_HEADS, MLA_NOPE + MLA_ROPE)
    nope = w_uq[:, :, :MLA_NOPE].reshape(MLA_RANK, -1)
    rope = w_uq[:, :, MLA_NOPE:]
    pad = lambda t: jnp.pad(t, ((0, 0), (0, 0), (0, LANES - MLA_ROPE))).reshape(MLA_RANK, -1)
    wuq = jnp.concatenate([nope, pad(rope), pad(rope[:, :, _ROT_PARTNER])], axis=1).astype(BF16)

    w_ukv = p["mla_w_ukv"][l].reshape(MLA_RANK, MLA_HEADS, MLA_NOPE + MLA_V)
    wukv = jnp.concatenate([w_ukv[:, :, :MLA_NOPE].reshape(MLA_RANK, -1),
                            w_ukv[:, :, MLA_NOPE:].reshape(MLA_RANK, -1)], axis=1).astype(BF16)

    hv = np.arange(GLA_VW) // GLA_DV
    return dict(
        n1w=p["norm1_w"][l][None, :], w_in=w_in_p,
        sgw=p["sgu_norm_w"][l][None, :], sgb=p["sgu_norm_b"][l][None, :],
        ws=p["sgu_w"][l].reshape(SGU_HEADS * SGU_CHUNK, SGU_CHUNK).astype(BF16),
        bs=jnp.repeat(p["sgu_b"][l].T, SGU_HEAD_DIM, axis=1),
        wg=wg.astype(BF16), bg=bg,
        qnw=p["mla_q_norm_w"][l][None, :], wuq=wuq, kvnw=p["mla_kv_norm_w"][l][None, :], wukv=wukv,
        glaw=jnp.tile(p["gla_norm_w"][l], GLA_HEADS)[None, :],
        hsum=jnp.asarray((hv[:, None] == hv[None, :]).astype(np.float32), BF16),
        w_out=p["w_out"][l].astype(BF16), n2w=p["norm2_w"][l][None, :],
        w1=p["w_ff1"][l].astype(BF16), w2=p["w_ff2"][l].astype(BF16))


def _pick(n, pref):
    for t in pref:
        if n % t == 0:
            return t
    return n


class _Tiles(NamedTuple):
    tok: int
    ffn: int
    ctx: int
    q: int
    kv: int


def _tiles(n, nc):
    return _Tiles(tok=_pick(n, (1024, 512, 256, 128)), ffn=_pick(n, (512, 256, 128)), ctx=_pick(nc, (256, 128)),
                  q=_pick(n, (1024, 512, 256, 128)), kv=_pick(n + nc, (1408, 768, 512, 256, 128)))


def kernel(x, c, ctx, c_ctx, w_mod, b_mod, norm1_w, w_in, w_out, sgu_norm_w, sgu_norm_b, sgu_w, sgu_b, gla_wg_fwd, gla_bg_fwd, gla_wg_bwd, gla_bg_bwd, gla_norm_w, mla_q_norm_w, mla_w_uq, mla_kv_norm_w, mla_w_ukv, norm2_w, w_ff1, w_ff2, final_norm_w):
    p = dict(norm1_w=norm1_w, w_in=w_in, w_out=w_out, sgu_norm_w=sgu_norm_w, sgu_norm_b=sgu_norm_b, sgu_w=sgu_w,
             sgu_b=sgu_b, gla_wg_fwd=gla_wg_fwd, gla_bg_fwd=gla_bg_fwd, gla_wg_bwd=gla_wg_bwd, gla_bg_bwd=gla_bg_bwd,
             gla_norm_w=gla_norm_w, mla_q_norm_w=mla_q_norm_w, mla_w_uq=mla_w_uq, mla_kv_norm_w=mla_kv_norm_w,
             mla_w_ukv=mla_w_ukv, norm2_w=norm2_w, w_ff1=w_ff1, w_ff2=w_ff2)
    bsz, n, d = x.shape
    nc = ctx.shape[1]
    depth = w_mod.shape[0]
    fnw = final_norm_w[None, :]

    cvecs = jnp.zeros((8, d), F32).at[0:bsz].set(c).at[bsz].set(c_ctx)
    mod = _mod_call(cvecs, w_mod, b_mod)

    t = _tiles(n, nc)
    cos, sin = _rope_tables(n)
    cos_c, sin_c = _identity_tables(nc)

    xc = ctx
    for l in range(depth):
        last = l == depth - 1
        lw = _layer_weights(l, p)
        m = mod[l, 0:bsz].reshape(bsz, 1, 6, d)
        sh1, sc1, g1, sh2, sc2, g2 = [m[:, :, i] for i in range(6)]
        mc = jnp.broadcast_to(mod[l, bsz].reshape(1, 1, 6, d), (bsz, 1, 6, d))
        sh1c, sc1c, g1c, sh2c, sc2c, g2c = [mc[:, :, i] for i in range(6)]

        ysgu, qk, gv, gate, gr, q, k, vt = _proj_call(x, sh1, sc1, lw, cos, sin, t.tok, n + nc, 0)
        ysgu_c, qk_c, gv_c, gate_c, gr_c, q_c, k, vt = _proj_call(xc, sh1c, sc1c, lw, cos_c, sin_c, t.ctx, n + nc, n,
                                                                   kv_prev=(k, vt))
        zero = jnp.zeros((bsz, 2, GLA_VW, GLA_KW), F32)
        of_c, ob_c, s_ctx = _gla_call(qk_c, gv_c, gate_c, zero, t.ctx)
        o_f, o_b, _ = _gla_call(qk, gv, gate, s_ctx, t.tok)
        ymla = _attn_call(q, k, vt, t.q, t.kv, 0, n + nc)
        x = _out_call(x, ysgu, o_f, o_b, gr, ymla, g1, sh2, sc2, g2, lw, fnw, t.ffn, last)

        if not last:
            ymla_c = _attn_call(q_c, k, vt, t.ctx, t.ctx, n, nc)
            xc = _out_call(xc, ysgu_c, of_c, ob_c, gr_c, ymla_c, g1c, sh2c, sc2c, g2c, lw, fnw, t.ctx, False)
    return x
```

```python
import functools
import math
from typing import NamedTuple

import numpy as np
import jax
import jax.numpy as jnp
from jax import lax
from jax.experimental import pallas as pl
from jax.experimental.pallas import tpu as pltpu

F32 = jnp.float32
BF16 = jnp.bfloat16

EPS = 1e-6
GRID_W = 64
ROPE_BASE = 10000.0
LANES = 128
SGU_HEADS = 4
SGU_HEAD_DIM = 64
SGU_WIDTH = 256
SGU_CHUNK = 128
GLA_HEADS = 4
GLA_DK = 32
GLA_DV = 64
GLA_KW = 128
GLA_VW = 256
GLA_RANK = 16
GLA_TAU = 16.0
GLA_BLOCK = 16
GLA_CHUNK = 64
GLA_SAFE_DECAY = 60.0
MLA_HEADS = 4
MLA_RANK = 256
MLA_NOPE = 128
MLA_ROPE = 64
MLA_V = 128
MLA_QK = 256
MLA_VROWS = 144
VMEM_LIMIT = 56 * 1024 * 1024

C_QK_ROPE, C_GV_GR, C_SU_SV, C_CKV_DQ, C_END = 0, 512, 1024, 1536, 2048


def _cparams(sem):
    return pltpu.CompilerParams(dimension_semantics=sem, vmem_limit_bytes=VMEM_LIMIT)


def _dot(a, b):
    return jnp.dot(a, b, preferred_element_type=F32)


def _dot_nt(a, b):
    return lax.dot_general(a, b, (((1,), (1,)), ((), ())), preferred_element_type=F32)


def _dot_tn(a, b):
    return lax.dot_general(a, b, (((0,), (0,)), ((), ())), preferred_element_type=F32)


def _rms(x, w):
    return x * lax.rsqrt(jnp.mean(x * x, axis=-1, keepdims=True) + EPS) * w


def _gelu_tanh(x):
    return 0.5 * x * (1.0 + jnp.tanh(math.sqrt(2.0 / math.pi) * (x + 0.044715 * (x * x * x))))


def _silu(x):
    return x * jax.nn.sigmoid(x)


def _full(shape):
    return pl.BlockSpec(shape, lambda *_: (0,) * len(shape))


def _mod_kernel(c_ref, w_ref, b_ref, o_ref):
    s = _silu(c_ref[...]).astype(BF16)
    o_ref[...] = _dot(s, w_ref[...].astype(BF16)) + b_ref[...]


def _mod_call(cvecs, w_mod, b_mod):
    nl, d, d6 = w_mod.shape
    tn = 1536
    return pl.pallas_call(
        _mod_kernel,
        grid=(nl, d6 // tn),
        in_specs=[pl.BlockSpec((8, d), lambda l, j: (0, 0)),
                  pl.BlockSpec((None, d, tn), lambda l, j: (l, 0, j)),
                  pl.BlockSpec((None, 1, tn), lambda l, j: (l, 0, j))],
        out_specs=pl.BlockSpec((None, 8, tn), lambda l, j: (l, 0, j)),
        out_shape=jax.ShapeDtypeStruct((nl, 8, d6), F32),
        compiler_params=_cparams(("parallel", "parallel")),
        name="mod",
    )(cvecs, w_mod, b_mod.reshape(nl, 1, d6))


def _proj_kernel(x_ref, sh_ref, sc_ref, n1w_ref, win_ref, sgw_ref, sgb_ref, ws_ref, bs_ref, wg_ref, bg_ref,
                 qnw_ref, wuq_ref, kvnw_ref, wukv_ref, cos_ref, sin_ref, *rest, q_scale):
    ysgu_ref, qk_ref, gv_ref, gate_ref, gr_ref, q_ref, k_ref, vt_ref = rest[-8:]
    tm = x_ref.shape[0]
    h = _rms(x_ref[...], n1w_ref[...]) * (1.0 + sc_ref[...]) + sh_ref[...]
    hb = h.astype(BF16)

    def proj(lo, hi):
        return _dot(hb, win_ref[:, lo:hi])

    p2 = proj(C_SU_SV, C_CKV_DQ)
    p0 = proj(C_QK_ROPE, C_GV_GR)
    qk_ref[:, 0:GLA_KW] = p0[:, 0:GLA_KW] * (GLA_DK ** -0.5)
    qk_ref[:, GLA_KW:] = p0[:, GLA_KW:2 * GLA_KW]
    ga, gb = p0[:, 2 * GLA_KW:2 * GLA_KW + LANES], p0[:, 2 * GLA_KW + LANES:]
    p1 = proj(C_GV_GR, C_SU_SV)
    gv_ref[...] = p1[:, 0:GLA_VW]
    gr_ref[...] = p1[:, GLA_VW:]

    z = _dot(ga.astype(BF16), wg_ref[...]) + bg_ref[...]
    log_sig = jnp.minimum(z, 0.0) - jnp.log1p(jnp.exp(-jnp.abs(z)))
    gate_ref[...] = log_sig * (1.0 / GLA_TAU)
    cos = cos_ref[...]
    sin = sin_ref[...]
    k_rope = (ga * cos + gb * sin).astype(BF16)

    p3 = proj(C_CKV_DQ, C_END)
    ckv = _rms(p3[:, 0:MLA_RANK], kvnw_ref[...]).astype(BF16)
    kv = _dot(ckv, wukv_ref[...])
    ones_row = (lax.broadcasted_iota(jnp.int32, (MLA_VROWS - MLA_V, tm), 0) == 0).astype(BF16)
    for hh in range(MLA_HEADS):
        k_ref[hh, :, 0:MLA_NOPE] = kv[:, hh * MLA_NOPE:(hh + 1) * MLA_NOPE].astype(BF16)
        k_ref[hh, :, MLA_NOPE:] = k_rope
        off = MLA_HEADS * MLA_NOPE + hh * MLA_V
        vt_ref[hh, 0:MLA_V, :] = kv[:, off:off + MLA_V].T.astype(BF16)
        vt_ref[hh, MLA_V:, :] = ones_row

    cq = _rms(p3[:, MLA_RANK:], qnw_ref[...]).astype(BF16)
    qq = _dot(cq, wuq_ref[...])
    nq = MLA_HEADS * LANES
    for hh in range(MLA_HEADS):
        sl = slice(hh * LANES, (hh + 1) * LANES)
        q_ref[hh, :, 0:MLA_NOPE] = (qq[:, sl] * q_scale).astype(BF16)
        rot = qq[:, nq + hh * LANES:nq + (hh + 1) * LANES] * cos + qq[:, 2 * nq + hh * LANES:2 * nq + (hh + 1) * LANES] * sin
        q_ref[hh, :, MLA_NOPE:] = (rot * q_scale).astype(BF16)

    u = _gelu_tanh(p2[:, 0:SGU_WIDTH])
    g = _gelu_tanh(p2[:, SGU_WIDTH:])
    mu = jnp.mean(g, axis=-1, keepdims=True)
    gc = g - mu
    var = jnp.mean(gc * gc, axis=-1, keepdims=True)
    vln = (gc * lax.rsqrt(var + EPS) * sgw_ref[...] + sgb_ref[...]).astype(BF16)
    head_of_lane = lax.broadcasted_iota(jnp.int32, (SGU_CHUNK, SGU_WIDTH), 1) // SGU_HEAD_DIM
    ws = ws_ref[...]
    for c in range(0, tm // SGU_CHUNK, 2):
        r2 = _dot(ws, jnp.concatenate([vln[c * SGU_CHUNK:(c + 1) * SGU_CHUNK, :],
                                       vln[(c + 1) * SGU_CHUNK:(c + 2) * SGU_CHUNK, :]], axis=1))
        for half in range(2):
            rows = slice((c + half) * SGU_CHUNK, (c + half + 1) * SGU_CHUNK)
            r = r2[:, half * SGU_WIDTH:(half + 1) * SGU_WIDTH]
            s = r[0:SGU_CHUNK]
            for hh in range(1, SGU_HEADS):
                s = jnp.where(head_of_lane == hh, r[hh * SGU_CHUNK:(hh + 1) * SGU_CHUNK], s)
            ysgu_ref[rows, :] = (u[rows, :] * (s + bs_ref[...])).astype(BF16)


def _proj_call(x, sh, sc, lw, cos, sin, tm, nk_total, key_off, kv_prev=None):
    bsz, n, d = x.shape
    nh = MLA_HEADS
    kb = key_off // tm
    q_scale = (MLA_NOPE + MLA_ROPE) ** -0.5 * math.log2(math.e)
    tok = lambda w: pl.BlockSpec((None, tm, w), lambda b, i: (b, i, 0))
    vec = pl.BlockSpec((None, 1, d), lambda b, i: (b, 0, 0))
    heads = lambda w: pl.BlockSpec((None, nh, tm, w), lambda b, i: (b, 0, i, 0))
    tab = pl.BlockSpec((tm, LANES), lambda b, i: (i, 0))
    k_spec = pl.BlockSpec((None, nh, tm, MLA_QK), lambda b, i: (b, 0, kb + i, 0))
    vt_spec = pl.BlockSpec((None, nh, MLA_VROWS, tm), lambda b, i: (b, 0, 0, kb + i))
    in_specs = [tok(d), vec, vec, _full((1, d)), _full((d, C_END)),
                _full((1, SGU_WIDTH)), _full((1, SGU_WIDTH)), _full((SGU_HEADS * SGU_CHUNK, SGU_CHUNK)),
                _full((SGU_CHUNK, SGU_WIDTH)), _full((LANES, 2 * GLA_KW)), _full((1, 2 * GLA_KW)),
                _full((1, MLA_RANK)), _full((MLA_RANK, 3 * nh * LANES)),
                _full((1, MLA_RANK)), _full((MLA_RANK, nh * (MLA_NOPE + MLA_V))), tab, tab]
    out_specs = [tok(SGU_WIDTH), tok(2 * GLA_KW), tok(GLA_VW), tok(2 * GLA_KW), tok(GLA_VW),
                 heads(MLA_QK), k_spec, vt_spec]
    out_shape = [jax.ShapeDtypeStruct((bsz, n, SGU_WIDTH), BF16),
                 jax.ShapeDtypeStruct((bsz, n, 2 * GLA_KW), F32),
                 jax.ShapeDtypeStruct((bsz, n, GLA_VW), F32),
                 jax.ShapeDtypeStruct((bsz, n, 2 * GLA_KW), F32),
                 jax.ShapeDtypeStruct((bsz, n, GLA_VW), F32),
                 jax.ShapeDtypeStruct((bsz, nh, n, MLA_QK), BF16),
                 jax.ShapeDtypeStruct((bsz, nh, nk_total, MLA_QK), BF16),
                 jax.ShapeDtypeStruct((bsz, nh, MLA_VROWS, nk_total), BF16)]
    args = [x, sh, sc, lw["n1w"], lw["w_in"], lw["sgw"], lw["sgb"], lw["ws"], lw["bs"], lw["wg"], lw["bg"],
            lw["qnw"], lw["wuq"], lw["kvnw"], lw["wukv"], cos, sin]
    aliases = {}
    if kv_prev is not None:
        aliases = {len(args): 6, len(args) + 1: 7}
        in_specs = in_specs + [pl.BlockSpec(memory_space=pl.ANY)] * 2
        args = args + list(kv_prev)
    return pl.pallas_call(
        functools.partial(_proj_kernel, q_scale=q_scale),
        grid=(bsz, n // tm), in_specs=in_specs, out_specs=out_specs, out_shape=out_shape,
        input_output_aliases=aliases,
        compiler_params=_cparams(("parallel", "parallel")), name="proj",
    )(*args)


def _split3(x):
    hi = x.astype(BF16)
    r1 = x - hi.astype(F32)
    mid = r1.astype(BF16)
    lo = (r1 - mid.astype(F32)).astype(BF16)
    return hi, mid, lo


def _row_bcast(ref, r, n):
    return ref[pl.ds(r, n, stride=0), :]


def _gla_block(q_ref, k_ref, v0_ref, v1_ref, b_ref, st_ref, o_ref, r0, fwd, ind, mask_t):
    nb = GLA_BLOCK
    rows = pl.ds(r0, nb)
    bcast = lambda ref, r: _row_bcast(ref, r, nb)
    q = q_ref[rows, :]
    k = k_ref[rows, :]
    v = jnp.concatenate([v0_ref[rows, :], v1_ref[rows, :]], axis=1)
    b = b_ref[rows, :]
    b_end = bcast(b_ref, r0 + (nb - 1) if fwd else r0)
    st = st_ref[...]
    o = _dot_nt((q * jnp.exp(b)).astype(BF16), st.astype(BF16))
    kh = (k * jnp.exp(b_end - b)).astype(BF16)
    st_ref[...] = st * jnp.exp(b_end[0:1, :]) + mask_t * _dot_tn(v.astype(BF16), kh)

    row = lax.broadcasted_iota(jnp.int32, (nb, GLA_KW), 0)
    parts = []
    for j in range(nb):
        valid = (row >= j) if fwd else (row <= j)
        e = jnp.exp(jnp.where(valid, b - bcast(b_ref, r0 + j), -jnp.inf))
        parts.append((e * q * bcast(k_ref, r0 + j)).astype(BF16))
    r = _dot(jnp.concatenate(parts, axis=0), ind)
    for j in range(nb):
        vj = jnp.concatenate([bcast(v0_ref, r0 + j), bcast(v1_ref, r0 + j)], axis=1)
        o = o + r[j * nb:(j + 1) * nb] * vj
    o_ref[rows, :] = o


def _chunk_prep(q_ref, k_ref, b_ref, a_sc, qt_sc, kh_sc, d, slot, r0, fwd, qmask, tri):
    nc = GLA_CHUNK
    rows = pl.ds(r0, nc)
    b = b_ref[rows, :]
    e_end = jnp.exp(b_ref[pl.ds(r0 + (nc - 1) if fwd else r0, 1), :])
    qt = q_ref[rows, :] * jnp.exp(b)
    kt = k_ref[rows, :] * jnp.exp(-b)
    q4 = (jnp.concatenate([qt] * GLA_HEADS, axis=0) * qmask).astype(BF16)
    a_sc[d, slot] = jnp.where(tri, _dot_nt(q4, kt.astype(BF16)), 0.0).astype(BF16)
    qt_sc[d, slot] = qt.astype(BF16)
    kh_sc[d, slot] = (kt * e_end).astype(BF16)


def _chunk_apply(v_ref, b_ref, st_ref, o_ref, a_sc, qt_sc, kh_sc, d, slot, r0, fwd, mask_t):
    nc = GLA_CHUNK
    rows = pl.ds(r0, nc)
    vb = v_ref[rows, :].astype(BF16)
    e_end = jnp.exp(b_ref[pl.ds(r0 + (nc - 1) if fwd else r0, 1), :])
    st = st_ref[...]
    o = _dot_nt(qt_sc[d, slot], st.astype(BF16))
    st_ref[...] = st * e_end + mask_t * _dot_tn(vb, kh_sc[d, slot])
    r = _dot(a_sc[d, slot], vb)
    head_of_lane = lax.broadcasted_iota(jnp.int32, (nc, GLA_VW), 1) // GLA_DV
    intra = r[0:nc]
    for h in range(1, GLA_HEADS):
        intra = jnp.where(head_of_lane == h, r[h * nc:(h + 1) * nc], intra)
    o_ref[rows, :] = o + intra


def _cumsum_blocks(g_ref, tri_ref, b_sc):
    tri = tri_ref[...]
    span = tri.shape[0]
    for c in range(g_ref.shape[0] // span):
        rows = slice(c * span, (c + 1) * span)
        hi, mid, lo = _split3(g_ref[rows, :])
        b_sc[rows, :] = _dot(tri, hi) + _dot(tri, mid) + _dot(tri, lo)


def _gla_kernel(qf_ref, kf_ref, v0f_ref, v1f_ref, vf_ref, gf_ref, qb_ref, kb_ref, v0b_ref, v1b_ref, vb_ref, gb_ref,
                s0_ref, t16f_ref, t16b_ref, t64f_ref, t64b_ref, ind_ref, maskt_ref, qmask_ref,
                of_ref, ob_ref, sfin_ref, stf, stb, bf, bb, a_sc, qt_sc, kh_sc):
    t = pl.program_id(1)
    tt = qf_ref.shape[0]

    @pl.when(t == 0)
    def _():
        stf[...] = s0_ref[0]
        stb[...] = s0_ref[1]

    _cumsum_blocks(gf_ref, t64f_ref, bf)
    _cumsum_blocks(gb_ref, t64b_ref, bb)
    lowest = jnp.minimum(jnp.min(bf[...]), jnp.min(bb[...]))
    mask_t = maskt_ref[...]

    @pl.when(lowest > -GLA_SAFE_DECAY)
    def _():
        nch = tt // GLA_CHUNK
        qmask = qmask_ref[...]
        row = lax.broadcasted_iota(jnp.int32, (GLA_HEADS * GLA_CHUNK, GLA_CHUNK), 0) % GLA_CHUNK
        col = lax.broadcasted_iota(jnp.int32, (GLA_HEADS * GLA_CHUNK, GLA_CHUNK), 1)
        lower, upper = col <= row, col >= row

        off = lambda c: pl.multiple_of(c * GLA_CHUNK, GLA_CHUNK)

        def prep(c_f, c_b, slot):
            _chunk_prep(qf_ref, kf_ref, bf, a_sc, qt_sc, kh_sc, 0, slot, off(c_f), True, qmask, lower)
            _chunk_prep(qb_ref, kb_ref, bb, a_sc, qt_sc, kh_sc, 1, slot, off(c_b), False, qmask, upper)

        prep(0, nch - 1, 0)

        def step(i, carry):
            slot = i % 2
            nxt = jnp.minimum(i + 1, nch - 1)
            prep(nxt, nch - 1 - nxt, 1 - slot)
            _chunk_apply(vf_ref, bf, stf, of_ref, a_sc, qt_sc, kh_sc, 0, slot, off(i), True, mask_t)
            _chunk_apply(vb_ref, bb, stb, ob_ref, a_sc, qt_sc, kh_sc, 1, slot, off(nch - 1 - i), False, mask_t)
            return carry

        lax.fori_loop(0, nch, step, 0, unroll=4)

    @pl.when(jnp.logical_not(lowest > -GLA_SAFE_DECAY))
    def _():
        nblk = tt // GLA_BLOCK
        _cumsum_blocks(gf_ref, t16f_ref, bf)
        _cumsum_blocks(gb_ref, t16b_ref, bb)
        ind = ind_ref[...]

        def step(i, carry):
            rf = pl.multiple_of(i * GLA_BLOCK, GLA_BLOCK)
            rb = pl.multiple_of((nblk - 1 - i) * GLA_BLOCK, GLA_BLOCK)
            _gla_block(qf_ref, kf_ref, v0f_ref, v1f_ref, bf, stf, of_ref, rf, True, ind, mask_t)
            _gla_block(qb_ref, kb_ref, v0b_ref, v1b_ref, bb, stb, ob_ref, rb, False, ind, mask_t)
            return carry

        lax.fori_loop(0, nblk, step, 0)

    @pl.when(t == pl.num_programs(1) - 1)
    def _():
        sfin_ref[0] = stf[...]
        sfin_ref[1] = stb[...]


def _tri(span, blk):
    i = np.arange(span)
    same = (i[:, None] // blk) == (i[None, :] // blk)
    lower = (same & (i[None, :] <= i[:, None])).astype(np.float32)
    upper = (same & (i[None, :] >= i[:, None])).astype(np.float32)
    return jnp.asarray(lower, BF16), jnp.asarray(upper, BF16)


def _gla_call(qk, gv, gates, s0, tt):
    bsz, n, _ = qk.shape
    nt = n // tt
    span = min(tt, 256)
    t16f, t16b = _tri(span, GLA_BLOCK)
    t64f, t64b = _tri(span, GLA_CHUNK)
    kd = np.arange(GLA_KW) // GLA_DK
    ve = np.arange(GLA_VW) // GLA_DV
    ind = (kd[:, None] == ve[None, :]).astype(np.float32)
    qmask = (np.repeat(np.arange(GLA_HEADS), GLA_CHUNK)[:, None] == kd[None, :]).astype(np.float32)
    half = lambda rev, c: pl.BlockSpec((None, tt, LANES), lambda b, t: (b, nt - 1 - t if rev else t, c))
    full = lambda rev: pl.BlockSpec((None, tt, GLA_VW), lambda b, t: (b, nt - 1 - t if rev else t, 0))
    st = pl.BlockSpec((None, 2, GLA_VW, GLA_KW), lambda b, t: (b, 0, 0, 0))
    fwd_in = [half(False, 0), half(False, 1), half(False, 0), half(False, 1), full(False), half(False, 0)]
    bwd_in = [half(True, 0), half(True, 1), half(True, 0), half(True, 1), full(True), half(True, 1)]
    return pl.pallas_call(
        _gla_kernel,
        grid=(bsz, nt),
        in_specs=fwd_in + bwd_in + [st] + [_full((span, span))] * 4 + [
            _full((GLA_KW, GLA_VW)), _full((GLA_VW, GLA_KW)), _full((GLA_HEADS * GLA_CHUNK, GLA_KW))],
        out_specs=[full(False), full(True), st],
        out_shape=[jax.ShapeDtypeStruct((bsz, n, GLA_VW), F32), jax.ShapeDtypeStruct((bsz, n, GLA_VW), F32),
                   jax.ShapeDtypeStruct((bsz, 2, GLA_VW, GLA_KW), F32)],
        scratch_shapes=[pltpu.VMEM((GLA_VW, GLA_KW), F32), pltpu.VMEM((GLA_VW, GLA_KW), F32),
                        pltpu.VMEM((tt, GLA_KW), F32), pltpu.VMEM((tt, GLA_KW), F32),
                        pltpu.VMEM((2, 2, GLA_HEADS * GLA_CHUNK, GLA_CHUNK), BF16),
                        pltpu.VMEM((2, 2, GLA_CHUNK, GLA_KW), BF16), pltpu.VMEM((2, 2, GLA_CHUNK, GLA_KW), BF16)],
        compiler_params=_cparams(("parallel", "arbitrary")), name="gla",
    )(qk, qk, gv, gv, gv, gates, qk, qk, gv, gv, gv, gates, s0, t16f, t16b, t64f, t64b,
      jnp.asarray(ind, BF16), jnp.asarray(ind.T, F32), jnp.asarray(qmask, F32))


def _attn_kernel(q_ref, qn_ref, k_ref, vt_ref, o_ref, s_sc, m_sc, acc_sc, qt_sc, *, tk):
    nk = k_ref.shape[0] // tk
    i = pl.program_id(2)
    carry_scores = nk % 2 == 0
    par = i % 2 if carry_scores else 0

    def scores(j, slot, qslot):
        r0 = pl.multiple_of(j * tk, tk)
        s_sc[slot] = _dot(k_ref[pl.ds(r0, tk), :], qt_sc[qslot])

    def first_scores():
        qt_sc[0] = q_ref[...].T
        scores(0, 0, 0)

    if carry_scores:
        pl.when(i == 0)(first_scores)
    else:
        first_scores()
    m_sc[...] = jnp.full(m_sc.shape, -jnp.inf, F32)
    acc_sc[...] = jnp.zeros(acc_sc.shape, F32)

    def update(j, slot):
        s = s_sc[slot]
        m_prev = m_sc[...]
        m_new = jnp.maximum(m_prev, jnp.max(s, axis=0, keepdims=True))
        alpha = jnp.exp2(m_prev - m_new)
        p = jnp.exp2(s - m_new).astype(BF16)
        r0 = pl.multiple_of(j * tk, tk)
        acc_sc[...] = alpha * acc_sc[...] + _dot(vt_ref[:, pl.ds(r0, tk)], p)
        m_sc[...] = m_new

    def pair(t, carry):
        j = 2 * t
        scores(j + 1, 1, par)
        update(j, 0)
        scores(j + 2, 0, par)
        update(j + 1, 1)
        return carry

    lax.fori_loop(0, (nk - 1) // 2, pair, 0)
    if carry_scores:
        scores(nk - 1, 1, par)
        update(nk - 2, 0)
        qt_sc[1 - par] = qn_ref[...].T
        scores(0, 0, 1 - par)
        update(nk - 1, 1)
    else:
        update(nk - 1, 0)
    acc = acc_sc[...]
    o_t = acc[0:MLA_V] * (1.0 / acc[MLA_V:MLA_V + 1])
    o_ref[...] = o_t.T.astype(o_ref.dtype)


def _attn_call(q, k, vt, tq, tk, key_off, nk):
    bsz, nh, nq, dq = q.shape
    kb = key_off // nk
    nqt = nq // tq
    return pl.pallas_call(
        functools.partial(_attn_kernel, tk=tk),
        grid=(bsz, nh, nqt),
        in_specs=[pl.BlockSpec((None, None, tq, dq), lambda b, h, i: (b, h, i, 0)),
                  pl.BlockSpec((None, None, tq, dq), lambda b, h, i: (b, h, jnp.minimum(i + 1, nqt - 1), 0)),
                  pl.BlockSpec((None, None, nk, dq), lambda b, h, i: (b, h, kb, 0)),
                  pl.BlockSpec((None, None, MLA_VROWS, nk), lambda b, h, i: (b, h, 0, kb))],
        out_specs=pl.BlockSpec((None, None, tq, MLA_V), lambda b, h, i: (b, h, i, 0)),
        out_shape=jax.ShapeDtypeStruct((bsz, nh, nq, MLA_V), BF16),
        scratch_shapes=[pltpu.VMEM((2, tk, tq), F32), pltpu.VMEM((1, tq), F32), pltpu.VMEM((MLA_VROWS, tq), F32),
                        pltpu.VMEM((2, dq, tq), BF16)],
        compiler_params=_cparams(("parallel", "parallel", "arbitrary")), name="attn",
    )(q, q, k, vt)


def _out_kernel(x_ref, ysgu_ref, of_ref, ob_ref, gr_ref, ymla_ref, g1_ref, sh2_ref, sc2_ref, g2_ref, glaw_ref,
                hsum_ref, wout_ref, n2w_ref, w1_ref, w2_ref, fnw_ref, o_ref, y_sc, *, final_norm):
    o = of_ref[...] + ob_ref[...]
    o2 = o * o
    hi = o2.astype(BF16)
    lo = (o2 - hi.astype(F32)).astype(BF16)
    ms = (_dot(hi, hsum_ref[...]) + _dot(lo, hsum_ref[...])) * (1.0 / GLA_DV)
    y_gla = o * lax.rsqrt(ms + EPS) * glaw_ref[...] * _silu(gr_ref[...])
    y_sc[:, 0:SGU_WIDTH] = ysgu_ref[...]
    y_sc[:, SGU_WIDTH:SGU_WIDTH + GLA_VW] = y_gla.astype(BF16)
    for hh in range(MLA_HEADS):
        off = SGU_WIDTH + GLA_VW + hh * MLA_V
        y_sc[:, off:off + MLA_V] = ymla_ref[hh]
    x1 = x_ref[...] + g1_ref[...] * _dot(y_sc[...], wout_ref[...])
    hn = (_rms(x1, n2w_ref[...]) * (1.0 + sc2_ref[...]) + sh2_ref[...]).astype(BF16)
    dff = w1_ref.shape[1]
    fc = 1024
    acc = jnp.zeros(x1.shape, F32)
    for c in range(dff // fc):
        a = jnp.maximum(_dot(hn, w1_ref[:, c * fc:(c + 1) * fc]), 0.0)
        acc = acc + _dot((a * a).astype(BF16), w2_ref[c * fc:(c + 1) * fc, :])
    x2 = x1 + g2_ref[...] * acc
    if final_norm:
        x2 = _rms(x2, fnw_ref[...])
    o_ref[...] = x2


def _out_call(x, ysgu, o_f, o_b, gr, ymla, g1, sh2, sc2, g2, lw, fnw, tm, final_norm):
    bsz, n, d = x.shape
    nh = MLA_HEADS
    mix = SGU_WIDTH + GLA_VW + nh * MLA_V
    dff = lw["w1"].shape[1]
    tok = lambda w: pl.BlockSpec((None, tm, w), lambda b, i: (b, i, 0))
    vec = pl.BlockSpec((None, 1, d), lambda b, i: (b, 0, 0))
    once = lambda shape: pl.BlockSpec(shape, lambda *_: (0,) * len(shape), pipeline_mode=pl.Buffered(1))
    in_specs = [tok(d), tok(SGU_WIDTH), tok(GLA_VW), tok(GLA_VW), tok(GLA_VW),
                pl.BlockSpec((None, nh, tm, MLA_V), lambda b, i: (b, 0, i, 0)),
                vec, vec, vec, vec, _full((1, GLA_VW)), _full((GLA_VW, GLA_VW)),
                once((mix, d)), _full((1, d)), once((d, dff)), once((dff, d)), _full((1, d))]
    return pl.pallas_call(
        functools.partial(_out_kernel, final_norm=final_norm),
        grid=(bsz, n // tm), in_specs=in_specs, out_specs=tok(d),
        out_shape=jax.ShapeDtypeStruct((bsz, n, d), F32),
        scratch_shapes=[pltpu.VMEM((tm, mix), BF16)],
        compiler_params=_cparams(("parallel", "parallel")), name="out_ffn",
    )(x, ysgu, o_f, o_b, gr, ymla, g1, sh2, sc2, g2, lw["glaw"], lw["hsum"], lw["w_out"], lw["n2w"],
      lw["w1"], lw["w2"], fnw)


def _rope_tables(n):
    half = MLA_ROPE // 2
    pos = np.arange(n)
    freq = ROPE_BASE ** (-np.arange(half // 2, dtype=np.float64) * 2.0 / half)
    ang_r = (pos // GRID_W)[:, None] * freq[None, :]
    ang_c = (pos % GRID_W)[:, None] * freq[None, :]
    cos = np.zeros((n, LANES), np.float32)
    sin = np.zeros((n, LANES), np.float32)
    cos[:, 0:64] = np.concatenate([np.cos(ang_r), np.cos(ang_r), np.cos(ang_c), np.cos(ang_c)], axis=1)
    sin[:, 0:64] = np.concatenate([-np.sin(ang_r), np.sin(ang_r), -np.sin(ang_c), np.sin(ang_c)], axis=1)
    return jnp.asarray(cos), jnp.asarray(sin)


def _identity_tables(n):
    cos = np.zeros((n, LANES), np.float32)
    cos[:, 0:64] = 1.0
    return jnp.asarray(cos), jnp.zeros((n, LANES), F32)


_ROT_PARTNER = np.concatenate([np.arange(16, 32), np.arange(0, 16), np.arange(48, 64), np.arange(32, 48)])


def _layer_weights(p):
    w_in = p["w_in"]
    d = w_in.shape[0]
    o_gk, o_gv, o_gf, o_gb, o_ckv, o_kr, o_su, o_sv, o_gq, o_gr, o_dq = (
        0, 128, 384, 400, 416, 672, 736, 992, 1248, 1376, 1632)
    col = lambda o, w: w_in[:, o:o + w]
    kr = col(o_kr, MLA_ROPE)
    z = lambda w: jnp.zeros((d, w), w_in.dtype)
    w_in_p = jnp.concatenate([
        col(o_gq, 128), col(o_gk, 128),
        kr, col(o_gf, GLA_RANK), col(o_gb, GLA_RANK), z(32), kr[:, _ROT_PARTNER], z(64),
        col(o_gv, 256), col(o_gr, 256), col(o_su, 256), col(o_sv, 256),
        col(o_ckv, 256), col(o_dq, 256)], axis=1).astype(BF16)

    wg = jnp.zeros((LANES, 2 * GLA_KW), F32)
    wg = wg.at[64:80, 0:GLA_KW].set(p["gla_wg_fwd"]).at[80:96, GLA_KW:].set(p["gla_wg_bwd"])
    bg = jnp.concatenate([p["gla_bg_fwd"], p["gla_bg_bwd"]])[None, :]

    w_uq = p["mla_w_uq"].reshape(MLA_RANK, MLA_HEADS, MLA_NOPE + MLA_ROPE)
    nope = w_uq[:, :, :MLA_NOPE].reshape(MLA_RANK, -1)
    rope = w_uq[:, :, MLA_NOPE:]
    pad = lambda t: jnp.pad(t, ((0, 0), (0, 0), (0, LANES - MLA_ROPE))).reshape(MLA_RANK, -1)
    wuq = jnp.concatenate([nope, pad(rope), pad(rope[:, :, _ROT_PARTNER])], axis=1).astype(BF16)

    w_ukv = p["mla_w_ukv"].reshape(MLA_RANK, MLA_HEADS, MLA_NOPE + MLA_V)
    wukv = jnp.concatenate([w_ukv[:, :, :MLA_NOPE].reshape(MLA_RANK, -1),
                            w_ukv[:, :, MLA_NOPE:].reshape(MLA_RANK, -1)], axis=1).astype(BF16)

    hv = np.arange(GLA_VW) // GLA_DV
    return dict(
        n1w=p["norm1_w"][None, :], w_in=w_in_p,
        sgw=p["sgu_norm_w"][None, :], sgb=p["sgu_norm_b"][None, :],
        ws=p["sgu_w"].reshape(SGU_HEADS * SGU_CHUNK, SGU_CHUNK).astype(BF16),
        bs=jnp.repeat(p["sgu_b"].T, SGU_HEAD_DIM, axis=1),
        wg=wg.astype(BF16), bg=bg,
        qnw=p["mla_q_norm_w"][None, :], wuq=wuq, kvnw=p["mla_kv_norm_w"][None, :], wukv=wukv,
        glaw=jnp.tile(p["gla_norm_w"], GLA_HEADS)[None, :],
        hsum=jnp.asarray((hv[:, None] == hv[None, :]).astype(np.float32), BF16),
        w_out=p["w_out"].astype(BF16), n2w=p["norm2_w"][None, :],
        w1=p["w_ff1"].astype(BF16), w2=p["w_ff2"].astype(BF16))


def _pick(n, pref):
    for t in pref:
        if n % t == 0:
            return t
    return n


class _Tiles(NamedTuple):
    tok: int
    ffn: int
    ctx: int
    q: int
    kv: int


def _tiles(n, nc):
    return _Tiles(tok=_pick(n, (1024, 512, 256, 128)), ffn=_pick(n, (512, 256, 128)), ctx=_pick(nc, (256, 128)),
                  q=_pick(n, (1024, 512, 256, 128)), kv=_pick(n + nc, (1408, 768, 512, 256, 128)))


def kernel(x, c, ctx, c_ctx, w_mod, b_mod, norm1_w, w_in, w_out, sgu_norm_w, sgu_norm_b, sgu_w, sgu_b, gla_wg_fwd, gla_bg_fwd, gla_wg_bwd, gla_bg_bwd, gla_norm_w, mla_q_norm_w, mla_w_uq, mla_kv_norm_w, mla_w_ukv, norm2_w, w_ff1, w_ff2, final_norm_w):
    p = dict(norm1_w=norm1_w, w_in=w_in, w_out=w_out, sgu_norm_w=sgu_norm_w, sgu_norm_b=sgu_norm_b, sgu_w=sgu_w,
             sgu_b=sgu_b, gla_wg_fwd=gla_wg_fwd, gla_bg_fwd=gla_bg_fwd, gla_wg_bwd=gla_wg_bwd, gla_bg_bwd=gla_bg_bwd,
             gla_norm_w=gla_norm_w, mla_q_norm_w=mla_q_norm_w, mla_w_uq=mla_w_uq, mla_kv_norm_w=mla_kv_norm_w,
             mla_w_ukv=mla_w_ukv, norm2_w=norm2_w, w_ff1=w_ff1, w_ff2=w_ff2)
    bsz, n, d = x.shape
    nc = ctx.shape[1]
    depth = w_mod.shape[0]
    fnw = final_norm_w[None, :]

    cvecs = jnp.zeros((8, d), F32).at[0:bsz].set(c).at[bsz].set(c_ctx)
    mod = _mod_call(cvecs, w_mod, b_mod)

    t = _tiles(n, nc)
    cos, sin = _rope_tables(n)
    cos_c, sin_c = _identity_tables(nc)

    lw_all = jax.vmap(_layer_weights)(p)
    xc = ctx
    for l in range(depth):
        last = l == depth - 1
        lw = {name: w[l] for name, w in lw_all.items()}
        m = mod[l, 0:bsz].reshape(bsz, 1, 6, d)
        sh1, sc1, g1, sh2, sc2, g2 = [m[:, :, i] for i in range(6)]
        mc = jnp.broadcast_to(mod[l, bsz].reshape(1, 1, 6, d), (bsz, 1, 6, d))
        sh1c, sc1c, g1c, sh2c, sc2c, g2c = [mc[:, :, i] for i in range(6)]

        ysgu, qk, gv, gate, gr, q, k, vt = _proj_call(x, sh1, sc1, lw, cos, sin, t.tok, n + nc, 0)
        ysgu_c, qk_c, gv_c, gate_c, gr_c, q_c, k, vt = _proj_call(xc, sh1c, sc1c, lw, cos_c, sin_c, t.ctx, n + nc, n,
                                                                   kv_prev=(k, vt))
        zero = jnp.zeros((bsz, 2, GLA_VW, GLA_KW), F32)
        of_c, ob_c, s_ctx = _gla_call(qk_c, gv_c, gate_c, zero, t.ctx)
        o_f, o_b, _ = _gla_call(qk, gv, gate, s_ctx, t.tok)
        ymla = _attn_call(q, k, vt, t.q, t.kv, 0, n + nc)
        x = _out_call(x, ysgu, o_f, o_b, gr, ymla, g1, sh2, sc2, g2, lw, fnw, t.ffn, last)

        if not last:
            ymla_c = _attn_call(q_c, k, vt, t.ctx, t.ctx, n, nc)
            xc = _out_call(xc, ysgu_c, of_c, ob_c, gr_c, ymla_c, g1c, sh2c, sc2c, g2c, lw, fnw, t.ctx, False)
    return x
```

```python
import functools
import math
from typing import NamedTuple

import numpy as np
import jax
import jax.numpy as jnp
from jax import lax
from jax.experimental import pallas as pl
from jax.experimental.pallas import tpu as pltpu

F32 = jnp.float32
BF16 = jnp.bfloat16

EPS = 1e-6
GRID_W = 64
ROPE_BASE = 10000.0
LANES = 128
SGU_HEADS = 4
SGU_HEAD_DIM = 64
SGU_WIDTH = 256
SGU_CHUNK = 128
GLA_HEADS = 4
GLA_DK = 32
GLA_DV = 64
GLA_KW = 128
GLA_VW = 256
GLA_RANK = 16
GLA_TAU = 16.0
GLA_BLOCK = 16
GLA_CHUNK = 64
GLA_SAFE_DECAY = 60.0
MLA_HEADS = 4
MLA_RANK = 256
MLA_NOPE = 128
MLA_ROPE = 64
MLA_V = 128
MLA_QK = 256
MLA_VROWS = 144
VMEM_LIMIT = 56 * 1024 * 1024

C_QK_ROPE, C_GV_GR, C_SU_SV, C_CKV_DQ, C_END = 0, 512, 1024, 1536, 2048


def _cparams(sem):
    return pltpu.CompilerParams(dimension_semantics=sem, vmem_limit_bytes=VMEM_LIMIT)


def _dot(a, b):
    return jnp.dot(a, b, preferred_element_type=F32)


def _dot_nt(a, b):
    return lax.dot_general(a, b, (((1,), (1,)), ((), ())), preferred_element_type=F32)


def _dot_tn(a, b):
    return lax.dot_general(a, b, (((0,), (0,)), ((), ())), preferred_element_type=F32)


def _rms(x, w):
    return x * lax.rsqrt(jnp.mean(x * x, axis=-1, keepdims=True) + EPS) * w


def _gelu_tanh(x):
    return 0.5 * x * (1.0 + jnp.tanh(math.sqrt(2.0 / math.pi) * (x + 0.044715 * (x * x * x))))


def _silu(x):
    return x * jax.nn.sigmoid(x)


def _full(shape):
    return pl.BlockSpec(shape, lambda *_: (0,) * len(shape))


def _mod_kernel(c_ref, w_ref, b_ref, o_ref):
    s = _silu(c_ref[...]).astype(BF16)
    o_ref[...] = _dot(s, w_ref[...].astype(BF16)) + b_ref[...]


def _mod_call(cvecs, w_mod, b_mod):
    nl, d, d6 = w_mod.shape
    tn = 1536
    return pl.pallas_call(
        _mod_kernel,
        grid=(nl, d6 // tn),
        in_specs=[pl.BlockSpec((8, d), lambda l, j: (0, 0)),
                  pl.BlockSpec((None, d, tn), lambda l, j: (l, 0, j)),
                  pl.BlockSpec((None, 1, tn), lambda l, j: (l, 0, j))],
        out_specs=pl.BlockSpec((None, 8, tn), lambda l, j: (l, 0, j)),
        out_shape=jax.ShapeDtypeStruct((nl, 8, d6), F32),
        compiler_params=_cparams(("parallel", "parallel")),
        name="mod",
    )(cvecs, w_mod, b_mod.reshape(nl, 1, d6))


def _proj_kernel(x_ref, sh_ref, sc_ref, n1w_ref, win_ref, sgw_ref, sgb_ref, ws_ref, bs_ref, wg_ref, bg_ref,
                 qnw_ref, wuq_ref, kvnw_ref, wukv_ref, cos_ref, sin_ref, *rest, q_scale):
    ysgu_ref, qk_ref, gv_ref, gate_ref, gr_ref, q_ref, k_ref, vt_ref = rest[-8:]
    tm = x_ref.shape[0]
    h = _rms(x_ref[...], n1w_ref[...]) * (1.0 + sc_ref[...]) + sh_ref[...]
    hb = h.astype(BF16)

    def proj(lo, hi):
        return _dot(hb, win_ref[:, lo:hi])

    p2 = proj(C_SU_SV, C_CKV_DQ)
    p0 = proj(C_QK_ROPE, C_GV_GR)
    qk_ref[:, 0:GLA_KW] = p0[:, 0:GLA_KW] * (GLA_DK ** -0.5)
    qk_ref[:, GLA_KW:] = p0[:, GLA_KW:2 * GLA_KW]
    ga, gb = p0[:, 2 * GLA_KW:2 * GLA_KW + LANES], p0[:, 2 * GLA_KW + LANES:]
    p1 = proj(C_GV_GR, C_SU_SV)
    gv_ref[...] = p1[:, 0:GLA_VW]
    gr_ref[...] = p1[:, GLA_VW:]

    z = _dot(ga.astype(BF16), wg_ref[...]) + bg_ref[...]
    log_sig = jnp.minimum(z, 0.0) - jnp.log1p(jnp.exp(-jnp.abs(z)))
    gate_ref[...] = log_sig * (1.0 / GLA_TAU)
    cos = cos_ref[...]
    sin = sin_ref[...]
    k_rope = (ga * cos + gb * sin).astype(BF16)

    p3 = proj(C_CKV_DQ, C_END)
    ckv = _rms(p3[:, 0:MLA_RANK], kvnw_ref[...]).astype(BF16)
    kv = _dot(ckv, wukv_ref[...])
    ones_row = (lax.broadcasted_iota(jnp.int32, (MLA_VROWS - MLA_V, tm), 0) == 0).astype(BF16)
    for hh in range(MLA_HEADS):
        k_ref[hh, :, 0:MLA_NOPE] = kv[:, hh * MLA_NOPE:(hh + 1) * MLA_NOPE].astype(BF16)
        k_ref[hh, :, MLA_NOPE:] = k_rope
        off = MLA_HEADS * MLA_NOPE + hh * MLA_V
        vt_ref[hh, 0:MLA_V, :] = kv[:, off:off + MLA_V].T.astype(BF16)
        vt_ref[hh, MLA_V:, :] = ones_row

    cq = _rms(p3[:, MLA_RANK:], qnw_ref[...]).astype(BF16)
    qq = _dot(cq, wuq_ref[...])
    nq = MLA_HEADS * LANES
    for hh in range(MLA_HEADS):
        sl = slice(hh * LANES, (hh + 1) * LANES)
        q_ref[hh, :, 0:MLA_NOPE] = (qq[:, sl] * q_scale).astype(BF16)
        rot = qq[:, nq + hh * LANES:nq + (hh + 1) * LANES] * cos + qq[:, 2 * nq + hh * LANES:2 * nq + (hh + 1) * LANES] * sin
        q_ref[hh, :, MLA_NOPE:] = (rot * q_scale).astype(BF16)

    u = _gelu_tanh(p2[:, 0:SGU_WIDTH])
    g = _gelu_tanh(p2[:, SGU_WIDTH:])
    mu = jnp.mean(g, axis=-1, keepdims=True)
    gc = g - mu
    var = jnp.mean(gc * gc, axis=-1, keepdims=True)
    vln = (gc * lax.rsqrt(var + EPS) * sgw_ref[...] + sgb_ref[...]).astype(BF16)
    head_of_lane = lax.broadcasted_iota(jnp.int32, (SGU_CHUNK, SGU_WIDTH), 1) // SGU_HEAD_DIM
    ws = ws_ref[...]
    for c in range(0, tm // SGU_CHUNK, 2):
        r2 = _dot(ws, jnp.concatenate([vln[c * SGU_CHUNK:(c + 1) * SGU_CHUNK, :],
                                       vln[(c + 1) * SGU_CHUNK:(c + 2) * SGU_CHUNK, :]], axis=1))
        for half in range(2):
            rows = slice((c + half) * SGU_CHUNK, (c + half + 1) * SGU_CHUNK)
            r = r2[:, half * SGU_WIDTH:(half + 1) * SGU_WIDTH]
            s = r[0:SGU_CHUNK]
            for hh in range(1, SGU_HEADS):
                s = jnp.where(head_of_lane == hh, r[hh * SGU_CHUNK:(hh + 1) * SGU_CHUNK], s)
            ysgu_ref[rows, :] = (u[rows, :] * (s + bs_ref[...])).astype(BF16)


def _proj_call(x, sh, sc, lw, cos, sin, tm, nk_total, key_off, kv_prev=None):
    bsz, n, d = x.shape
    nh = MLA_HEADS
    kb = key_off // tm
    q_scale = (MLA_NOPE + MLA_ROPE) ** -0.5 * math.log2(math.e)
    tok = lambda w: pl.BlockSpec((None, tm, w), lambda b, i: (b, i, 0))
    vec = pl.BlockSpec((None, 1, d), lambda b, i: (b, 0, 0))
    heads = lambda w: pl.BlockSpec((None, nh, tm, w), lambda b, i: (b, 0, i, 0))
    tab = pl.BlockSpec((tm, LANES), lambda b, i: (i, 0))
    k_spec = pl.BlockSpec((None, nh, tm, MLA_QK), lambda b, i: (b, 0, kb + i, 0))
    vt_spec = pl.BlockSpec((None, nh, MLA_VROWS, tm), lambda b, i: (b, 0, 0, kb + i))
    in_specs = [tok(d), vec, vec, _full((1, d)), _full((d, C_END)),
                _full((1, SGU_WIDTH)), _full((1, SGU_WIDTH)), _full((SGU_HEADS * SGU_CHUNK, SGU_CHUNK)),
                _full((SGU_CHUNK, SGU_WIDTH)), _full((LANES, 2 * GLA_KW)), _full((1, 2 * GLA_KW)),
                _full((1, MLA_RANK)), _full((MLA_RANK, 3 * nh * LANES)),
                _full((1, MLA_RANK)), _full((MLA_RANK, nh * (MLA_NOPE + MLA_V))), tab, tab]
    out_specs = [tok(SGU_WIDTH), tok(2 * GLA_KW), tok(GLA_VW), tok(2 * GLA_KW), tok(GLA_VW),
                 heads(MLA_QK), k_spec, vt_spec]
    out_shape = [jax.ShapeDtypeStruct((bsz, n, SGU_WIDTH), BF16),
                 jax.ShapeDtypeStruct((bsz, n, 2 * GLA_KW), F32),
                 jax.ShapeDtypeStruct((bsz, n, GLA_VW), F32),
                 jax.ShapeDtypeStruct((bsz, n, 2 * GLA_KW), F32),
                 jax.ShapeDtypeStruct((bsz, n, GLA_VW), F32),
                 jax.ShapeDtypeStruct((bsz, nh, n, MLA_QK), BF16),
                 jax.ShapeDtypeStruct((bsz, nh, nk_total, MLA_QK), BF16),
                 jax.ShapeDtypeStruct((bsz, nh, MLA_VROWS, nk_total), BF16)]
    args = [x, sh, sc, lw["n1w"], lw["w_in"], lw["sgw"], lw["sgb"], lw["ws"], lw["bs"], lw["wg"], lw["bg"],
            lw["qnw"], lw["wuq"], lw["kvnw"], lw["wukv"], cos, sin]
    aliases = {}
    if kv_prev is not None:
        aliases = {len(args): 6, len(args) + 1: 7}
        in_specs = in_specs + [pl.BlockSpec(memory_space=pl.ANY)] * 2
        args = args + list(kv_prev)
    return pl.pallas_call(
        functools.partial(_proj_kernel, q_scale=q_scale),
        grid=(bsz, n // tm), in_specs=in_specs, out_specs=out_specs, out_shape=out_shape,
        input_output_aliases=aliases,
        compiler_params=_cparams(("parallel", "parallel")), name="proj",
    )(*args)


def _split3(x):
    hi = x.astype(BF16)
    r1 = x - hi.astype(F32)
    mid = r1.astype(BF16)
    lo = (r1 - mid.astype(F32)).astype(BF16)
    return hi, mid, lo


def _row_bcast(ref, r, n):
    return ref[pl.ds(r, n, stride=0), :]


def _gla_block(q_ref, k_ref, v0_ref, v1_ref, b_ref, st_ref, o_ref, r0, fwd, ind, mask_t):
    nb = GLA_BLOCK
    rows = pl.ds(r0, nb)
    bcast = lambda ref, r: _row_bcast(ref, r, nb)
    q = q_ref[rows, :]
    k = k_ref[rows, :]
    v = jnp.concatenate([v0_ref[rows, :], v1_ref[rows, :]], axis=1)
    b = b_ref[rows, :]
    b_end = bcast(b_ref, r0 + (nb - 1) if fwd else r0)
    st = st_ref[...]
    o = _dot_nt((q * jnp.exp(b)).astype(BF16), st.astype(BF16))
    kh = (k * jnp.exp(b_end - b)).astype(BF16)
    st_ref[...] = st * jnp.exp(b_end[0:1, :]) + mask_t * _dot_tn(v.astype(BF16), kh)

    row = lax.broadcasted_iota(jnp.int32, (nb, GLA_KW), 0)
    parts = []
    for j in range(nb):
        valid = (row >= j) if fwd else (row <= j)
        e = jnp.exp(jnp.where(valid, b - bcast(b_ref, r0 + j), -jnp.inf))
        parts.append((e * q * bcast(k_ref, r0 + j)).astype(BF16))
    r = _dot(jnp.concatenate(parts, axis=0), ind)
    for j in range(nb):
        vj = jnp.concatenate([bcast(v0_ref, r0 + j), bcast(v1_ref, r0 + j)], axis=1)
        o = o + r[j * nb:(j + 1) * nb] * vj
    o_ref[rows, :] = o


def _chunk_prep(q_ref, k_ref, b_ref, a_sc, qt_sc, kh_sc, d, slot, r0, fwd, qmask, tri):
    nc = GLA_CHUNK
    rows = pl.ds(r0, nc)
    b = b_ref[rows, :]
    e_end = jnp.exp(b_ref[pl.ds(r0 + (nc - 1) if fwd else r0, 1), :])
    qt = q_ref[rows, :] * jnp.exp(b)
    kt = k_ref[rows, :] * jnp.exp(-b)
    q4 = (jnp.concatenate([qt] * GLA_HEADS, axis=0) * qmask).astype(BF16)
    a_sc[d, slot] = jnp.where(tri, _dot_nt(q4, kt.astype(BF16)), 0.0).astype(BF16)
    qt_sc[d, slot] = qt.astype(BF16)
    kh_sc[d, slot] = (kt * e_end).astype(BF16)


def _chunk_apply(v_ref, b_ref, st_ref, o_ref, a_sc, qt_sc, kh_sc, d, slot, r0, fwd, mask_t):
    nc = GLA_CHUNK
    rows = pl.ds(r0, nc)
    vb = v_ref[rows, :].astype(BF16)
    e_end = jnp.exp(b_ref[pl.ds(r0 + (nc - 1) if fwd else r0, 1), :])
    st = st_ref[...]
    o = _dot_nt(qt_sc[d, slot], st.astype(BF16))
    st_ref[...] = st * e_end + mask_t * _dot_tn(vb, kh_sc[d, slot])
    r = _dot(a_sc[d, slot], vb)
    head_of_lane = lax.broadcasted_iota(jnp.int32, (nc, GLA_VW), 1) // GLA_DV
    intra = r[0:nc]
    for h in range(1, GLA_HEADS):
        intra = jnp.where(head_of_lane == h, r[h * nc:(h + 1) * nc], intra)
    o_ref[rows, :] = o + intra


def _cumsum_blocks(g_ref, tri_ref, b_sc):
    tri = tri_ref[...]
    span = tri.shape[0]
    for c in range(g_ref.shape[0] // span):
        rows = slice(c * span, (c + 1) * span)
        hi, mid, lo = _split3(g_ref[rows, :])
        b_sc[rows, :] = _dot(tri, hi) + _dot(tri, mid) + _dot(tri, lo)


def _gla_kernel(qf_ref, kf_ref, v0f_ref, v1f_ref, vf_ref, gf_ref, qb_ref, kb_ref, v0b_ref, v1b_ref, vb_ref, gb_ref,
                s0_ref, t16f_ref, t16b_ref, t64f_ref, t64b_ref, ind_ref, maskt_ref, qmask_ref,
                of_ref, ob_ref, sfin_ref, stf, stb, bf, bb, a_sc, qt_sc, kh_sc):
    t = pl.program_id(1)
    tt = qf_ref.shape[0]

    @pl.when(t == 0)
    def _():
        stf[...] = s0_ref[0]
        stb[...] = s0_ref[1]

    _cumsum_blocks(gf_ref, t64f_ref, bf)
    _cumsum_blocks(gb_ref, t64b_ref, bb)
    lowest = jnp.minimum(jnp.min(bf[...]), jnp.min(bb[...]))
    mask_t = maskt_ref[...]

    @pl.when(lowest > -GLA_SAFE_DECAY)
    def _():
        nch = tt // GLA_CHUNK
        qmask = qmask_ref[...]
        row = lax.broadcasted_iota(jnp.int32, (GLA_HEADS * GLA_CHUNK, GLA_CHUNK), 0) % GLA_CHUNK
        col = lax.broadcasted_iota(jnp.int32, (GLA_HEADS * GLA_CHUNK, GLA_CHUNK), 1)
        lower, upper = col <= row, col >= row

        off = lambda c: pl.multiple_of(c * GLA_CHUNK, GLA_CHUNK)

        def prep(c_f, c_b, slot):
            _chunk_prep(qf_ref, kf_ref, bf, a_sc, qt_sc, kh_sc, 0, slot, off(c_f), True, qmask, lower)
            _chunk_prep(qb_ref, kb_ref, bb, a_sc, qt_sc, kh_sc, 1, slot, off(c_b), False, qmask, upper)

        prep(0, nch - 1, 0)

        def step(i, carry):
            slot = i % 2
            nxt = jnp.minimum(i + 1, nch - 1)
            prep(nxt, nch - 1 - nxt, 1 - slot)
            _chunk_apply(vf_ref, bf, stf, of_ref, a_sc, qt_sc, kh_sc, 0, slot, off(i), True, mask_t)
            _chunk_apply(vb_ref, bb, stb, ob_ref, a_sc, qt_sc, kh_sc, 1, slot, off(nch - 1 - i), False, mask_t)
            return carry

        lax.fori_loop(0, nch, step, 0, unroll=4)

    @pl.when(jnp.logical_not(lowest > -GLA_SAFE_DECAY))
    def _():
        nblk = tt // GLA_BLOCK
        _cumsum_blocks(gf_ref, t16f_ref, bf)
        _cumsum_blocks(gb_ref, t16b_ref, bb)
        ind = ind_ref[...]

        def step(i, carry):
            rf = pl.multiple_of(i * GLA_BLOCK, GLA_BLOCK)
            rb = pl.multiple_of((nblk - 1 - i) * GLA_BLOCK, GLA_BLOCK)
            _gla_block(qf_ref, kf_ref, v0f_ref, v1f_ref, bf, stf, of_ref, rf, True, ind, mask_t)
            _gla_block(qb_ref, kb_ref, v0b_ref, v1b_ref, bb, stb, ob_ref, rb, False, ind, mask_t)
            return carry

        lax.fori_loop(0, nblk, step, 0)

    @pl.when(t == pl.num_programs(1) - 1)
    def _():
        sfin_ref[0] = stf[...]
        sfin_ref[1] = stb[...]


def _tri(span, blk):
    i = np.arange(span)
    same = (i[:, None] // blk) == (i[None, :] // blk)
    lower = (same & (i[None, :] <= i[:, None])).astype(np.float32)
    upper = (same & (i[None, :] >= i[:, None])).astype(np.float32)
    return jnp.asarray(lower, BF16), jnp.asarray(upper, BF16)


def _gla_call(qk, gv, gates, s0, tt):
    bsz, n, _ = qk.shape
    nt = n // tt
    span = min(tt, 256)
    t16f, t16b = _tri(span, GLA_BLOCK)
    t64f, t64b = _tri(span, GLA_CHUNK)
    kd = np.arange(GLA_KW) // GLA_DK
    ve = np.arange(GLA_VW) // GLA_DV
    ind = (kd[:, None] == ve[None, :]).astype(np.float32)
    qmask = (np.repeat(np.arange(GLA_HEADS), GLA_CHUNK)[:, None] == kd[None, :]).astype(np.float32)
    half = lambda rev, c: pl.BlockSpec((None, tt, LANES), lambda b, t: (b, nt - 1 - t if rev else t, c))
    full = lambda rev: pl.BlockSpec((None, tt, GLA_VW), lambda b, t: (b, nt - 1 - t if rev else t, 0))
    st = pl.BlockSpec((None, 2, GLA_VW, GLA_KW), lambda b, t: (b, 0, 0, 0))
    fwd_in = [half(False, 0), half(False, 1), half(False, 0), half(False, 1), full(False), half(False, 0)]
    bwd_in = [half(True, 0), half(True, 1), half(True, 0), half(True, 1), full(True), half(True, 1)]
    return pl.pallas_call(
        _gla_kernel,
        grid=(bsz, nt),
        in_specs=fwd_in + bwd_in + [st] + [_full((span, span))] * 4 + [
            _full((GLA_KW, GLA_VW)), _full((GLA_VW, GLA_KW)), _full((GLA_HEADS * GLA_CHUNK, GLA_KW))],
        out_specs=[full(False), full(True), st],
        out_shape=[jax.ShapeDtypeStruct((bsz, n, GLA_VW), F32), jax.ShapeDtypeStruct((bsz, n, GLA_VW), F32),
                   jax.ShapeDtypeStruct((bsz, 2, GLA_VW, GLA_KW), F32)],
        scratch_shapes=[pltpu.VMEM((GLA_VW, GLA_KW), F32), pltpu.VMEM((GLA_VW, GLA_KW), F32),
                        pltpu.VMEM((tt, GLA_KW), F32), pltpu.VMEM((tt, GLA_KW), F32),
                        pltpu.VMEM((2, 2, GLA_HEADS * GLA_CHUNK, GLA_CHUNK), BF16),
                        pltpu.VMEM((2, 2, GLA_CHUNK, GLA_KW), BF16), pltpu.VMEM((2, 2, GLA_CHUNK, GLA_KW), BF16)],
        compiler_params=_cparams(("parallel", "arbitrary")), name="gla",
    )(qk, qk, gv, gv, gv, gates, qk, qk, gv, gv, gv, gates, s0, t16f, t16b, t64f, t64b,
      jnp.asarray(ind, BF16), jnp.asarray(ind.T, F32), jnp.asarray(qmask, F32))


def _attn_kernel(q_ref, qn_ref, k_ref, vt_ref, o_ref, s_sc, m_sc, acc_sc, qt_sc, *, tk):
    nk = k_ref.shape[0] // tk
    i = pl.program_id(2)
    carry_scores = nk % 2 == 0
    par = i % 2 if carry_scores else 0

    def scores(j, slot, qslot):
        r0 = pl.multiple_of(j * tk, tk)
        s_sc[slot] = _dot(k_ref[pl.ds(r0, tk), :], qt_sc[qslot])

    def first_scores():
        qt_sc[0] = q_ref[...].T
        scores(0, 0, 0)

    if carry_scores:
        pl.when(i == 0)(first_scores)
    else:
        first_scores()
    m_sc[...] = jnp.full(m_sc.shape, -jnp.inf, F32)
    acc_sc[...] = jnp.zeros(acc_sc.shape, F32)

    def update(j, slot):
        s = s_sc[slot]
        m_prev = m_sc[...]
        m_new = jnp.maximum(m_prev, jnp.max(s, axis=0, keepdims=True))
        alpha = jnp.exp2(m_prev - m_new)
        p = jnp.exp2(s - m_new).astype(BF16)
        r0 = pl.multiple_of(j * tk, tk)
        acc_sc[...] = alpha * acc_sc[...] + _dot(vt_ref[:, pl.ds(r0, tk)], p)
        m_sc[...] = m_new

    def pair(t, carry):
        j = 2 * t
        scores(j + 1, 1, par)
        update(j, 0)
        scores(j + 2, 0, par)
        update(j + 1, 1)
        return carry

    lax.fori_loop(0, (nk - 1) // 2, pair, 0)
    if carry_scores:
        scores(nk - 1, 1, par)
        update(nk - 2, 0)
        qt_sc[1 - par] = qn_ref[...].T
        scores(0, 0, 1 - par)
        update(nk - 1, 1)
    else:
        update(nk - 1, 0)
    acc = acc_sc[...]
    o_t = acc[0:MLA_V] * (1.0 / acc[MLA_V:MLA_V + 1])
    o_ref[...] = o_t.T.astype(o_ref.dtype)


def _attn_call(q, k, vt, tq, tk, key_off, nk):
    bsz, nh, nq, dq = q.shape
    kb = key_off // nk
    nqt = nq // tq
    return pl.pallas_call(
        functools.partial(_attn_kernel, tk=tk),
        grid=(bsz, nh, nqt),
        in_specs=[pl.BlockSpec((None, None, tq, dq), lambda b, h, i: (b, h, i, 0)),
                  pl.BlockSpec((None, None, tq, dq), lambda b, h, i: (b, h, jnp.minimum(i + 1, nqt - 1), 0)),
                  pl.BlockSpec((None, None, nk, dq), lambda b, h, i: (b, h, kb, 0)),
                  pl.BlockSpec((None, None, MLA_VROWS, nk), lambda b, h, i: (b, h, 0, kb))],
        out_specs=pl.BlockSpec((None, None, tq, MLA_V), lambda b, h, i: (b, h, i, 0)),
        out_shape=jax.ShapeDtypeStruct((bsz, nh, nq, MLA_V), BF16),
        scratch_shapes=[pltpu.VMEM((2, tk, tq), F32), pltpu.VMEM((1, tq), F32), pltpu.VMEM((MLA_VROWS, tq), F32),
                        pltpu.VMEM((2, dq, tq), BF16)],
        compiler_params=_cparams(("parallel", "parallel", "arbitrary")), name="attn",
    )(q, q, k, vt)


def _out_kernel(x_ref, ysgu_ref, of_ref, ob_ref, gr_ref, ymla_ref, g1_ref, sh2_ref, sc2_ref, g2_ref, glaw_ref,
                hsum_ref, wout_ref, n2w_ref, w1_ref, w2_ref, fnw_ref, o_ref, y_sc, *, final_norm):
    o = of_ref[...] + ob_ref[...]
    o2 = o * o
    hi = o2.astype(BF16)
    lo = (o2 - hi.astype(F32)).astype(BF16)
    ms = (_dot(hi, hsum_ref[...]) + _dot(lo, hsum_ref[...])) * (1.0 / GLA_DV)
    y_gla = o * lax.rsqrt(ms + EPS) * glaw_ref[...] * _silu(gr_ref[...])
    y_sc[:, 0:SGU_WIDTH] = ysgu_ref[...]
    y_sc[:, SGU_WIDTH:SGU_WIDTH + GLA_VW] = y_gla.astype(BF16)
    for hh in range(MLA_HEADS):
        off = SGU_WIDTH + GLA_VW + hh * MLA_V
        y_sc[:, off:off + MLA_V] = ymla_ref[hh]
    x1 = x_ref[...] + g1_ref[...] * _dot(y_sc[...], wout_ref[...])
    hn = (_rms(x1, n2w_ref[...]) * (1.0 + sc2_ref[...]) + sh2_ref[...]).astype(BF16)
    dff = w1_ref.shape[1]
    fc = 1024
    acc = jnp.zeros(x1.shape, F32)
    for c in range(dff // fc):
        a = jnp.maximum(_dot(hn, w1_ref[:, c * fc:(c + 1) * fc]), 0.0)
        acc = acc + _dot((a * a).astype(BF16), w2_ref[c * fc:(c + 1) * fc, :])
    x2 = x1 + g2_ref[...] * acc
    if final_norm:
        x2 = _rms(x2, fnw_ref[...])
    o_ref[...] = x2


def _out_call(x, ysgu, o_f, o_b, gr, ymla, g1, sh2, sc2, g2, lw, fnw, tm, final_norm):
    bsz, n, d = x.shape
    nh = MLA_HEADS
    mix = SGU_WIDTH + GLA_VW + nh * MLA_V
    dff = lw["w1"].shape[1]
    tok = lambda w: pl.BlockSpec((None, tm, w), lambda b, i: (b, i, 0))
    vec = pl.BlockSpec((None, 1, d), lambda b, i: (b, 0, 0))
    once = lambda shape: pl.BlockSpec(shape, lambda *_: (0,) * len(shape), pipeline_mode=pl.Buffered(1))
    in_specs = [tok(d), tok(SGU_WIDTH), tok(GLA_VW), tok(GLA_VW), tok(GLA_VW),
                pl.BlockSpec((None, nh, tm, MLA_V), lambda b, i: (b, 0, i, 0)),
                vec, vec, vec, vec, _full((1, GLA_VW)), _full((GLA_VW, GLA_VW)),
                once((mix, d)), _full((1, d)), once((d, dff)), once((dff, d)), _full((1, d))]
    return pl.pallas_call(
        functools.partial(_out_kernel, final_norm=final_norm),
        grid=(bsz, n // tm), in_specs=in_specs, out_specs=tok(d),
        out_shape=jax.ShapeDtypeStruct((bsz, n, d), F32),
        scratch_shapes=[pltpu.VMEM((tm, mix), BF16)],
        compiler_params=_cparams(("parallel", "parallel")), name="out_ffn",
    )(x, ysgu, o_f, o_b, gr, ymla, g1, sh2, sc2, g2, lw["glaw"], lw["hsum"], lw["w_out"], lw["n2w"],
      lw["w1"], lw["w2"], fnw)


def _rope_tables(n):
    half = MLA_ROPE // 2
    pos = np.arange(n)
    freq = ROPE_BASE ** (-np.arange(half // 2, dtype=np.float64) * 2.0 / half)
    ang_r = (pos // GRID_W)[:, None] * freq[None, :]
    ang_c = (pos % GRID_W)[:, None] * freq[None, :]
    cos = np.zeros((n, LANES), np.float32)
    sin = np.zeros((n, LANES), np.float32)
    cos[:, 0:64] = np.concatenate([np.cos(ang_r), np.cos(ang_r), np.cos(ang_c), np.cos(ang_c)], axis=1)
    sin[:, 0:64] = np.concatenate([-np.sin(ang_r), np.sin(ang_r), -np.sin(ang_c), np.sin(ang_c)], axis=1)
    return jnp.asarray(cos), jnp.asarray(sin)


def _identity_tables(n):
    cos = np.zeros((n, LANES), np.float32)
    cos[:, 0:64] = 1.0
    return jnp.asarray(cos), jnp.zeros((n, LANES), F32)


def _rot_partner(t):
    return jnp.concatenate([t[..., 16:32], t[..., 0:16], t[..., 48:64], t[..., 32:48]], axis=-1)


def _layer_weights(p):
    w_in = p["w_in"]
    d = w_in.shape[0]
    o_gk, o_gv, o_gf, o_gb, o_ckv, o_kr, o_su, o_sv, o_gq, o_gr, o_dq = (
        0, 128, 384, 400, 416, 672, 736, 992, 1248, 1376, 1632)
    col = lambda o, w: w_in[:, o:o + w]
    kr = col(o_kr, MLA_ROPE)
    z = lambda w: jnp.zeros((d, w), w_in.dtype)
    w_in_p = jnp.concatenate([
        col(o_gq, 128), col(o_gk, 128),
        kr, col(o_gf, GLA_RANK), col(o_gb, GLA_RANK), z(32), _rot_partner(kr), z(64),
        col(o_gv, 256), col(o_gr, 256), col(o_su, 256), col(o_sv, 256),
        col(o_ckv, 256), col(o_dq, 256)], axis=1).astype(BF16)

    zg = jnp.zeros((GLA_RANK, GLA_KW), F32)
    wg = jnp.concatenate([jnp.concatenate([p["gla_wg_fwd"], zg], axis=1),
                          jnp.concatenate([zg, p["gla_wg_bwd"]], axis=1)], axis=0)
    wg = jnp.pad(wg, ((MLA_ROPE, LANES - MLA_ROPE - 2 * GLA_RANK), (0, 0)))
    bg = jnp.concatenate([p["gla_bg_fwd"], p["gla_bg_bwd"]])[None, :]

    w_uq = p["mla_w_uq"].reshape(MLA_RANK, MLA_HEADS, MLA_NOPE + MLA_ROPE)
    nope = w_uq[:, :, :MLA_NOPE].reshape(MLA_RANK, -1)
    rope = w_uq[:, :, MLA_NOPE:]
    pad = lambda t: jnp.pad(t, ((0, 0), (0, 0), (0, LANES - MLA_ROPE))).reshape(MLA_RANK, -1)
    wuq = jnp.concatenate([nope, pad(rope), pad(_rot_partner(rope))], axis=1).astype(BF16)

    w_ukv = p["mla_w_ukv"].reshape(MLA_RANK, MLA_HEADS, MLA_NOPE + MLA_V)
    wukv = jnp.concatenate([w_ukv[:, :, :MLA_NOPE].reshape(MLA_RANK, -1),
                            w_ukv[:, :, MLA_NOPE:].reshape(MLA_RANK, -1)], axis=1).astype(BF16)

    hv = np.arange(GLA_VW) // GLA_DV
    return dict(
        n1w=p["norm1_w"][None, :], w_in=w_in_p,
        sgw=p["sgu_norm_w"][None, :], sgb=p["sgu_norm_b"][None, :],
        ws=p["sgu_w"].reshape(SGU_HEADS * SGU_CHUNK, SGU_CHUNK).astype(BF16),
        bs=jnp.repeat(p["sgu_b"].T, SGU_HEAD_DIM, axis=1),
        wg=wg.astype(BF16), bg=bg,
        qnw=p["mla_q_norm_w"][None, :], wuq=wuq, kvnw=p["mla_kv_norm_w"][None, :], wukv=wukv,
        glaw=jnp.tile(p["gla_norm_w"], GLA_HEADS)[None, :],
        hsum=jnp.asarray((hv[:, None] == hv[None, :]).astype(np.float32), BF16),
        w_out=p["w_out"].astype(BF16), n2w=p["norm2_w"][None, :],
        w1=p["w_ff1"].astype(BF16), w2=p["w_ff2"].astype(BF16))


def _pick(n, pref):
    for t in pref:
        if n % t == 0:
            return t
    return n


class _Tiles(NamedTuple):
    tok: int
    ffn: int
    ctx: int
    q: int
    kv: int


def _tiles(n, nc):
    return _Tiles(tok=_pick(n, (1024, 512, 256, 128)), ffn=_pick(n, (512, 256, 128)), ctx=_pick(nc, (256, 128)),
                  q=_pick(n, (1024, 512, 256, 128)), kv=_pick(n + nc, (1408, 768, 512, 256, 128)))


def kernel(x, c, ctx, c_ctx, w_mod, b_mod, norm1_w, w_in, w_out, sgu_norm_w, sgu_norm_b, sgu_w, sgu_b, gla_wg_fwd, gla_bg_fwd, gla_wg_bwd, gla_bg_bwd, gla_norm_w, mla_q_norm_w, mla_w_uq, mla_kv_norm_w, mla_w_ukv, norm2_w, w_ff1, w_ff2, final_norm_w):
    p = dict(norm1_w=norm1_w, w_in=w_in, w_out=w_out, sgu_norm_w=sgu_norm_w, sgu_norm_b=sgu_norm_b, sgu_w=sgu_w,
             sgu_b=sgu_b, gla_wg_fwd=gla_wg_fwd, gla_bg_fwd=gla_bg_fwd, gla_wg_bwd=gla_wg_bwd, gla_bg_bwd=gla_bg_bwd,
             gla_norm_w=gla_norm_w, mla_q_norm_w=mla_q_norm_w, mla_w_uq=mla_w_uq, mla_kv_norm_w=mla_kv_norm_w,
             mla_w_ukv=mla_w_ukv, norm2_w=norm2_w, w_ff1=w_ff1, w_ff2=w_ff2)
    bsz, n, d = x.shape
    nc = ctx.shape[1]
    depth = w_mod.shape[0]
    fnw = final_norm_w[None, :]

    cvecs = jnp.zeros((8, d), F32).at[0:bsz].set(c).at[bsz].set(c_ctx)
    mod = _mod_call(cvecs, w_mod, b_mod)

    t = _tiles(n, nc)
    cos, sin = _rope_tables(n)
    cos_c, sin_c = _identity_tables(nc)

    lw_all = jax.vmap(_layer_weights)(p)
    xc = ctx
    for l in range(depth):
        last = l == depth - 1
        lw = {name: w[l] for name, w in lw_all.items()}
        m = mod[l, 0:bsz].reshape(bsz, 1, 6, d)
        sh1, sc1, g1, sh2, sc2, g2 = [m[:, :, i] for i in range(6)]
        mc = jnp.broadcast_to(mod[l, bsz].reshape(1, 1, 6, d), (bsz, 1, 6, d))
        sh1c, sc1c, g1c, sh2c, sc2c, g2c = [mc[:, :, i] for i in range(6)]

        ysgu, qk, gv, gate, gr, q, k, vt = _proj_call(x, sh1, sc1, lw, cos, sin, t.tok, n + nc, 0)
        ysgu_c, qk_c, gv_c, gate_c, gr_c, q_c, k, vt = _proj_call(xc, sh1c, sc1c, lw, cos_c, sin_c, t.ctx, n + nc, n,
                                                                   kv_prev=(k, vt))
        zero = jnp.zeros((bsz, 2, GLA_VW, GLA_KW), F32)
        of_c, ob_c, s_ctx = _gla_call(qk_c, gv_c, gate_c, zero, t.ctx)
        o_f, o_b, _ = _gla_call(qk, gv, gate, s_ctx, t.tok)
        ymla = _attn_call(q, k, vt, t.q, t.kv, 0, n + nc)
        x = _out_call(x, ysgu, o_f, o_b, gr, ymla, g1, sh2, sc2, g2, lw, fnw, t.ffn, last)

        if not last:
            ymla_c = _attn_call(q_c, k, vt, t.ctx, t.ctx, n, nc)
            xc = _out_call(xc, ysgu_c, of_c, ob_c, gr_c, ymla_c, g1c, sh2c, sc2c, g2c, lw, fnw, t.ctx, False)
    return x
```
